```python
import math
import jax
import jax.numpy as jnp
from jax import lax

D_MODEL = 2048
BATCH = 8
SEQ = 4096
DEPTH = 2
DEC_BATCH = 32
DEC_SEQ = 16
PAST_LEN = 1024

CHUNK = 64
Q_BLOCK = 128
RMS_EPS = 1e-6
NEG_INF = -1e30
LB_FLOOR = 1e-30

MLA_WIDTH = D_MODEL // 2
V_HEAD = 128
MLA_HEADS = MLA_WIDTH // V_HEAD
QK_NOPE = 128
QK_ROPE = 64
Q_LORA = 512
KV_LORA = 256
ROPE_THETA = 10000.0
ATTN_SCALE = (QK_NOPE + QK_ROPE) ** -0.5

HG_WIDTH = D_MODEL - MLA_WIDTH
HG_DK = 128
HG_DV = 128
HG_HEADS = HG_WIDTH // HG_DV
HG_KEY = HG_HEADS * HG_DK

MIX_WIDTH = MLA_WIDTH + HG_WIDTH
IN_SIZES = (Q_LORA, KV_LORA, QK_ROPE, HG_KEY, HG_KEY, HG_WIDTH, HG_WIDTH)
IN_DIM = Q_LORA + KV_LORA + QK_ROPE + 2 * HG_KEY + 2 * HG_WIDTH

D_FF = 5632
N_EXPERTS = 8
TOP_K = 2
MOE_FF = 2816
N_DENSE = (DEPTH + 1) // 2
N_MOE = DEPTH // 2

kernel_name = 'hybrid_mla_hgrn2_streaming_step'


def rms_norm(x, w):
    xf = x.astype(jnp.float32)
    y = xf * lax.rsqrt(jnp.mean(xf * xf, axis=-1, keepdims=True) + RMS_EPS)
    return (y * w.astype(jnp.float32)).astype(x.dtype)


def apply_rope(x, pos):
    half = x.shape[-1] // 2
    inv_freq = jnp.exp(jnp.arange(half, dtype=jnp.float32) * (-math.log(ROPE_THETA) / half))
    ang = pos.astype(jnp.float32)[:, None] * inv_freq[None, :]
    shape = (1, pos.shape[0]) + (1,) * (x.ndim - 3) + (half,)
    cos = jnp.cos(ang).reshape(shape)
    sin = jnp.sin(ang).reshape(shape)
    xf = x.astype(jnp.float32)
    x1, x2 = xf[..., :half], xf[..., half:]
    return jnp.concatenate([x1 * cos - x2 * sin, x2 * cos + x1 * sin], axis=-1).astype(x.dtype)


def split_columns(p):
    parts, start = [], 0
    for size in IN_SIZES:
        parts.append(p[..., start:start + size])
        start += size
    return parts


def mla_attention(q_lat, q_pe, ckv, kpe, q_pos, k_pos):
    B, L, H, C = q_lat.shape
    qb = min(Q_BLOCK, L)
    nb = L // qb

    def blk(t):
        return jnp.moveaxis(t.reshape((B, nb, qb) + t.shape[2:]), 1, 0)

    k_chunk = k_pos // CHUNK

    def one_block(args):
        ql, qp, pos_b = args
        s = (jnp.einsum('bqhc,bkc->bhqk', ql, ckv)
             + jnp.einsum('bqhr,bkr->bhqk', qp, kpe)).astype(jnp.float32) * ATTN_SCALE
        mask = k_chunk[None, :] <= (pos_b // CHUNK)[:, None]
        p = jax.nn.softmax(jnp.where(mask, s, NEG_INF), axis=-1).astype(ckv.dtype)
        return jnp.einsum('bhqk,bkc->bqhc', p, ckv)

    o = lax.map(one_block, (blk(q_lat), blk(q_pe), q_pos.reshape(nb, qb)))
    return jnp.moveaxis(o, 0, 1).reshape(B, L, H, C)


def hgrn2_recurrence(q, k, v, log_f, s0, chunk):
    B, L, H, _ = q.shape
    DV = v.shape[-1]
    n = L // chunk

    def to_chunks(t):
        return jnp.moveaxis(t.astype(jnp.float32).reshape((B, n, chunk) + t.shape[2:]), 1, 0)

    causal = jnp.tril(jnp.ones((chunk, chunk), dtype=bool))[None, :, :, None, None]

    def step(S, inp):
        qc, kc, vc, gc = inp
        G = jnp.cumsum(gc, axis=1)
        decay = jnp.exp(jnp.where(causal, G[:, :, None] - G[:, None, :], NEG_INF))
        scores = jnp.einsum('bthk,bshk,btshk->bhts', qc, kc, decay)
        o = (jnp.einsum('bthk,bhkv->bthv', qc * jnp.exp(G), S)
             + jnp.einsum('bhts,bshv->bthv', scores, vc))
        G_last = G[:, -1]
        S_new = (S * jnp.exp(G_last)[..., None]
                 + jnp.einsum('bshk,bshv->bhkv', kc * jnp.exp(G_last[:, None] - G), vc))
        return S_new, o

    s_fin, o = lax.scan(step, s0.astype(jnp.float32),
                        (to_chunks(q), to_chunks(k), to_chunks(v), to_chunks(log_f)))
    return jnp.moveaxis(o, 0, 1).reshape(B, L, H, DV), s_fin


def token_mix(h, pos, past_ckv, past_kpe, s0, chunk, lb,
              w_in, q_norm_w, kv_norm_w, w_uq, w_uk, w_uv, hg_norm_w):
    B, L, _ = h.shape
    c_q, c_kv, k_pe, hq, hf, hi, hg = split_columns(h @ w_in)

    q = (rms_norm(c_q, q_norm_w) @ w_uq).reshape(B, L, MLA_HEADS, QK_NOPE + QK_ROPE)
    q_pe = apply_rope(q[..., QK_NOPE:], pos)
    q_lat = jnp.einsum('blhn,chn->blhc', q[..., :QK_NOPE], w_uk)
    ckv = rms_norm(c_kv, kv_norm_w)
    kpe = apply_rope(k_pe, pos)
    if past_ckv is None:
        keys_ckv, keys_kpe = ckv, kpe
    else:
        keys_ckv = jnp.concatenate([past_ckv.astype(ckv.dtype), ckv], axis=1)
        keys_kpe = jnp.concatenate([past_kpe.astype(kpe.dtype), kpe], axis=1)
    k_pos = jnp.arange(keys_ckv.shape[1])
    o_lat = mla_attention(q_lat, q_pe, keys_ckv, keys_kpe, pos, k_pos)
    o_mla = jnp.einsum('blhc,chv->blhv', o_lat, w_uv).reshape(B, L, MLA_WIDTH)

    hf32 = hf.astype(jnp.float32)
    log_lb = jnp.log(jnp.maximum(lb, LB_FLOOR))
    log_f = jnp.logaddexp(log_lb, jnp.log1p(-lb) + jax.nn.log_sigmoid(hf32))
    k_in = (1.0 - lb) * jax.nn.sigmoid(-hf32)

    def heads(t, d):
        return t.reshape(B, L, HG_HEADS, d)

    o_hg, s_fin = hgrn2_recurrence(heads(jax.nn.silu(hq), HG_DK), heads(k_in, HG_DK),
                                   heads(hi, HG_DV), heads(log_f, HG_DK), s0, chunk)
    o_hg = rms_norm(o_hg.astype(h.dtype), hg_norm_w) * jax.nn.silu(heads(hg, HG_DV))
    mixed = jnp.concatenate([o_mla, o_hg.reshape(B, L, HG_WIDTH)], axis=-1)
    return mixed, ckv, kpe, s_fin.astype(s0.dtype)


def swiglu(x, w_gate, w_up, w_down):
    return (jax.nn.silu(x @ w_gate) * (x @ w_up)) @ w_down


def moe_swiglu(x, w_router, we_gate, we_up, we_down):
    logits = (x @ w_router).astype(jnp.float32)
    top_val, top_idx = lax.top_k(logits, TOP_K)
    weights = jax.nn.softmax(top_val, axis=-1)
    gate = jnp.sum(jax.nn.one_hot(top_idx, N_EXPERTS, dtype=jnp.float32) * weights[..., None],
                   axis=-2).astype(x.dtype)
    out = jnp.zeros_like(x)
    for e in range(N_EXPERTS):
        out = out + gate[..., e:e + 1] * swiglu(x, we_gate[e], we_up[e], we_down[e])
    return out


def setup_inputs(seed: int = 0) -> dict:
    key = jax.random.key(seed)
    ks = jax.random.split(key, 24)
    f32 = jnp.float32

    def nrm(k, shape, scale):
        return scale * jax.random.normal(k, shape, f32)

    def gain(k, shape):
        return 1.0 + 0.01 * jax.random.normal(k, shape, f32)

    return {
        'x_prompt': nrm(ks[0], (BATCH, SEQ, D_MODEL), 1.0),
        'x_sample': nrm(ks[1], (DEC_BATCH, DEC_SEQ, D_MODEL), 1.0),
        'cache_ckv': nrm(ks[2], (DEPTH, DEC_BATCH, PAST_LEN, KV_LORA), 1.0),
        'cache_kpe': nrm(ks[3], (DEPTH, DEC_BATCH, PAST_LEN, QK_ROPE), 1.0),
        'state_hgrn': nrm(ks[4], (DEPTH, DEC_BATCH, HG_HEADS, HG_DK, HG_DV), 0.5),
        'attn_norm_w': gain(ks[5], (DEPTH, D_MODEL)),
        'w_in': nrm(ks[6], (DEPTH, D_MODEL, IN_DIM), D_MODEL ** -0.5),
        'q_norm_w': gain(ks[7], (DEPTH, Q_LORA)),
        'kv_norm_w': gain(ks[8], (DEPTH, KV_LORA)),
        'w_uq': nrm(ks[9], (DEPTH, Q_LORA, MLA_HEADS * (QK_NOPE + QK_ROPE)), Q_LORA ** -0.5),
        'w_uk': nrm(ks[10], (DEPTH, KV_LORA, MLA_HEADS, QK_NOPE), KV_LORA ** -0.5),
        'w_uv': nrm(ks[11], (DEPTH, KV_LORA, MLA_HEADS, V_HEAD), KV_LORA ** -0.5),
        'hg_lower_bounds': nrm(ks[12], (DEPTH, HG_KEY), 0.1),
        'hg_norm_w': gain(ks[13], (DEPTH, HG_DV)),
        'w_out': nrm(ks[14], (DEPTH, MIX_WIDTH, D_MODEL), MIX_WIDTH ** -0.5),
        'ffn_norm_w': gain(ks[15], (DEPTH, D_MODEL)),
        'w_gate': nrm(ks[16], (N_DENSE, D_MODEL, D_FF), D_MODEL ** -0.5),
        'w_up': nrm(ks[17], (N_DENSE, D_MODEL, D_FF), D_MODEL ** -0.5),
        'w_down': nrm(ks[18], (N_DENSE, D_FF, D_MODEL), D_FF ** -0.5),
        'w_router': nrm(ks[19], (N_MOE, D_MODEL, N_EXPERTS), D_MODEL ** -0.5),
        'we_gate': nrm(ks[20], (N_MOE, N_EXPERTS, D_MODEL, MOE_FF), D_MODEL ** -0.5),
        'we_up': nrm(ks[21], (N_MOE, N_EXPERTS, D_MODEL, MOE_FF), D_MODEL ** -0.5),
        'we_down': nrm(ks[22], (N_MOE, N_EXPERTS, MOE_FF, D_MODEL), MOE_FF ** -0.5),
        'final_norm_w': gain(ks[23], (D_MODEL,)),
    }


def reference(x_prompt, x_sample, cache_ckv, cache_kpe, state_hgrn,
              attn_norm_w, w_in, q_norm_w, kv_norm_w, w_uq, w_uk, w_uv,
              hg_lower_bounds, hg_norm_w, w_out, ffn_norm_w,
              w_gate, w_up, w_down, w_router, we_gate, we_up, we_down, final_norm_w):
    probs = jax.nn.softmax(hg_lower_bounds.astype(jnp.float32), axis=0)
    lower_bounds = jnp.cumsum(probs, axis=0) - probs[0:1]

    b_p, l_p, _ = x_prompt.shape
    l_s = x_sample.shape[1]
    past_len = cache_ckv.shape[2]
    pos_p = jnp.arange(l_p)
    pos_s = past_len + jnp.arange(l_s)
    s0_p = jnp.zeros((b_p, HG_HEADS, HG_DK, HG_DV), x_prompt.dtype)

    xp, xs = x_prompt, x_sample
    ckv_p_l, kpe_p_l, st_p_l, ckv_s_l, kpe_s_l, st_s_l = [], [], [], [], [], []
    for l in range(DEPTH):
        mix_w = (w_in[l], q_norm_w[l], kv_norm_w[l], w_uq[l], w_uk[l], w_uv[l], hg_norm_w[l])
        mp, ckv_p, kpe_p, st_p = token_mix(rms_norm(xp, attn_norm_w[l]), pos_p, None, None,
                                           s0_p, CHUNK, lower_bounds[l], *mix_w)
        ms, ckv_s, kpe_s, st_s = token_mix(rms_norm(xs, attn_norm_w[l]), pos_s, cache_ckv[l],
                                           cache_kpe[l], state_hgrn[l], l_s, lower_bounds[l], *mix_w)
        xp = xp + mp @ w_out[l]
        xs = xs + ms @ w_out[l]
        hp = rms_norm(xp, ffn_norm_w[l])
        hs = rms_norm(xs, ffn_norm_w[l])
        i = l // 2
        if l % 2 == 0:
            xp = xp + swiglu(hp, w_gate[i], w_up[i], w_down[i])
            xs = xs + swiglu(hs, w_gate[i], w_up[i], w_down[i])
        else:
            xp = xp + moe_swiglu(hp, w_router[i], we_gate[i], we_up[i], we_down[i])
            xs = xs + moe_swiglu(hs, w_router[i], we_gate[i], we_up[i], we_down[i])
        ckv_p_l.append(ckv_p)
        kpe_p_l.append(kpe_p)
        st_p_l.append(st_p)
        ckv_s_l.append(ckv_s)
        kpe_s_l.append(kpe_s)
        st_s_l.append(st_s)

    y_prompt = rms_norm(xp, final_norm_w)
    y_sample = rms_norm(xs, final_norm_w)
    return (y_prompt, y_sample,
            jnp.stack(ckv_p_l), jnp.stack(kpe_p_l), jnp.stack(st_p_l),
            jnp.stack(ckv_s_l), jnp.stack(kpe_s_l), jnp.stack(st_s_l))
```

```python
import functools
import math

import jax
import jax.numpy as jnp
from jax import lax
from jax.experimental import pallas as pl
from jax.experimental.pallas import tpu as pltpu

F32 = jnp.float32
BF16 = jnp.bfloat16

D_MODEL = 2048
CHUNK = 64
RMS_EPS = 1e-6
NEG_INF = -1e30
LB_FLOOR = 1e-30

MLA_HEADS = 8
V_HEAD = 128
QK_NOPE = 128
QK_ROPE = 64
Q_LORA = 512
KV_LORA = 256
ROPE_THETA = 10000.0
ATTN_SCALE = (QK_NOPE + QK_ROPE) ** -0.5
MLA_WIDTH = MLA_HEADS * V_HEAD

HG_HEADS = 8
HG_DK = 128
HG_DV = 128
HG_KEY = HG_HEADS * HG_DK
HG_WIDTH = HG_HEADS * HG_DV
HG_SUB = 16

N_EXPERTS = 8

LANE = 128
QK_PAD = KV_LORA + LANE

COL_CQ = 0
COL_CKV = Q_LORA
COL_HQ = Q_LORA + KV_LORA
COL_HF = COL_HQ + HG_KEY
COL_HI = COL_HF + HG_KEY
COL_HG = COL_HI + HG_WIDTH
COL_KPE = COL_HG + HG_WIDTH
IN_EXT = COL_KPE + 2 * LANE

VMEM_LIMIT = 56 * 1024 * 1024
TOKEN_TILE = 512


def _params(semantics):
    return pltpu.CompilerParams(dimension_semantics=semantics, vmem_limit_bytes=VMEM_LIMIT)


def _rms(x, w):
    return x * lax.rsqrt(jnp.mean(x * x, axis=-1, keepdims=True) + RMS_EPS) * w


def _norm_matmul_kernel(x_ref, nw_ref, w_ref, o_ref, h_scr):
    @pl.when(pl.program_id(1) == 0)
    def _():
        h_scr[...] = _rms(x_ref[...], nw_ref[...]).astype(BF16)

    o_ref[...] = jnp.dot(h_scr[...], w_ref[...], preferred_element_type=F32).astype(o_ref.dtype)


def norm_matmul(x, nw, w, *, tm, tn):
    n, d = x.shape
    cols = w.shape[1]
    return pl.pallas_call(
        _norm_matmul_kernel,
        grid=(n // tm, cols // tn),
        in_specs=[pl.BlockSpec((tm, d), lambda i, j: (i, 0)),
                  pl.BlockSpec((1, d), lambda i, j: (0, 0)),
                  pl.BlockSpec((d, tn), lambda i, j: (0, j))],
        out_specs=pl.BlockSpec((tm, tn), lambda i, j: (i, j)),
        out_shape=jax.ShapeDtypeStruct((n, cols), BF16),
        scratch_shapes=[pltpu.VMEM((tm, d), BF16)],
        compiler_params=_params(("parallel", "arbitrary")),
        name="in_proj",
    )(x, nw, w)


def _mla_prep_kernel(cq_ref, ckv_ref, kp_ref, qnw_ref, kvnw_ref, wuq_ref, wuk_ref, cos_ref, sin_ref,
                     q_ref, kb_ref, ckv_out_ref, kpe_out_ref):
    cos = cos_ref[...]
    sin = sin_ref[...]
    cqn = _rms(cq_ref[...].astype(F32), qnw_ref[...]).astype(BF16)
    q = jnp.dot(cqn, wuq_ref[...], preferred_element_type=F32)
    hw = MLA_HEADS * LANE
    for h in range(MLA_HEADS):
        sl = slice(h * LANE, (h + 1) * LANE)
        q_lat = jnp.dot(q[:, sl].astype(BF16), wuk_ref[h], preferred_element_type=F32)
        rope = q[:, hw + h * LANE:hw + (h + 1) * LANE] * cos + q[:, 2 * hw + h * LANE:2 * hw + (h + 1) * LANE] * sin
        q_ref[h, :, 0:KV_LORA] = q_lat.astype(BF16)
        q_ref[h, :, KV_LORA:QK_PAD] = rope.astype(BF16)
    ckv = _rms(ckv_ref[...].astype(F32), kvnw_ref[...])
    ckv_out_ref[...] = ckv
    kb_ref[:, 0:KV_LORA] = ckv.astype(BF16)
    kp = kp_ref[...].astype(F32)
    kpe = kp[:, 0:LANE] * cos + kp[:, LANE:2 * LANE] * sin
    kpe_out_ref[...] = kpe[:, 0:QK_ROPE]
    kb_ref[:, KV_LORA:QK_PAD] = kpe.astype(BF16)


def mla_prep(p, qnw, kvnw, wuq, wuk, cos, sin, *, tm):
    n = p.shape[0]
    n_pos = cos.shape[0] // tm
    return pl.pallas_call(
        _mla_prep_kernel,
        grid=(n // tm,),
        in_specs=[pl.BlockSpec((tm, Q_LORA), lambda i: (i, COL_CQ // Q_LORA)),
                  pl.BlockSpec((tm, KV_LORA), lambda i: (i, COL_CKV // KV_LORA)),
                  pl.BlockSpec((tm, 2 * LANE), lambda i: (i, COL_KPE // (2 * LANE))),
                  pl.BlockSpec((1, Q_LORA), lambda i: (0, 0)),
                  pl.BlockSpec((1, KV_LORA), lambda i: (0, 0)),
                  pl.BlockSpec(wuq.shape, lambda i: (0, 0)),
                  pl.BlockSpec(wuk.shape, lambda i: (0, 0, 0)),
                  pl.BlockSpec((tm, LANE), lambda i: (i % n_pos, 0)),
                  pl.BlockSpec((tm, LANE), lambda i: (i % n_pos, 0))],
        out_specs=[pl.BlockSpec((MLA_HEADS, tm, QK_PAD), lambda i: (0, i, 0)),
                   pl.BlockSpec((tm, QK_PAD), lambda i: (i, 0)),
                   pl.BlockSpec((tm, KV_LORA), lambda i: (i, 0)),
                   pl.BlockSpec((tm, QK_ROPE), lambda i: (i, 0))],
        out_shape=[jax.ShapeDtypeStruct((MLA_HEADS, n, QK_PAD), BF16),
                   jax.ShapeDtypeStruct((n, QK_PAD), BF16),
                   jax.ShapeDtypeStruct((n, KV_LORA), F32),
                   jax.ShapeDtypeStruct((n, QK_ROPE), F32)],
        compiler_params=_params(("parallel",)),
        name="mla_prep",
    )(p, p, p, qnw, kvnw, wuq, wuk, cos, sin)


def _attention_kernel(q_ref, k_ref, wuv_ref, o_ref, m_scr, l_scr, acc_scr, *, tq, kb, causal, n_kb):
    rows = MLA_HEADS * tq
    q = q_ref[...].reshape(rows, QK_PAD)
    m_scr[...] = jnp.full((rows, 1), NEG_INF, F32)
    l_scr[...] = jnp.zeros((rows, 1), F32)
    acc_scr[...] = jnp.zeros((rows, KV_LORA), F32)

    def block(j, masked):
        if isinstance(j, int):
            k = k_ref[j * kb:(j + 1) * kb, :]
        else:
            k = k_ref[pl.ds(pl.multiple_of(j * kb, kb), kb), :]
        s = lax.dot_general(q, k, (((1,), (1,)), ((), ())), preferred_element_type=F32) * ATTN_SCALE
        if masked:
            q_pos = pl.program_id(1) * tq + jnp.bitwise_and(
                lax.broadcasted_iota(jnp.int32, (rows, kb), 0), tq - 1)
            k_pos = j * kb + lax.broadcasted_iota(jnp.int32, (rows, kb), 1)
            s = jnp.where(k_pos // CHUNK <= q_pos // CHUNK, s, NEG_INF)
        m_old = m_scr[...]
        m_new = jnp.maximum(m_old, jnp.max(s, axis=-1, keepdims=True))
        alpha = jnp.exp(m_old - m_new)
        p = jnp.exp(s - m_new)
        l_scr[...] = alpha * l_scr[...] + jnp.sum(p, axis=-1, keepdims=True)
        acc_scr[...] = alpha * acc_scr[...] + jnp.dot(p.astype(BF16), k[:, 0:KV_LORA],
                                                      preferred_element_type=F32)
        m_scr[...] = m_new

    if causal:
        n_blocks = (tq * (pl.program_id(1) + 1) + kb - 1) // kb

        def body(j, carry):
            block(j, False)
            return carry

        lax.fori_loop(0, n_blocks - 1, body, 0)
        block(n_blocks - 1, True)
    else:
        for j in range(n_kb):
            block(j, False)

    o = (acc_scr[...] / l_scr[...]).astype(BF16)
    for h in range(MLA_HEADS):
        o_ref[:, h * V_HEAD:(h + 1) * V_HEAD] = jnp.dot(
            o[h * tq:(h + 1) * tq], wuv_ref[h], preferred_element_type=F32).astype(o_ref.dtype)


def attention(q, k, wuv, *, batch, lq, lk, tq, kb, causal):
    nq = lq // tq
    rows = MLA_HEADS * tq
    kern = functools.partial(_attention_kernel, tq=tq, kb=kb, causal=causal, n_kb=lk // kb)
    return pl.pallas_call(
        kern,
        grid=(batch, nq),
        in_specs=[pl.BlockSpec((MLA_HEADS, tq, QK_PAD), lambda b, i: (0, b * nq + i, 0)),
                  pl.BlockSpec((lk, QK_PAD), lambda b, i: (b, 0)),
                  pl.BlockSpec(wuv.shape, lambda b, i: (0, 0, 0))],
        out_specs=pl.BlockSpec((tq, MLA_WIDTH), lambda b, i: (b * nq + i, 0)),
        out_shape=jax.ShapeDtypeStruct((batch * lq, MLA_WIDTH), BF16),
        scratch_shapes=[pltpu.VMEM((rows, 1), F32), pltpu.VMEM((rows, 1), F32),
                        pltpu.VMEM((rows, KV_LORA), F32)],
        compiler_params=_params(("parallel", "arbitrary")),
        name="attention_causal" if causal else "attention_full",
    )(q, k, wuv)


def _sigmoid(x):
    return 1.0 / (1.0 + jnp.exp(-x))


def _hgrn_kernel(hq_ref, hf_ref, hi_ref, hg_ref, lb_ref, nw_ref, s0_ref, o_ref, sfin_ref,
                 st_scr, oacc_scr, *, chunk, n_chunks):
    t = pl.program_id(2)

    @pl.when(t == 0)
    def _():
        st_scr[...] = s0_ref[0, 0].T

    lb = lb_ref[...]
    log_lb = jnp.log(jnp.maximum(lb, LB_FLOOR))
    log_1m_lb = jnp.log1p(-lb)
    one_m_lb = 1.0 - lb
    nw = nw_ref[...]
    tri = (lax.broadcasted_iota(jnp.int32, (chunk, chunk), 0)
           >= lax.broadcasted_iota(jnp.int32, (chunk, chunk), 1)).astype(F32)
    ones = jnp.ones((HG_DK, HG_DV), BF16)
    row_id = lax.broadcasted_iota(jnp.int32, (chunk, HG_DK), 0)
    n_sub = chunk // HG_SUB

    def chunk_body(c, carry):
        sl = pl.ds(pl.multiple_of(c * chunk, chunk), chunk)
        hf = hf_ref[sl, :].astype(F32)
        hq = hq_ref[sl, :].astype(F32)
        v = hi_ref[sl, :].astype(F32)
        hg = hg_ref[sl, :].astype(F32)
        log_sig = jnp.minimum(hf, 0.0) - jnp.log1p(jnp.exp(-jnp.abs(hf)))
        b = log_1m_lb + log_sig
        log_f = jnp.maximum(log_lb, b) + jnp.log1p(jnp.exp(-jnp.abs(log_lb - b)))
        k = one_m_lb * _sigmoid(-hf)
        q = hq * _sigmoid(hq)
        g = jnp.dot(tri, log_f, precision=lax.Precision.HIGHEST, preferred_element_type=F32)
        v16 = v.astype(BF16)
        st = st_scr[...]

        oacc_scr[...] = lax.dot_general((q * jnp.exp(g)).astype(BF16), st.astype(BF16),
                                        (((1,), (1,)), ((), ())), preferred_element_type=F32)
        for i in range(1, n_sub):
            r = i * HG_SUB
            g_edge = g[r - 1:r]
            qt = q[r:r + HG_SUB] * jnp.exp(g[r:r + HG_SUB] - g_edge)
            kt = k[0:r] * jnp.exp(g_edge - g[0:r])
            a = lax.dot_general(qt.astype(BF16), kt.astype(BF16), (((1,), (1,)), ((), ())),
                                preferred_element_type=F32)
            oacc_scr[r:r + HG_SUB, :] += jnp.dot(a.astype(BF16), v16[0:r], preferred_element_type=F32)
        for grp in range(chunk // 8):
            r0 = grp * 8
            r1 = (r0 // HG_SUB + 1) * HG_SUB
            n = r1 - r0
            parts = []
            for s in range(r0, r0 + 8):
                d = jnp.where(row_id[r0:r1] >= s, g[r0:r1] - g[s:s + 1], NEG_INF)
                parts.append(jnp.exp(d) * q[r0:r1] * k[s:s + 1])
            sums = jnp.dot(jnp.concatenate(parts, axis=0).astype(BF16), ones, preferred_element_type=F32)
            upd = sums[0:n] * v[r0:r0 + 1]
            for u in range(1, 8):
                upd = upd + sums[u * n:(u + 1) * n] * v[r0 + u:r0 + u + 1]
            oacc_scr[r0:r1, :] += upd

        g_last = g[chunk - 1:chunk]
        kh = k * jnp.exp(g_last - g)
        st_scr[...] = st * jnp.exp(g_last) + lax.dot_general(
            v16, kh.astype(BF16), (((0,), (0,)), ((), ())), preferred_element_type=F32)

        o_ref[sl, :] = (_rms(oacc_scr[...], nw) * (hg * _sigmoid(hg))).astype(o_ref.dtype)
        return carry

    lax.fori_loop(0, n_chunks, chunk_body, 0)

    @pl.when(t == pl.num_programs(2) - 1)
    def _():
        sfin_ref[0, 0] = st_scr[...].T


def hgrn(p, lb, nw, s0, *, batch, seq, chunk, tb):
    nt = seq // tb
    kern = functools.partial(_hgrn_kernel, chunk=chunk, n_chunks=tb // chunk)

    def col(base):
        return lambda b, h, t: (b * nt + t, base // LANE + h)

    return pl.pallas_call(
        kern,
        grid=(batch, HG_HEADS, nt),
        in_specs=[pl.BlockSpec((tb, LANE), col(COL_HQ)),
                  pl.BlockSpec((tb, LANE), col(COL_HF)),
                  pl.BlockSpec((tb, LANE), col(COL_HI)),
                  pl.BlockSpec((tb, LANE), col(COL_HG)),
                  pl.BlockSpec((1, HG_DK), lambda b, h, t: (0, h)),
                  pl.BlockSpec((1, HG_DV), lambda b, h, t: (0, 0)),
                  pl.BlockSpec((1, 1, HG_DK, HG_DV), lambda b, h, t: (b, h, 0, 0))],
        out_specs=[pl.BlockSpec((tb, HG_DV), lambda b, h, t: (b * nt + t, h)),
                   pl.BlockSpec((1, 1, HG_DK, HG_DV), lambda b, h, t: (b, h, 0, 0))],
        out_shape=[jax.ShapeDtypeStruct((batch * seq, HG_WIDTH), BF16),
                   jax.ShapeDtypeStruct((batch, HG_HEADS, HG_DK, HG_DV), F32)],
        scratch_shapes=[pltpu.VMEM((HG_DV, HG_DK), F32), pltpu.VMEM((chunk, HG_DV), F32)],
        compiler_params=_params(("parallel", "parallel", "arbitrary")),
        name="hgrn",
    )(p, p, p, p, lb, nw, s0)


def _out_proj_kernel(x_ref, a_ref, b_ref, w_ref, o_ref):
    o_ref[...] = (x_ref[...]
                  + jnp.dot(a_ref[...], w_ref[0:MLA_WIDTH, :], preferred_element_type=F32)
                  + jnp.dot(b_ref[...], w_ref[MLA_WIDTH:MLA_WIDTH + HG_WIDTH, :], preferred_element_type=F32))


def out_proj(x, a, b, w, *, tm):
    n, d = x.shape
    return pl.pallas_call(
        _out_proj_kernel,
        grid=(n // tm,),
        in_specs=[pl.BlockSpec((tm, d), lambda i: (i, 0)),
                  pl.BlockSpec((tm, MLA_WIDTH), lambda i: (i, 0)),
                  pl.BlockSpec((tm, HG_WIDTH), lambda i: (i, 0)),
                  pl.BlockSpec(w.shape, lambda i: (0, 0))],
        out_specs=pl.BlockSpec((tm, d), lambda i: (i, 0)),
        out_shape=jax.ShapeDtypeStruct((n, d), F32),
        compiler_params=_params(("parallel",)),
        name="out_proj",
    )(x, a, b, w)


def _silu(x):
    return x * _sigmoid(x)


def _ffn_kernel(x_ref, nw_ref, wg_ref, wu_ref, wd_ref, o_ref, h_scr, acc_scr):
    j = pl.program_id(1)

    @pl.when(j == 0)
    def _():
        h_scr[...] = _rms(x_ref[...], nw_ref[...]).astype(BF16)
        acc_scr[...] = jnp.zeros_like(acc_scr)

    h = h_scr[...]
    a = _silu(jnp.dot(h, wg_ref[...], preferred_element_type=F32)) * jnp.dot(
        h, wu_ref[...], preferred_element_type=F32)
    acc_scr[...] += jnp.dot(a.astype(BF16), wd_ref[...], preferred_element_type=F32)

    @pl.when(j == pl.num_programs(1) - 1)
    def _():
        o_ref[...] = x_ref[...] + acc_scr[...]


def ffn(x, nw, wg, wu, wd, *, tm, tf):
    n, d = x.shape
    dff = wg.shape[1]
    return pl.pallas_call(
        _ffn_kernel,
        grid=(n // tm, dff // tf),
        in_specs=[pl.BlockSpec((tm, d), lambda i, j: (i, 0)),
                  pl.BlockSpec((1, d), lambda i, j: (0, 0)),
                  pl.BlockSpec((d, tf), lambda i, j: (0, j)),
                  pl.BlockSpec((d, tf), lambda i, j: (0, j)),
                  pl.BlockSpec((tf, d), lambda i, j: (j, 0))],
        out_specs=pl.BlockSpec((tm, d), lambda i, j: (i, 0)),
        out_shape=jax.ShapeDtypeStruct((n, d), F32),
        scratch_shapes=[pltpu.VMEM((tm, d), BF16), pltpu.VMEM((tm, d), F32)],
        compiler_params=_params(("parallel", "arbitrary")),
        name="ffn",
    )(x, nw, wg, wu, wd)


def _moe_kernel(x_ref, nw_ref, wr_ref, wg_ref, wu_ref, wd_ref, fw_ref, o_ref,
                h_scr, acc_scr, i1_scr, i2_scr, w1_scr, w2_scr):
    e = pl.program_id(1)
    j = pl.program_id(2)

    @pl.when((e == 0) & (j == 0))
    def _():
        h = _rms(x_ref[...], nw_ref[...])
        h_scr[...] = h.astype(BF16)
        acc_scr[...] = jnp.zeros_like(acc_scr)
        logits = jnp.dot(h, wr_ref[...], precision=lax.Precision.HIGHEST, preferred_element_type=F32)
        ids = lax.broadcasted_iota(jnp.int32, logits.shape, 1).astype(F32)
        m1 = jnp.max(logits, axis=-1, keepdims=True)
        i1 = jnp.min(jnp.where(logits == m1, ids, float(N_EXPERTS)), axis=-1, keepdims=True)
        rest = jnp.where(ids == i1, -jnp.inf, logits)
        m2 = jnp.max(rest, axis=-1, keepdims=True)
        i2 = jnp.min(jnp.where(rest == m2, ids, float(N_EXPERTS)), axis=-1, keepdims=True)
        t = jnp.exp(m2 - m1)
        i1_scr[...] = i1
        i2_scr[...] = i2
        w1_scr[...] = 1.0 / (1.0 + t)
        w2_scr[...] = t / (1.0 + t)

    h = h_scr[...]
    a = _silu(jnp.dot(h, wg_ref[...], preferred_element_type=F32)) * jnp.dot(
        h, wu_ref[...], preferred_element_type=F32)
    ef = e.astype(F32)
    gate = jnp.where(i1_scr[...] == ef, w1_scr[...], 0.0) + jnp.where(i2_scr[...] == ef, w2_scr[...], 0.0)
    acc_scr[...] += gate * jnp.dot(a.astype(BF16), wd_ref[...], preferred_element_type=F32)

    @pl.when((e == pl.num_programs(1) - 1) & (j == pl.num_programs(2) - 1))
    def _():
        o_ref[...] = _rms(x_ref[...] + acc_scr[...], fw_ref[...])


def moe_final(x, nw, wr, wg, wu, wd, fw, *, tm, tf):
    n, d = x.shape
    n_exp, _, dff = wg.shape
    return pl.pallas_call(
        _moe_kernel,
        grid=(n // tm, n_exp, dff // tf),
        in_specs=[pl.BlockSpec((tm, d), lambda i, e, j: (i, 0)),
                  pl.BlockSpec((1, d), lambda i, e, j: (0, 0)),
                  pl.BlockSpec(wr.shape, lambda i, e, j: (0, 0)),
                  pl.BlockSpec((None, d, tf), lambda i, e, j: (e, 0, j)),
                  pl.BlockSpec((None, d, tf), lambda i, e, j: (e, 0, j)),
                  pl.BlockSpec((None, tf, d), lambda i, e, j: (e, j, 0)),
                  pl.BlockSpec((1, d), lambda i, e, j: (0, 0))],
        out_specs=pl.BlockSpec((tm, d), lambda i, e, j: (i, 0)),
        out_shape=jax.ShapeDtypeStruct((n, d), F32),
        scratch_shapes=[pltpu.VMEM((tm, d), BF16), pltpu.VMEM((tm, d), F32),
                        pltpu.VMEM((tm, 1), F32), pltpu.VMEM((tm, 1), F32),
                        pltpu.VMEM((tm, 1), F32), pltpu.VMEM((tm, 1), F32)],
        compiler_params=_params(("parallel", "arbitrary", "arbitrary")),
        name="moe",
    )(x, nw, wr, wg, wu, wd, fw)


def _swap_halves(t):
    half = t.shape[-1] // 2
    return jnp.concatenate([t[..., half:], t[..., :half]], axis=-1)


def _prep_w_in(w):
    d = w.shape[0]
    k_pe = w[:, Q_LORA + KV_LORA:Q_LORA + KV_LORA + QK_ROPE]
    pad = jnp.zeros((d, LANE - QK_ROPE), w.dtype)
    return jnp.concatenate([w[:, :Q_LORA + KV_LORA], w[:, Q_LORA + KV_LORA + QK_ROPE:],
                            k_pe, pad, _swap_halves(k_pe), pad], axis=1).astype(BF16)


def _prep_w_uq(w):
    w = w.reshape(Q_LORA, MLA_HEADS, QK_NOPE + QK_ROPE)
    nope = w[..., :QK_NOPE]
    rope = w[..., QK_NOPE:]
    pad = jnp.zeros((Q_LORA, MLA_HEADS, LANE - QK_ROPE), w.dtype)
    parts = [nope, jnp.concatenate([rope, pad], -1), jnp.concatenate([_swap_halves(rope), pad], -1)]
    return jnp.concatenate([t.reshape(Q_LORA, MLA_HEADS * LANE) for t in parts], axis=1).astype(BF16)


def _rope_tables(pos):
    half = QK_ROPE // 2
    inv_freq = jnp.exp(jnp.arange(half, dtype=F32) * (-math.log(ROPE_THETA) / half))
    ang = pos.astype(F32)[:, None] * inv_freq[None, :]
    cos, sin = jnp.cos(ang), jnp.sin(ang)
    pad = jnp.zeros((pos.shape[0], LANE - QK_ROPE), F32)
    return jnp.concatenate([cos, cos, pad], -1), jnp.concatenate([-sin, sin, pad], -1)


def _row(v):
    return v.reshape(1, -1)


def kernel(x_prompt, x_sample, cache_ckv, cache_kpe, state_hgrn, attn_norm_w, w_in, q_norm_w, kv_norm_w,
           w_uq, w_uk, w_uv, hg_lower_bounds, hg_norm_w, w_out, ffn_norm_w, w_gate, w_up, w_down,
           w_router, we_gate, we_up, we_down, final_norm_w):
    depth = w_in.shape[0]
    b_p, l_p, d = x_prompt.shape
    b_s, l_s, _ = x_sample.shape
    past = cache_ckv.shape[2]
    n_p, n_s = b_p * l_p, b_s * l_s
    assert depth == 2 and d == D_MODEL

    probs = jax.nn.softmax(hg_lower_bounds.astype(F32), axis=0)
    lower_bounds = jnp.cumsum(probs, axis=0) - probs[0:1]

    cos_p, sin_p = _rope_tables(jnp.arange(l_p))
    cos_s, sin_s = _rope_tables(past + jnp.arange(l_s))
    cos_s, sin_s = jnp.tile(cos_s, (b_s, 1)), jnp.tile(sin_s, (b_s, 1))
    zero_state = jnp.zeros((b_p, HG_HEADS, HG_DK, HG_DV), F32)

    xp = x_prompt.reshape(n_p, d)
    xs = x_sample.reshape(n_s, d)
    outs = {k: [] for k in ("ckv_p", "kpe_p", "st_p", "ckv_s", "kpe_s", "st_s")}

    for l in range(depth):
        w_in_l = _prep_w_in(w_in[l])
        w_uq_l = _prep_w_uq(w_uq[l])
        w_uk_l = jnp.transpose(w_uk[l], (1, 2, 0)).astype(BF16)
        w_uv_l = jnp.transpose(w_uv[l], (1, 0, 2)).astype(BF16)
        w_out_l = w_out[l].astype(BF16)
        lb_l = _row(lower_bounds[l])
        mixers = []
        for (x, n, batch, seq, cos, sin) in ((xp, n_p, b_p, l_p, cos_p, sin_p),
                                             (xs, n_s, b_s, l_s, cos_s, sin_s)):
            tm = min(TOKEN_TILE, n)
            p = norm_matmul(x, _row(attn_norm_w[l]), w_in_l, tm=tm, tn=1024)
            q, kb, ckv, kpe = mla_prep(p, _row(q_norm_w[l]), _row(kv_norm_w[l]), w_uq_l, w_uk_l,
                                       cos, sin, tm=tm)
            if x is xp:
                o_mla = attention(q, kb, w_uv_l, batch=batch, lq=seq, lk=seq, tq=128, kb=256, causal=True)
                o_hg, st = hgrn(p, lb_l, _row(hg_norm_w[l]), zero_state, batch=batch, seq=seq,
                                chunk=CHUNK, tb=min(TOKEN_TILE, seq))
            else:
                past_k = jnp.concatenate(
                    [cache_ckv[l].astype(BF16), cache_kpe[l].astype(BF16),
                     jnp.zeros((batch, past, LANE - QK_ROPE), BF16)], axis=-1)
                keys = jnp.concatenate([past_k, kb.reshape(batch, seq, QK_PAD)], axis=1)
                lk = past + seq
                o_mla = attention(q, keys.reshape(batch * lk, QK_PAD), w_uv_l, batch=batch, lq=seq, lk=lk,
                                  tq=seq, kb=lk, causal=False)
                o_hg, st = hgrn(p, lb_l, _row(hg_norm_w[l]), state_hgrn[l], batch=batch, seq=seq,
                                chunk=seq, tb=seq)
            x1 = out_proj(x, o_mla, o_hg, w_out_l, tm=tm)
            mixers.append((x1, ckv.reshape(batch, seq, KV_LORA), kpe.reshape(batch, seq, QK_ROPE), st))
        (xp, ckv_p, kpe_p, st_p), (xs, ckv_s, kpe_s, st_s) = mixers
        for name, val in (("ckv_p", ckv_p), ("kpe_p", kpe_p), ("st_p", st_p),
                          ("ckv_s", ckv_s), ("kpe_s", kpe_s), ("st_s", st_s)):
            outs[name].append(val)

        i = l // 2
        if l % 2 == 0:
            wg, wu, wd = w_gate[i].astype(BF16), w_up[i].astype(BF16), w_down[i].astype(BF16)
            xp = ffn(xp, _row(ffn_norm_w[l]), wg, wu, wd, tm=min(TOKEN_TILE, n_p), tf=512)
            xs = ffn(xs, _row(ffn_norm_w[l]), wg, wu, wd, tm=min(TOKEN_TILE, n_s), tf=512)
        else:
            wg, wu, wd = we_gate[i].astype(BF16), we_up[i].astype(BF16), we_down[i].astype(BF16)
            xp = moe_final(xp, _row(ffn_norm_w[l]), w_router[i], wg, wu, wd, _row(final_norm_w),
                           tm=min(TOKEN_TILE, n_p), tf=256)
            xs = moe_final(xs, _row(ffn_norm_w[l]), w_router[i], wg, wu, wd, _row(final_norm_w),
                           tm=min(TOKEN_TILE, n_s), tf=256)

    return (xp.reshape(b_p, l_p, d), xs.reshape(b_s, l_s, d),
            jnp.stack(outs["ckv_p"]), jnp.stack(outs["kpe_p"]), jnp.stack(outs["st_p"]),
            jnp.stack(outs["ckv_s"]), jnp.stack(outs["kpe_s"]), jnp.stack(outs["st_s"]))
```

```python
import functools
import math

import jax
import jax.numpy as jnp
from jax import lax
from jax.experimental import pallas as pl
from jax.experimental.pallas import tpu as pltpu

F32 = jnp.float32
BF16 = jnp.bfloat16

D_MODEL = 2048
CHUNK = 64
RMS_EPS = 1e-6
NEG_INF = -1e30
LB_FLOOR = 1e-30

MLA_HEADS = 8
V_HEAD = 128
QK_NOPE = 128
QK_ROPE = 64
Q_LORA = 512
KV_LORA = 256
ROPE_THETA = 10000.0
ATTN_SCALE = (QK_NOPE + QK_ROPE) ** -0.5
MLA_WIDTH = MLA_HEADS * V_HEAD

HG_HEADS = 8
HG_DK = 128
HG_DV = 128
HG_KEY = HG_HEADS * HG_DK
HG_WIDTH = HG_HEADS * HG_DV
HG_SUB = 16

N_EXPERTS = 8

LANE = 128
QK_PAD = KV_LORA + LANE

COL_CQ = 0
COL_CKV = Q_LORA
COL_HQ = Q_LORA + KV_LORA
COL_HF = COL_HQ + HG_KEY
COL_HI = COL_HF + HG_KEY
COL_HG = COL_HI + HG_WIDTH
COL_KPE = COL_HG + HG_WIDTH
IN_EXT = COL_KPE + 2 * LANE

VMEM_LIMIT = 56 * 1024 * 1024
TOKEN_TILE = 512


def _params(semantics):
    return pltpu.CompilerParams(dimension_semantics=semantics, vmem_limit_bytes=VMEM_LIMIT)


def _rms(x, w):
    return x * lax.rsqrt(jnp.mean(x * x, axis=-1, keepdims=True) + RMS_EPS) * w


def _norm_matmul_kernel(x_ref, nw_ref, w_ref, o_ref, h_scr):
    @pl.when(pl.program_id(1) == 0)
    def _():
        h_scr[...] = _rms(x_ref[...], nw_ref[...]).astype(BF16)

    o_ref[...] = jnp.dot(h_scr[...], w_ref[...], preferred_element_type=F32).astype(o_ref.dtype)


def norm_matmul(x, nw, w, *, tm, tn):
    n, d = x.shape
    cols = w.shape[1]
    return pl.pallas_call(
        _norm_matmul_kernel,
        grid=(n // tm, cols // tn),
        in_specs=[pl.BlockSpec((tm, d), lambda i, j: (i, 0)),
                  pl.BlockSpec((1, d), lambda i, j: (0, 0)),
                  pl.BlockSpec((d, tn), lambda i, j: (0, j))],
        out_specs=pl.BlockSpec((tm, tn), lambda i, j: (i, j)),
        out_shape=jax.ShapeDtypeStruct((n, cols), BF16),
        scratch_shapes=[pltpu.VMEM((tm, d), BF16)],
        compiler_params=_params(("parallel", "arbitrary")),
        name="in_proj",
    )(x, nw, w)


def _mla_prep_kernel(cq_ref, ckv_ref, kp_ref, qnw_ref, kvnw_ref, wuq_ref, wuk_ref, cos_ref, sin_ref,
                     q_ref, kb_ref, ckv_out_ref, kpe_out_ref):
    cos = cos_ref[...]
    sin = sin_ref[...]
    cqn = _rms(cq_ref[...].astype(F32), qnw_ref[...]).astype(BF16)
    q = jnp.dot(cqn, wuq_ref[...], preferred_element_type=F32)
    hw = MLA_HEADS * LANE
    for h in range(MLA_HEADS):
        sl = slice(h * LANE, (h + 1) * LANE)
        q_lat = jnp.dot(q[:, sl].astype(BF16), wuk_ref[h], preferred_element_type=F32)
        rope = q[:, hw + h * LANE:hw + (h + 1) * LANE] * cos + q[:, 2 * hw + h * LANE:2 * hw + (h + 1) * LANE] * sin
        q_ref[h, :, 0:KV_LORA] = q_lat.astype(BF16)
        q_ref[h, :, KV_LORA:QK_PAD] = rope.astype(BF16)
    ckv = _rms(ckv_ref[...].astype(F32), kvnw_ref[...])
    ckv_out_ref[...] = ckv
    kb_ref[:, 0:KV_LORA] = ckv.astype(BF16)
    kp = kp_ref[...].astype(F32)
    kpe = kp[:, 0:LANE] * cos + kp[:, LANE:2 * LANE] * sin
    kpe_out_ref[...] = kpe[:, 0:QK_ROPE]
    kb_ref[:, KV_LORA:QK_PAD] = kpe.astype(BF16)


def mla_prep(p, qnw, kvnw, wuq, wuk, cos, sin, *, tm):
    n = p.shape[0]
    n_pos = cos.shape[0] // tm
    return pl.pallas_call(
        _mla_prep_kernel,
        grid=(n // tm,),
        in_specs=[pl.BlockSpec((tm, Q_LORA), lambda i: (i, COL_CQ // Q_LORA)),
                  pl.BlockSpec((tm, KV_LORA), lambda i: (i, COL_CKV // KV_LORA)),
                  pl.BlockSpec((tm, 2 * LANE), lambda i: (i, COL_KPE // (2 * LANE))),
                  pl.BlockSpec((1, Q_LORA), lambda i: (0, 0)),
                  pl.BlockSpec((1, KV_LORA), lambda i: (0, 0)),
                  pl.BlockSpec(wuq.shape, lambda i: (0, 0)),
                  pl.BlockSpec(wuk.shape, lambda i: (0, 0, 0)),
                  pl.BlockSpec((tm, LANE), lambda i: (i % n_pos, 0)),
                  pl.BlockSpec((tm, LANE), lambda i: (i % n_pos, 0))],
        out_specs=[pl.BlockSpec((MLA_HEADS, tm, QK_PAD), lambda i: (0, i, 0)),
                   pl.BlockSpec((tm, QK_PAD), lambda i: (i, 0)),
                   pl.BlockSpec((tm, KV_LORA), lambda i: (i, 0)),
                   pl.BlockSpec((tm, QK_ROPE), lambda i: (i, 0))],
        out_shape=[jax.ShapeDtypeStruct((MLA_HEADS, n, QK_PAD), BF16),
                   jax.ShapeDtypeStruct((n, QK_PAD), BF16),
                   jax.ShapeDtypeStruct((n, KV_LORA), F32),
                   jax.ShapeDtypeStruct((n, QK_ROPE), F32)],
        compiler_params=_params(("parallel",)),
        name="mla_prep",
    )(p, p, p, qnw, kvnw, wuq, wuk, cos, sin)


def _lanes(x, n):
    if n == LANE:
        return x
    if n % LANE == 0:
        return pltpu.repeat(x, n // LANE, axis=1)
    return jnp.broadcast_to(x[:, 0:1], (x.shape[0], n))


def _attention_kernel(q_ref, k_ref, wuv_ref, o_ref, m_scr, l_scr, acc_scr, *, tq, kb, causal, lk, n_split):
    rows = MLA_HEADS * tq
    part_rows = rows // n_split
    q = q_ref[...].reshape(rows, QK_PAD)
    m_scr[...] = jnp.full((rows, LANE), NEG_INF, F32)
    l_scr[...] = jnp.zeros((rows, LANE), F32)
    acc_scr[...] = jnp.zeros((rows, KV_LORA), F32)

    def block(start, size, mask):
        k = k_ref[pl.ds(start, size), :]
        v = k[:, 0:KV_LORA]
        for part in range(n_split):
            r = slice(part * part_rows, (part + 1) * part_rows)
            s = lax.dot_general(q[r], k, (((1,), (1,)), ((), ())), preferred_element_type=F32) * ATTN_SCALE
            if mask is not None:
                s = jnp.where(mask, s, NEG_INF)
            m_old = m_scr[r, :]
            m_new = jnp.maximum(m_old, jnp.max(s, axis=-1, keepdims=True))
            alpha = jnp.exp(m_old - m_new)
            p = jnp.exp(s - _lanes(m_new, size))
            l_scr[r, :] = alpha * l_scr[r, :] + jnp.sum(p, axis=-1, keepdims=True)
            acc_scr[r, :] = acc_scr[r, :] * _lanes(alpha, KV_LORA) + jnp.dot(
                p.astype(BF16), v, preferred_element_type=F32)
            m_scr[r, :] = m_new

    if causal:
        q_start = pl.program_id(1) * tq
        n_main = q_start // kb

        def main_body(j, carry):
            block(pl.multiple_of(j * kb, kb), kb, None)
            return carry

        lax.fori_loop(0, n_main, main_body, 0)

        def rem_body(j, carry):
            block(pl.multiple_of(n_main * kb + j * tq, tq), tq, None)
            return carry

        lax.fori_loop(0, (q_start - n_main * kb) // tq, rem_body, 0)
        tok = jnp.bitwise_and(lax.broadcasted_iota(jnp.int32, (part_rows, tq), 0), tq - 1)
        col = lax.broadcasted_iota(jnp.int32, (part_rows, tq), 1)
        block(pl.multiple_of(q_start, tq), tq, col // CHUNK <= tok // CHUNK)
    else:
        for j in range(lk // kb):
            block(j * kb, kb, None)

    o = (acc_scr[...] / _lanes(l_scr[...], KV_LORA)).astype(BF16)
    for h in range(MLA_HEADS):
        o_ref[:, h * V_HEAD:(h + 1) * V_HEAD] = jnp.dot(
            o[h * tq:(h + 1) * tq], wuv_ref[h], preferred_element_type=F32).astype(o_ref.dtype)


def attention(q, k, wuv, *, batch, lq, lk, tq, kb, causal):
    nq = lq // tq
    rows = MLA_HEADS * tq
    kern = functools.partial(_attention_kernel, tq=tq, kb=kb, causal=causal, lk=lk,
                             n_split=2 if causal else 1)
    return pl.pallas_call(
        kern,
        grid=(batch, nq),
        in_specs=[pl.BlockSpec((MLA_HEADS, tq, QK_PAD), lambda b, i: (0, b * nq + i, 0)),
                  pl.BlockSpec((lk, QK_PAD), lambda b, i: (b, 0)),
                  pl.BlockSpec(wuv.shape, lambda b, i: (0, 0, 0))],
        out_specs=pl.BlockSpec((tq, MLA_WIDTH), lambda b, i: (b * nq + i, 0)),
        out_shape=jax.ShapeDtypeStruct((batch * lq, MLA_WIDTH), BF16),
        scratch_shapes=[pltpu.VMEM((rows, LANE), F32), pltpu.VMEM((rows, LANE), F32),
                        pltpu.VMEM((rows, KV_LORA), F32)],
        compiler_params=_params(("parallel", "arbitrary")),
        name="attention_causal" if causal else "attention_full",
    )(q, k, wuv)


def _sigmoid(x):
    return 1.0 / (1.0 + jnp.exp(-x))


def _hgrn_kernel(hq_ref, hf_ref, hi_ref, hg_ref, lb_ref, nw_ref, s0_ref, o_ref, sfin_ref,
                 st_scr, oacc_scr, *, chunk, n_chunks):
    t = pl.program_id(2)

    @pl.when(t == 0)
    def _():
        st_scr[...] = s0_ref[0, 0].T

    lb = lb_ref[...]
    log_lb = jnp.log(jnp.maximum(lb, LB_FLOOR))
    log_1m_lb = jnp.log1p(-lb)
    one_m_lb = 1.0 - lb
    nw = nw_ref[...]
    tri = (lax.broadcasted_iota(jnp.int32, (chunk, chunk), 0)
           >= lax.broadcasted_iota(jnp.int32, (chunk, chunk), 1)).astype(F32)
    ones = jnp.ones((HG_DK, HG_DV), BF16)
    row_id = lax.broadcasted_iota(jnp.int32, (chunk, HG_DK), 0)
    n_sub = chunk // HG_SUB

    def chunk_body(c):
        sl = slice(c * chunk, (c + 1) * chunk)
        oacc = oacc_scr.at[sl, :]
        hf = hf_ref[sl, :].astype(F32)
        hq = hq_ref[sl, :].astype(F32)
        v = hi_ref[sl, :].astype(F32)
        hg = hg_ref[sl, :].astype(F32)
        log_sig = jnp.minimum(hf, 0.0) - jnp.log1p(jnp.exp(-jnp.abs(hf)))
        b = log_1m_lb + log_sig
        log_f = jnp.maximum(log_lb, b) + jnp.log1p(jnp.exp(-jnp.abs(log_lb - b)))
        k = one_m_lb * _sigmoid(-hf)
        q = hq * _sigmoid(hq)
        g = jnp.dot(tri, log_f, precision=lax.Precision.HIGHEST, preferred_element_type=F32)
        v16 = v.astype(BF16)
        st = st_scr[...]

        oacc[...] = lax.dot_general((q * jnp.exp(g)).astype(BF16), st.astype(BF16),
                                        (((1,), (1,)), ((), ())), preferred_element_type=F32)
        for i in range(1, n_sub):
            r = i * HG_SUB
            g_edge = g[r - 1:r]
            qt = q[r:r + HG_SUB] * jnp.exp(g[r:r + HG_SUB] - g_edge)
            kt = k[0:r] * jnp.exp(g_edge - g[0:r])
            a = lax.dot_general(qt.astype(BF16), kt.astype(BF16), (((1,), (1,)), ((), ())),
                                preferred_element_type=F32)
            oacc[r:r + HG_SUB, :] += jnp.dot(a.astype(BF16), v16[0:r], preferred_element_type=F32)
        for grp in range(chunk // 8):
            r0 = grp * 8
            r1 = (r0 // HG_SUB + 1) * HG_SUB
            n = r1 - r0
            parts = []
            for s in range(r0, r0 + 8):
                d = jnp.where(row_id[r0:r1] >= s, g[r0:r1] - g[s:s + 1], NEG_INF)
                parts.append(jnp.exp(d) * q[r0:r1] * k[s:s + 1])
            sums = jnp.dot(jnp.concatenate(parts, axis=0).astype(BF16), ones, preferred_element_type=F32)
            upd = sums[0:n] * v[r0:r0 + 1]
            for u in range(1, 8):
                upd = upd + sums[u * n:(u + 1) * n] * v[r0 + u:r0 + u + 1]
            oacc[r0:r1, :] += upd

        g_last = g[chunk - 1:chunk]
        kh = k * jnp.exp(g_last - g)
        st_scr[...] = st * jnp.exp(g_last) + lax.dot_general(
            v16, kh.astype(BF16), (((0,), (0,)), ((), ())), preferred_element_type=F32)

        o_ref[sl, :] = (_rms(oacc[...], nw) * (hg * _sigmoid(hg))).astype(o_ref.dtype)

    for c in range(n_chunks):
        chunk_body(c)

    @pl.when(t == pl.num_programs(2) - 1)
    def _():
        sfin_ref[0, 0] = st_scr[...].T


def hgrn(p, lb, nw, s0, *, batch, seq, chunk, tb):
    nt = seq // tb
    kern = functools.partial(_hgrn_kernel, chunk=chunk, n_chunks=tb // chunk)

    def col(base):
        return lambda b, h, t: (b * nt + t, base // LANE + h)

    return pl.pallas_call(
        kern,
        grid=(batch, HG_HEADS, nt),
        in_specs=[pl.BlockSpec((tb, LANE), col(COL_HQ)),
                  pl.BlockSpec((tb, LANE), col(COL_HF)),
                  pl.BlockSpec((tb, LANE), col(COL_HI)),
                  pl.BlockSpec((tb, LANE), col(COL_HG)),
                  pl.BlockSpec((1, HG_DK), lambda b, h, t: (0, h)),
                  pl.BlockSpec((1, HG_DV), lambda b, h, t: (0, 0)),
                  pl.BlockSpec((1, 1, HG_DK, HG_DV), lambda b, h, t: (b, h, 0, 0))],
        out_specs=[pl.BlockSpec((tb, HG_DV), lambda b, h, t: (b * nt + t, h)),
                   pl.BlockSpec((1, 1, HG_DK, HG_DV), lambda b, h, t: (b, h, 0, 0))],
        out_shape=[jax.ShapeDtypeStruct((batch * seq, HG_WIDTH), BF16),
                   jax.ShapeDtypeStruct((batch, HG_HEADS, HG_DK, HG_DV), F32)],
        scratch_shapes=[pltpu.VMEM((HG_DV, HG_DK), F32), pltpu.VMEM((tb, HG_DV), F32)],
        compiler_params=_params(("parallel", "parallel", "arbitrary")),
        name="hgrn",
    )(p, p, p, p, lb, nw, s0)


def _out_proj_kernel(x_ref, a_ref, b_ref, w_ref, o_ref):
    o_ref[...] = (x_ref[...]
                  + jnp.dot(a_ref[...], w_ref[0:MLA_WIDTH, :], preferred_element_type=F32)
                  + jnp.dot(b_ref[...], w_ref[MLA_WIDTH:MLA_WIDTH + HG_WIDTH, :], preferred_element_type=F32))


def out_proj(x, a, b, w, *, tm):
    n, d = x.shape
    return pl.pallas_call(
        _out_proj_kernel,
        grid=(n // tm,),
        in_specs=[pl.BlockSpec((tm, d), lambda i: (i, 0)),
                  pl.BlockSpec((tm, MLA_WIDTH), lambda i: (i, 0)),
                  pl.BlockSpec((tm, HG_WIDTH), lambda i: (i, 0)),
                  pl.BlockSpec(w.shape, lambda i: (0, 0))],
        out_specs=pl.BlockSpec((tm, d), lambda i: (i, 0)),
        out_shape=jax.ShapeDtypeStruct((n, d), F32),
        compiler_params=_params(("parallel",)),
        name="out_proj",
    )(x, a, b, w)


def _silu(x):
    return x * _sigmoid(x)


def _ffn_kernel(x_ref, nw_ref, wg_ref, wu_ref, wd_ref, o_ref, h_scr, acc_scr):
    j = pl.program_id(1)

    @pl.when(j == 0)
    def _():
        h_scr[...] = _rms(x_ref[...], nw_ref[...]).astype(BF16)
        acc_scr[...] = jnp.zeros_like(acc_scr)

    h = h_scr[...]
    a = _silu(jnp.dot(h, wg_ref[...], preferred_element_type=F32)) * jnp.dot(
        h, wu_ref[...], preferred_element_type=F32)
    acc_scr[...] += jnp.dot(a.astype(BF16), wd_ref[...], preferred_element_type=F32)

    @pl.when(j == pl.num_programs(1) - 1)
    def _():
        o_ref[...] = x_ref[...] + acc_scr[...]


def ffn(x, nw, wg, wu, wd, *, tm, tf):
    n, d = x.shape
    dff = wg.shape[1]
    return pl.pallas_call(
        _ffn_kernel,
        grid=(n // tm, dff // tf),
        in_specs=[pl.BlockSpec((tm, d), lambda i, j: (i, 0)),
                  pl.BlockSpec((1, d), lambda i, j: (0, 0)),
                  pl.BlockSpec((d, tf), lambda i, j: (0, j)),
                  pl.BlockSpec((d, tf), lambda i, j: (0, j)),
                  pl.BlockSpec((tf, d), lambda i, j: (j, 0))],
        out_specs=pl.BlockSpec((tm, d), lambda i, j: (i, 0)),
        out_shape=jax.ShapeDtypeStruct((n, d), F32),
        scratch_shapes=[pltpu.VMEM((tm, d), BF16), pltpu.VMEM((tm, d), F32)],
        compiler_params=_params(("parallel", "arbitrary")),
        name="ffn",
    )(x, nw, wg, wu, wd)


def _moe_kernel(x_ref, nw_ref, wr_ref, wg_ref, wu_ref, wd_ref, fw_ref, o_ref,
                h_scr, acc_scr, i1_scr, i2_scr, w1_scr, w2_scr):
    e = pl.program_id(1)
    j = pl.program_id(2)

    @pl.when((e == 0) & (j == 0))
    def _():
        h = _rms(x_ref[...], nw_ref[...])
        h_scr[...] = h.astype(BF16)
        acc_scr[...] = jnp.zeros_like(acc_scr)
        logits = jnp.dot(h, wr_ref[...], precision=lax.Precision.HIGHEST, preferred_element_type=F32)
        ids = lax.broadcasted_iota(jnp.int32, logits.shape, 1).astype(F32)
        m1 = jnp.max(logits, axis=-1, keepdims=True)
        i1 = jnp.min(jnp.where(logits == m1, ids, float(N_EXPERTS)), axis=-1, keepdims=True)
        rest = jnp.where(ids == i1, -jnp.inf, logits)
        m2 = jnp.max(rest, axis=-1, keepdims=True)
        i2 = jnp.min(jnp.where(rest == m2, ids, float(N_EXPERTS)), axis=-1, keepdims=True)
        t = jnp.exp(m2 - m1)
        i1_scr[...] = i1
        i2_scr[...] = i2
        w1_scr[...] = 1.0 / (1.0 + t)
        w2_scr[...] = t / (1.0 + t)

    h = h_scr[...]
    a = _silu(jnp.dot(h, wg_ref[...], preferred_element_type=F32)) * jnp.dot(
        h, wu_ref[...], preferred_element_type=F32)
    ef = e.astype(F32)
    gate = jnp.where(i1_scr[...] == ef, w1_scr[...], 0.0) + jnp.where(i2_scr[...] == ef, w2_scr[...], 0.0)
    acc_scr[...] += gate * jnp.dot(a.astype(BF16), wd_ref[...], preferred_element_type=F32)

    @pl.when((e == pl.num_programs(1) - 1) & (j == pl.num_programs(2) - 1))
    def _():
        o_ref[...] = _rms(x_ref[...] + acc_scr[...], fw_ref[...])


def moe_final(x, nw, wr, wg, wu, wd, fw, *, tm, tf):
    n, d = x.shape
    n_exp, _, dff = wg.shape
    return pl.pallas_call(
        _moe_kernel,
        grid=(n // tm, n_exp, dff // tf),
        in_specs=[pl.BlockSpec((tm, d), lambda i, e, j: (i, 0)),
                  pl.BlockSpec((1, d), lambda i, e, j: (0, 0)),
                  pl.BlockSpec(wr.shape, lambda i, e, j: (0, 0)),
                  pl.BlockSpec((None, d, tf), lambda i, e, j: (e, 0, j)),
                  pl.BlockSpec((None, d, tf), lambda i, e, j: (e, 0, j)),
                  pl.BlockSpec((None, tf, d), lambda i, e, j: (e, j, 0)),
                  pl.BlockSpec((1, d), lambda i, e, j: (0, 0))],
        out_specs=pl.BlockSpec((tm, d), lambda i, e, j: (i, 0)),
        out_shape=jax.ShapeDtypeStruct((n, d), F32),
        scratch_shapes=[pltpu.VMEM((tm, d), BF16), pltpu.VMEM((tm, d), F32),
                        pltpu.VMEM((tm, 1), F32), pltpu.VMEM((tm, 1), F32),
                        pltpu.VMEM((tm, 1), F32), pltpu.VMEM((tm, 1), F32)],
        compiler_params=_params(("parallel", "arbitrary", "arbitrary")),
        name="moe",
    )(x, nw, wr, wg, wu, wd, fw)


def _swap_halves(t):
    half = t.shape[-1] // 2
    return jnp.concatenate([t[..., half:], t[..., :half]], axis=-1)


def _prep_w_in(w):
    d = w.shape[0]
    k_pe = w[:, Q_LORA + KV_LORA:Q_LORA + KV_LORA + QK_ROPE]
    pad = jnp.zeros((d, LANE - QK_ROPE), w.dtype)
    return jnp.concatenate([w[:, :Q_LORA + KV_LORA], w[:, Q_LORA + KV_LORA + QK_ROPE:],
                            k_pe, pad, _swap_halves(k_pe), pad], axis=1).astype(BF16)


def _prep_w_uq(w):
    w = w.reshape(Q_LORA, MLA_HEADS, QK_NOPE + QK_ROPE)
    nope = w[..., :QK_NOPE]
    rope = w[..., QK_NOPE:]
    pad = jnp.zeros((Q_LORA, MLA_HEADS, LANE - QK_ROPE), w.dtype)
    parts = [nope, jnp.concatenate([rope, pad], -1), jnp.concatenate([_swap_halves(rope), pad], -1)]
    return jnp.concatenate([t.reshape(Q_LORA, MLA_HEADS * LANE) for t in parts], axis=1).astype(BF16)


def _rope_tables(pos):
    half = QK_ROPE // 2
    inv_freq = jnp.exp(jnp.arange(half, dtype=F32) * (-math.log(ROPE_THETA) / half))
    ang = pos.astype(F32)[:, None] * inv_freq[None, :]
    cos, sin = jnp.cos(ang), jnp.sin(ang)
    pad = jnp.zeros((pos.shape[0], LANE - QK_ROPE), F32)
    return jnp.concatenate([cos, cos, pad], -1), jnp.concatenate([-sin, sin, pad], -1)


def _row(v):
    return v.reshape(1, -1)


def kernel(x_prompt, x_sample, cache_ckv, cache_kpe, state_hgrn, attn_norm_w, w_in, q_norm_w, kv_norm_w,
           w_uq, w_uk, w_uv, hg_lower_bounds, hg_norm_w, w_out, ffn_norm_w, w_gate, w_up, w_down,
           w_router, we_gate, we_up, we_down, final_norm_w):
    depth = w_in.shape[0]
    b_p, l_p, d = x_prompt.shape
    b_s, l_s, _ = x_sample.shape
    past = cache_ckv.shape[2]
    n_p, n_s = b_p * l_p, b_s * l_s
    assert depth == 2 and d == D_MODEL

    probs = jax.nn.softmax(hg_lower_bounds.astype(F32), axis=0)
    lower_bounds = jnp.cumsum(probs, axis=0) - probs[0:1]

    cos_p, sin_p = _rope_tables(jnp.arange(l_p))
    cos_s, sin_s = _rope_tables(past + jnp.arange(l_s))
    cos_s, sin_s = jnp.tile(cos_s, (b_s, 1)), jnp.tile(sin_s, (b_s, 1))
    zero_state = jnp.zeros((b_p, HG_HEADS, HG_DK, HG_DV), F32)

    xp = x_prompt.reshape(n_p, d)
    xs = x_sample.reshape(n_s, d)
    outs = {k: [] for k in ("ckv_p", "kpe_p", "st_p", "ckv_s", "kpe_s", "st_s")}

    for l in range(depth):
        w_in_l = _prep_w_in(w_in[l])
        w_uq_l = _prep_w_uq(w_uq[l])
        w_uk_l = jnp.transpose(w_uk[l], (1, 2, 0)).astype(BF16)
        w_uv_l = jnp.transpose(w_uv[l], (1, 0, 2)).astype(BF16)
        w_out_l = w_out[l].astype(BF16)
        lb_l = _row(lower_bounds[l])
        mixers = []
        for (x, n, batch, seq, cos, sin) in ((xp, n_p, b_p, l_p, cos_p, sin_p),
                                             (xs, n_s, b_s, l_s, cos_s, sin_s)):
            tm = min(TOKEN_TILE, n)
            p = norm_matmul(x, _row(attn_norm_w[l]), w_in_l, tm=tm, tn=1024)
            q, kb, ckv, kpe = mla_prep(p, _row(q_norm_w[l]), _row(kv_norm_w[l]), w_uq_l, w_uk_l,
                                       cos, sin, tm=tm)
            if x is xp:
                o_mla = attention(q, kb, w_uv_l, batch=batch, lq=seq, lk=seq, tq=128, kb=512, causal=True)
                o_hg, st = hgrn(p, lb_l, _row(hg_norm_w[l]), zero_state, batch=batch, seq=seq,
                                chunk=CHUNK, tb=min(TOKEN_TILE, seq))
            else:
                past_k = jnp.concatenate(
                    [cache_ckv[l].astype(BF16), cache_kpe[l].astype(BF16),
                     jnp.zeros((batch, past, LANE - QK_ROPE), BF16)], axis=-1)
                keys = jnp.concatenate([past_k, kb.reshape(batch, seq, QK_PAD)], axis=1)
                lk = past + seq
                o_mla = attention(q, keys.reshape(batch * lk, QK_PAD), w_uv_l, batch=batch, lq=seq, lk=lk,
                                  tq=seq, kb=lk, causal=False)
                o_hg, st = hgrn(p, lb_l, _row(hg_norm_w[l]), state_hgrn[l], batch=batch, seq=seq,
                                chunk=seq, tb=seq)
            x1 = out_proj(x, o_mla, o_hg, w_out_l, tm=tm)
            mixers.append((x1, ckv.reshape(batch, seq, KV_LORA), kpe.reshape(batch, seq, QK_ROPE), st))
        (xp, ckv_p, kpe_p, st_p), (xs, ckv_s, kpe_s, st_s) = mixers
        for name, val in (("ckv_p", ckv_p), ("kpe_p", kpe_p), ("st_p", st_p),
                          ("ckv_s", ckv_s), ("kpe_s", kpe_s), ("st_s", st_s)):
            outs[name].append(val)

        i = l // 2
        if l % 2 == 0:
            wg, wu, wd = w_gate[i].astype(BF16), w_up[i].astype(BF16), w_down[i].astype(BF16)
            xp = ffn(xp, _row(ffn_norm_w[l]), wg, wu, wd, tm=min(TOKEN_TILE, n_p), tf=512)
            xs = ffn(xs, _row(ffn_norm_w[l]), wg, wu, wd, tm=min(TOKEN_TILE, n_s), tf=512)
        else:
            wg, wu, wd = we_gate[i].astype(BF16), we_up[i].astype(BF16), we_down[i].astype(BF16)
            xp = moe_final(xp, _row(ffn_norm_w[l]), w_router[i], wg, wu, wd, _row(final_norm_w),
                           tm=min(TOKEN_TILE, n_p), tf=256)
            xs = moe_final(xs, _row(ffn_norm_w[l]), w_router[i], wg, wu, wd, _row(final_norm_w),
                           tm=min(TOKEN_TILE, n_s), tf=256)

    return (xp.reshape(b_p, l_p, d), xs.reshape(b_s, l_s, d),
            jnp.stack(outs["ckv_p"]), jnp.stack(outs["kpe_p"]), jnp.stack(outs["st_p"]),
            jnp.stack(outs["ckv_s"]), jnp.stack(outs["kpe_s"]), jnp.stack(outs["st_s"]))
```

```python
import functools
import math

import jax
import jax.numpy as jnp
from jax import lax
from jax.experimental import pallas as pl
from jax.experimental.pallas import tpu as pltpu

F32 = jnp.float32
BF16 = jnp.bfloat16

D_MODEL = 2048
CHUNK = 64
RMS_EPS = 1e-6
NEG_INF = -1e30
LB_FLOOR = 1e-30

MLA_HEADS = 8
V_HEAD = 128
QK_NOPE = 128
QK_ROPE = 64
Q_LORA = 512
KV_LORA = 256
ROPE_THETA = 10000.0
ATTN_SCALE = (QK_NOPE + QK_ROPE) ** -0.5
MLA_WIDTH = MLA_HEADS * V_HEAD

HG_HEADS = 8
HG_DK = 128
HG_DV = 128
HG_KEY = HG_HEADS * HG_DK
HG_WIDTH = HG_HEADS * HG_DV
HG_SUB = 16

N_EXPERTS = 8

LANE = 128
QK_PAD = KV_LORA + LANE

COL_CQ = 0
COL_CKV = Q_LORA
COL_HQ = Q_LORA + KV_LORA
COL_HF = COL_HQ + HG_KEY
COL_HI = COL_HF + HG_KEY
COL_HG = COL_HI + HG_WIDTH
COL_KPE = COL_HG + HG_WIDTH
IN_EXT = COL_KPE + 2 * LANE

VMEM_LIMIT = 56 * 1024 * 1024
TOKEN_TILE = 512


def _params(semantics):
    return pltpu.CompilerParams(dimension_semantics=semantics, vmem_limit_bytes=VMEM_LIMIT)


def _rms(x, w):
    return x * lax.rsqrt(jnp.mean(x * x, axis=-1, keepdims=True) + RMS_EPS) * w


def _norm_matmul_kernel(x_ref, nw_ref, w_ref, o_ref, h_scr):
    @pl.when(pl.program_id(1) == 0)
    def _():
        h_scr[...] = _rms(x_ref[...], nw_ref[...]).astype(BF16)

    o_ref[...] = jnp.dot(h_scr[...], w_ref[...], preferred_element_type=F32).astype(o_ref.dtype)


def norm_matmul(x, nw, w, *, tm, tn):
    n, d = x.shape
    cols = w.shape[1]
    return pl.pallas_call(
        _norm_matmul_kernel,
        grid=(n // tm, cols // tn),
        in_specs=[pl.BlockSpec((tm, d), lambda i, j: (i, 0)),
                  pl.BlockSpec((1, d), lambda i, j: (0, 0)),
                  pl.BlockSpec((d, tn), lambda i, j: (0, j))],
        out_specs=pl.BlockSpec((tm, tn), lambda i, j: (i, j)),
        out_shape=jax.ShapeDtypeStruct((n, cols), BF16),
        scratch_shapes=[pltpu.VMEM((tm, d), BF16)],
        compiler_params=_params(("arbitrary", "arbitrary")),
        name="in_proj",
    )(x, nw, w)


def _mla_prep_kernel(cq_ref, ckv_ref, kp_ref, qnw_ref, kvnw_ref, wuq_ref, wuk_ref, cos_ref, sin_ref,
                     q_ref, kb_ref, ckv_out_ref, kpe_out_ref):
    cos = cos_ref[...]
    sin = sin_ref[...]
    cqn = _rms(cq_ref[...].astype(F32), qnw_ref[...]).astype(BF16)
    q = jnp.dot(cqn, wuq_ref[...], preferred_element_type=F32)
    hw = MLA_HEADS * LANE
    for h in range(MLA_HEADS):
        sl = slice(h * LANE, (h + 1) * LANE)
        q_lat = jnp.dot(q[:, sl].astype(BF16), wuk_ref[h], preferred_element_type=F32)
        rope = q[:, hw + h * LANE:hw + (h + 1) * LANE] * cos + q[:, 2 * hw + h * LANE:2 * hw + (h + 1) * LANE] * sin
        q_ref[h, :, 0:KV_LORA] = q_lat.astype(BF16)
        q_ref[h, :, KV_LORA:QK_PAD] = rope.astype(BF16)
    ckv = _rms(ckv_ref[...].astype(F32), kvnw_ref[...])
    ckv_out_ref[...] = ckv
    kb_ref[:, 0:KV_LORA] = ckv.astype(BF16)
    kp = kp_ref[...].astype(F32)
    kpe = kp[:, 0:LANE] * cos + kp[:, LANE:2 * LANE] * sin
    kpe_out_ref[...] = kpe[:, 0:QK_ROPE]
    kb_ref[:, KV_LORA:QK_PAD] = kpe.astype(BF16)


def mla_prep(p, qnw, kvnw, wuq, wuk, cos, sin, *, tm):
    n = p.shape[0]
    n_pos = cos.shape[0] // tm
    return pl.pallas_call(
        _mla_prep_kernel,
        grid=(n // tm,),
        in_specs=[pl.BlockSpec((tm, Q_LORA), lambda i: (i, COL_CQ // Q_LORA)),
                  pl.BlockSpec((tm, KV_LORA), lambda i: (i, COL_CKV // KV_LORA)),
                  pl.BlockSpec((tm, 2 * LANE), lambda i: (i, COL_KPE // (2 * LANE))),
                  pl.BlockSpec((1, Q_LORA), lambda i: (0, 0)),
                  pl.BlockSpec((1, KV_LORA), lambda i: (0, 0)),
                  pl.BlockSpec(wuq.shape, lambda i: (0, 0)),
                  pl.BlockSpec(wuk.shape, lambda i: (0, 0, 0)),
                  pl.BlockSpec((tm, LANE), lambda i: (i % n_pos, 0)),
                  pl.BlockSpec((tm, LANE), lambda i: (i % n_pos, 0))],
        out_specs=[pl.BlockSpec((MLA_HEADS, tm, QK_PAD), lambda i: (0, i, 0)),
                   pl.BlockSpec((tm, QK_PAD), lambda i: (i, 0)),
                   pl.BlockSpec((tm, KV_LORA), lambda i: (i, 0)),
                   pl.BlockSpec((tm, QK_ROPE), lambda i: (i, 0))],
        out_shape=[jax.ShapeDtypeStruct((MLA_HEADS, n, QK_PAD), BF16),
                   jax.ShapeDtypeStruct((n, QK_PAD), BF16),
                   jax.ShapeDtypeStruct((n, KV_LORA), F32),
                   jax.ShapeDtypeStruct((n, QK_ROPE), F32)],
        compiler_params=_params(("arbitrary",)),
        name="mla_prep",
    )(p, p, p, qnw, kvnw, wuq, wuk, cos, sin)


def _lanes(x, n):
    if n == LANE:
        return x
    if n % LANE == 0:
        return jnp.concatenate([x] * (n // LANE), axis=1)
    return jnp.broadcast_to(x[:, 0:1], (x.shape[0], n))


def _attention_kernel(q_ref, k_ref, wuv_ref, o_ref, m_scr, l_scr, acc_scr, *, tq, kb, causal, lk, n_split):
    rows = MLA_HEADS * tq
    part_rows = rows // n_split
    q = q_ref[...].reshape(rows, QK_PAD)
    m_scr[...] = jnp.full((rows, LANE), NEG_INF, F32)
    l_scr[...] = jnp.zeros((rows, LANE), F32)
    acc_scr[...] = jnp.zeros((rows, KV_LORA), F32)

    def block(start, size, mask):
        k = k_ref[pl.ds(start, size), :]
        v = k[:, 0:KV_LORA]
        for part in range(n_split):
            r = slice(part * part_rows, (part + 1) * part_rows)
            s = lax.dot_general(q[r], k, (((1,), (1,)), ((), ())), preferred_element_type=F32) * ATTN_SCALE
            if mask is not None:
                s = jnp.where(mask, s, NEG_INF)
            m_old = m_scr[r, :]
            m_new = jnp.maximum(m_old, jnp.max(s, axis=-1, keepdims=True))
            alpha = jnp.exp(m_old - m_new)
            p = jnp.exp(s - _lanes(m_new, size))
            l_scr[r, :] = alpha * l_scr[r, :] + jnp.sum(p, axis=-1, keepdims=True)
            acc_scr[r, :] = acc_scr[r, :] * _lanes(alpha, KV_LORA) + jnp.dot(
                p.astype(BF16), v, preferred_element_type=F32)
            m_scr[r, :] = m_new

    if causal:
        q_start = pl.program_id(1) * tq
        n_main = q_start // kb

        def main_body(j, carry):
            block(pl.multiple_of(j * kb, kb), kb, None)
            return carry

        lax.fori_loop(0, n_main, main_body, 0)

        def rem_body(j, carry):
            block(pl.multiple_of(n_main * kb + j * tq, tq), tq, None)
            return carry

        lax.fori_loop(0, (q_start - n_main * kb) // tq, rem_body, 0)
        tok = jnp.bitwise_and(lax.broadcasted_iota(jnp.int32, (part_rows, tq), 0), tq - 1)
        col = lax.broadcasted_iota(jnp.int32, (part_rows, tq), 1)
        block(pl.multiple_of(q_start, tq), tq, col // CHUNK <= tok // CHUNK)
    else:
        for j in range(lk // kb):
            block(j * kb, kb, None)

    o = (acc_scr[...] / _lanes(l_scr[...], KV_LORA)).astype(BF16)
    for h in range(MLA_HEADS):
        o_ref[:, h * V_HEAD:(h + 1) * V_HEAD] = jnp.dot(
            o[h * tq:(h + 1) * tq], wuv_ref[h], preferred_element_type=F32).astype(o_ref.dtype)


def attention(q, k, wuv, *, batch, lq, lk, tq, kb, causal):
    nq = lq // tq
    rows = MLA_HEADS * tq
    kern = functools.partial(_attention_kernel, tq=tq, kb=kb, causal=causal, lk=lk,
                             n_split=2 if causal else 1)
    return pl.pallas_call(
        kern,
        grid=(batch, nq),
        in_specs=[pl.BlockSpec((MLA_HEADS, tq, QK_PAD), lambda b, i: (0, b * nq + i, 0)),
                  pl.BlockSpec((lk, QK_PAD), lambda b, i: (b, 0)),
                  pl.BlockSpec(wuv.shape, lambda b, i: (0, 0, 0))],
        out_specs=pl.BlockSpec((tq, MLA_WIDTH), lambda b, i: (b * nq + i, 0)),
        out_shape=jax.ShapeDtypeStruct((batch * lq, MLA_WIDTH), BF16),
        scratch_shapes=[pltpu.VMEM((rows, LANE), F32), pltpu.VMEM((rows, LANE), F32),
                        pltpu.VMEM((rows, KV_LORA), F32)],
        compiler_params=_params(("arbitrary", "arbitrary")),
        name="attention_causal" if causal else "attention_full",
    )(q, k, wuv)


def _sigmoid(x):
    return 1.0 / (1.0 + jnp.exp(-x))


def _hgrn_kernel(hq_ref, hf_ref, hi_ref, hg_ref, lb_ref, nw_ref, s0_ref, o_ref, sfin_ref,
                 st_scr, oacc_scr, *, chunk, n_chunks):
    t = pl.program_id(2)

    @pl.when(t == 0)
    def _():
        st_scr[...] = s0_ref[0, 0].T

    lb = lb_ref[...]
    log_lb = jnp.log(jnp.maximum(lb, LB_FLOOR))
    log_1m_lb = jnp.log1p(-lb)
    one_m_lb = 1.0 - lb
    nw = nw_ref[...]
    tri = (lax.broadcasted_iota(jnp.int32, (chunk, chunk), 0)
           >= lax.broadcasted_iota(jnp.int32, (chunk, chunk), 1)).astype(F32)
    ones = jnp.ones((HG_DK, HG_DV), BF16)
    row_id = lax.broadcasted_iota(jnp.int32, (chunk, HG_DK), 0)
    n_sub = chunk // HG_SUB

    def chunk_body(c):
        sl = slice(c * chunk, (c + 1) * chunk)
        oacc = oacc_scr.at[sl, :]
        hf = hf_ref[sl, :].astype(F32)
        hq = hq_ref[sl, :].astype(F32)
        v = hi_ref[sl, :].astype(F32)
        hg = hg_ref[sl, :].astype(F32)
        log_sig = jnp.minimum(hf, 0.0) - jnp.log1p(jnp.exp(-jnp.abs(hf)))
        b = log_1m_lb + log_sig
        log_f = jnp.maximum(log_lb, b) + jnp.log1p(jnp.exp(-jnp.abs(log_lb - b)))
        k = one_m_lb * _sigmoid(-hf)
        q = hq * _sigmoid(hq)
        g = jnp.dot(tri, log_f, precision=lax.Precision.HIGHEST, preferred_element_type=F32)
        v16 = v.astype(BF16)
        st = st_scr[...]

        oacc[...] = lax.dot_general((q * jnp.exp(g)).astype(BF16), st.astype(BF16),
                                        (((1,), (1,)), ((), ())), preferred_element_type=F32)
        for i in range(1, n_sub):
            r = i * HG_SUB
            g_edge = g[r - 1:r]
            qt = q[r:r + HG_SUB] * jnp.exp(g[r:r + HG_SUB] - g_edge)
            kt = k[0:r] * jnp.exp(g_edge - g[0:r])
            a = lax.dot_general(qt.astype(BF16), kt.astype(BF16), (((1,), (1,)), ((), ())),
                                preferred_element_type=F32)
            oacc[r:r + HG_SUB, :] += jnp.dot(a.astype(BF16), v16[0:r], preferred_element_type=F32)
        for grp in range(chunk // 8):
            r0 = grp * 8
            r1 = (r0 // HG_SUB + 1) * HG_SUB
            n = r1 - r0
            parts = []
            for s in range(r0, r0 + 8):
                d = jnp.where(row_id[r0:r1] >= s, g[r0:r1] - g[s:s + 1], NEG_INF)
                parts.append(jnp.exp(d) * q[r0:r1] * k[s:s + 1])
            sums = jnp.dot(jnp.concatenate(parts, axis=0).astype(BF16), ones, preferred_element_type=F32)
            upd = sums[0:n] * v[r0:r0 + 1]
            for u in range(1, 8):
                upd = upd + sums[u * n:(u + 1) * n] * v[r0 + u:r0 + u + 1]
            oacc[r0:r1, :] += upd

        g_last = g[chunk - 1:chunk]
        kh = k * jnp.exp(g_last - g)
        st_scr[...] = st * jnp.exp(g_last) + lax.dot_general(
            v16, kh.astype(BF16), (((0,), (0,)), ((), ())), preferred_element_type=F32)

        o_ref[sl, :] = (_rms(oacc[...], nw) * (hg * _sigmoid(hg))).astype(o_ref.dtype)

    for c in range(n_chunks):
        chunk_body(c)

    @pl.when(t == pl.num_programs(2) - 1)
    def _():
        sfin_ref[0, 0] = st_scr[...].T


def hgrn(p, lb, nw, s0, *, batch, seq, chunk, tb):
    nt = seq // tb
    kern = functools.partial(_hgrn_kernel, chunk=chunk, n_chunks=tb // chunk)

    def col(base):
        return lambda b, h, t: (b * nt + t, base // LANE + h)

    return pl.pallas_call(
        kern,
        grid=(batch, HG_HEADS, nt),
        in_specs=[pl.BlockSpec((tb, LANE), col(COL_HQ)),
                  pl.BlockSpec((tb, LANE), col(COL_HF)),
                  pl.BlockSpec((tb, LANE), col(COL_HI)),
                  pl.BlockSpec((tb, LANE), col(COL_HG)),
                  pl.BlockSpec((1, HG_DK), lambda b, h, t: (0, h)),
                  pl.BlockSpec((1, HG_DV), lambda b, h, t: (0, 0)),
                  pl.BlockSpec((1, 1, HG_DK, HG_DV), lambda b, h, t: (b, h, 0, 0))],
        out_specs=[pl.BlockSpec((tb, HG_DV), lambda b, h, t: (b * nt + t, h)),
                   pl.BlockSpec((1, 1, HG_DK, HG_DV), lambda b, h, t: (b, h, 0, 0))],
        out_shape=[jax.ShapeDtypeStruct((batch * seq, HG_WIDTH), BF16),
                   jax.ShapeDtypeStruct((batch, HG_HEADS, HG_DK, HG_DV), F32)],
        scratch_shapes=[pltpu.VMEM((HG_DV, HG_DK), F32), pltpu.VMEM((tb, HG_DV), F32)],
        compiler_params=_params(("arbitrary", "arbitrary", "arbitrary")),
        name="hgrn",
    )(p, p, p, p, lb, nw, s0)


def _out_proj_kernel(x_ref, a_ref, b_ref, w_ref, o_ref):
    o_ref[...] = (x_ref[...]
                  + jnp.dot(a_ref[...], w_ref[0:MLA_WIDTH, :], preferred_element_type=F32)
                  + jnp.dot(b_ref[...], w_ref[MLA_WIDTH:MLA_WIDTH + HG_WIDTH, :], preferred_element_type=F32))


def out_proj(x, a, b, w, *, tm):
    n, d = x.shape
    return pl.pallas_call(
        _out_proj_kernel,
        grid=(n // tm,),
        in_specs=[pl.BlockSpec((tm, d), lambda i: (i, 0)),
                  pl.BlockSpec((tm, MLA_WIDTH), lambda i: (i, 0)),
                  pl.BlockSpec((tm, HG_WIDTH), lambda i: (i, 0)),
                  pl.BlockSpec(w.shape, lambda i: (0, 0))],
        out_specs=pl.BlockSpec((tm, d), lambda i: (i, 0)),
        out_shape=jax.ShapeDtypeStruct((n, d), F32),
        compiler_params=_params(("arbitrary",)),
        name="out_proj",
    )(x, a, b, w)


def _silu(x):
    return x * _sigmoid(x)


def _ffn_kernel(x_ref, nw_ref, wg_ref, wu_ref, wd_ref, o_ref, h_scr, acc_scr):
    j = pl.program_id(1)

    @pl.when(j == 0)
    def _():
        h_scr[...] = _rms(x_ref[...], nw_ref[...]).astype(BF16)
        acc_scr[...] = jnp.zeros_like(acc_scr)

    h = h_scr[...]
    a = _silu(jnp.dot(h, wg_ref[...], preferred_element_type=F32)) * jnp.dot(
        h, wu_ref[...], preferred_element_type=F32)
    acc_scr[...] += jnp.dot(a.astype(BF16), wd_ref[...], preferred_element_type=F32)

    @pl.when(j == pl.num_programs(1) - 1)
    def _():
        o_ref[...] = x_ref[...] + acc_scr[...]


def ffn(x, nw, wg, wu, wd, *, tm, tf):
    n, d = x.shape
    dff = wg.shape[1]
    return pl.pallas_call(
        _ffn_kernel,
        grid=(n // tm, dff // tf),
        in_specs=[pl.BlockSpec((tm, d), lambda i, j: (i, 0)),
                  pl.BlockSpec((1, d), lambda i, j: (0, 0)),
                  pl.BlockSpec((d, tf), lambda i, j: (0, j)),
                  pl.BlockSpec((d, tf), lambda i, j: (0, j)),
                  pl.BlockSpec((tf, d), lambda i, j: (j, 0))],
        out_specs=pl.BlockSpec((tm, d), lambda i, j: (i, 0)),
        out_shape=jax.ShapeDtypeStruct((n, d), F32),
        scratch_shapes=[pltpu.VMEM((tm, d), BF16), pltpu.VMEM((tm, d), F32)],
        compiler_params=_params(("arbitrary", "arbitrary")),
        name="ffn",
    )(x, nw, wg, wu, wd)


def _moe_route_kernel(x_ref, nw_ref, wr_ref, hp_ref, e1_ref, e2_ref, w1_ref, w2_ref, r1_ref, r2_ref,
                      cnt_ref, cnt_scr):
    @pl.when(pl.program_id(0) == 0)
    def _():
        cnt_scr[...] = jnp.zeros_like(cnt_scr)

    tm = x_ref.shape[0]
    half = x_ref.shape[1] // 2
    h = _rms(x_ref[...], nw_ref[...])
    logits = jnp.dot(h, wr_ref[...], precision=lax.Precision.HIGHEST, preferred_element_type=F32)
    ids = lax.broadcasted_iota(jnp.int32, logits.shape, 1).astype(F32)
    m1 = jnp.max(logits, axis=-1, keepdims=True)
    i1 = jnp.min(jnp.where(logits == m1, ids, float(N_EXPERTS)), axis=-1, keepdims=True)
    rest = jnp.where(ids == i1, -jnp.inf, logits)
    m2 = jnp.max(rest, axis=-1, keepdims=True)
    i2 = jnp.min(jnp.where(rest == m2, ids, float(N_EXPERTS)), axis=-1, keepdims=True)
    t = jnp.exp(m2 - m1)
    e1_ref[...] = i1.astype(jnp.int32)
    e2_ref[...] = i2.astype(jnp.int32)
    w1_ref[...] = 1.0 / (1.0 + t)
    w2_ref[...] = t / (1.0 + t)
    oh1 = (ids == i1).astype(F32)
    oh2 = (ids == i2).astype(F32)
    both = oh1 + oh2
    before = (lax.broadcasted_iota(jnp.int32, (tm, tm), 0)
              > lax.broadcasted_iota(jnp.int32, (tm, tm), 1)).astype(BF16)
    prefix = jnp.dot(before, both.astype(BF16), preferred_element_type=F32) + cnt_scr[...]
    r1_ref[...] = jnp.sum(prefix * oh1, axis=-1, keepdims=True).astype(jnp.int32)
    r2_ref[...] = jnp.sum(prefix * oh2, axis=-1, keepdims=True).astype(jnp.int32)
    cnt_scr[...] += jnp.sum(both, axis=0, keepdims=True)
    cnt_ref[...] = cnt_scr[...].astype(jnp.int32)
    hb = h.astype(BF16).astype(F32)
    hi = pltpu.bitcast(hb[:, 0:half], jnp.uint32)
    lo = pltpu.bitcast(hb[:, half:2 * half], jnp.uint32)
    hp_ref[...] = hi | (lo >> 16)


def moe_route(x, nw, wr, *, tm):
    n, d = x.shape
    col = lambda dt: jax.ShapeDtypeStruct((n, 1), dt)
    col_spec = pl.BlockSpec((tm, 1), lambda i: (i, 0))
    return pl.pallas_call(
        _moe_route_kernel,
        grid=(n // tm,),
        in_specs=[pl.BlockSpec((tm, d), lambda i: (i, 0)),
                  pl.BlockSpec((1, d), lambda i: (0, 0)),
                  pl.BlockSpec(wr.shape, lambda i: (0, 0))],
        out_specs=[pl.BlockSpec((tm, d // 2), lambda i: (i, 0))] + [col_spec] * 6
                  + [pl.BlockSpec((1, N_EXPERTS), lambda i: (0, 0))],
        out_shape=[jax.ShapeDtypeStruct((n, d // 2), jnp.uint32), col(jnp.int32), col(jnp.int32),
                   col(F32), col(F32), col(jnp.int32), col(jnp.int32),
                   jax.ShapeDtypeStruct((1, N_EXPERTS), jnp.int32)],
        scratch_shapes=[pltpu.VMEM((1, N_EXPERTS), F32)],
        compiler_params=_params(("arbitrary",)),
        name="moe_route",
    )(x, nw, wr)


def _row_copy(src, src_row, dst, dst_row, sem):
    return pltpu.make_async_copy(src.at[pl.ds(src_row, 1), :], dst.at[pl.ds(dst_row, 1), :], sem)


def _moe_scatter_kernel(d1_ref, d2_ref, hp_hbm, xs_in_hbm, xs_hbm, sem, *, tm):
    del xs_in_hbm
    base = pl.program_id(0) * tm

    def issue(r, carry):
        _row_copy(hp_hbm, base + r, xs_hbm, d1_ref[0, 0, r], sem).start()
        _row_copy(hp_hbm, base + r, xs_hbm, d2_ref[0, 0, r], sem).start()
        return carry

    lax.fori_loop(0, tm, issue, 0, unroll=8)

    def drain(r, carry):
        _row_copy(hp_hbm, base + r, xs_hbm, d1_ref[0, 0, r], sem).wait()
        _row_copy(hp_hbm, base + r, xs_hbm, d2_ref[0, 0, r], sem).wait()
        return carry

    lax.fori_loop(0, tm, drain, 0, unroll=8)


def moe_scatter(hp, d1, d2, xs, *, tm):
    n = hp.shape[0]
    idx_spec = pl.BlockSpec((1, 1, tm), lambda i: (i, 0, 0), memory_space=pltpu.SMEM)
    any_spec = pl.BlockSpec(memory_space=pl.ANY)
    return pl.pallas_call(
        functools.partial(_moe_scatter_kernel, tm=tm),
        grid=(n // tm,),
        in_specs=[idx_spec, idx_spec, any_spec, any_spec],
        out_specs=any_spec,
        out_shape=jax.ShapeDtypeStruct(xs.shape, xs.dtype),
        scratch_shapes=[pltpu.SemaphoreType.DMA(())],
        input_output_aliases={3: 0},
        compiler_params=_params(("arbitrary",)),
        name="moe_scatter",
    )(d1.reshape(n // tm, 1, tm), d2.reshape(n // tm, 1, tm), hp, xs)


def _moe_expert_kernel(te_ref, na_ref, xs_ref, wg_ref, wu_ref, wd_ref, ys_ref, h_scr, acc_scr):
    i = pl.program_id(0)
    j = pl.program_id(1)
    last = pl.num_programs(1) - 1
    active = i < na_ref[0]
    half = xs_ref.shape[1]

    @pl.when(active & (j == 0))
    def _():
        packed = xs_ref[...]
        h_scr[:, 0:half] = pltpu.bitcast(packed & jnp.uint32(0xFFFF0000), F32).astype(BF16)
        h_scr[:, half:2 * half] = pltpu.bitcast(packed << 16, F32).astype(BF16)
        acc_scr[...] = jnp.zeros_like(acc_scr)

    @pl.when(active)
    def _():
        h = h_scr[...]
        a = _silu(jnp.dot(h, wg_ref[...], preferred_element_type=F32)) * jnp.dot(
            h, wu_ref[...], preferred_element_type=F32)
        acc_scr[...] += jnp.dot(a.astype(BF16), wd_ref[...], preferred_element_type=F32)

    @pl.when(j == last)
    def _():
        ys_ref[...] = jnp.where(active, acc_scr[...], 0.0)


def moe_experts(xs, tile_expert, n_active, wg, wu, wd, *, tm, tf):
    rows, half = xs.shape
    d = 2 * half
    nf = wg.shape[2] // tf

    def w_col(i, j, te, na):
        return (te[i], 0, jnp.where(i < na[0], j, nf - 1))

    def w_row(i, j, te, na):
        return (te[i], jnp.where(i < na[0], j, nf - 1), 0)

    return pl.pallas_call(
        _moe_expert_kernel,
        grid_spec=pltpu.PrefetchScalarGridSpec(
            num_scalar_prefetch=2,
            grid=(rows // tm, nf),
            in_specs=[pl.BlockSpec((tm, half), lambda i, j, te, na: (i, 0)),
                      pl.BlockSpec((None, d, tf), w_col),
                      pl.BlockSpec((None, d, tf), w_col),
                      pl.BlockSpec((None, tf, d), w_row)],
            out_specs=pl.BlockSpec((tm, d), lambda i, j, te, na: (i, 0)),
            scratch_shapes=[pltpu.VMEM((tm, d), BF16), pltpu.VMEM((tm, d), F32)]),
        out_shape=jax.ShapeDtypeStruct((rows, d), F32),
        compiler_params=_params(("arbitrary", "arbitrary")),
        name="moe_experts",
    )(tile_expert, n_active, xs, wg, wu, wd)


def _moe_combine_kernel(d1_ref, d2_ref, x_ref, w1_ref, w2_ref, fw_ref, ys_hbm, o_ref, a_scr, b_scr, sem, *, tm):
    def issue(r, carry):
        _row_copy(ys_hbm, d1_ref[0, 0, r], a_scr, r, sem).start()
        _row_copy(ys_hbm, d2_ref[0, 0, r], b_scr, r, sem).start()
        return carry

    lax.fori_loop(0, tm, issue, 0, unroll=8)

    def drain(r, carry):
        _row_copy(ys_hbm, d1_ref[0, 0, r], a_scr, r, sem).wait()
        _row_copy(ys_hbm, d2_ref[0, 0, r], b_scr, r, sem).wait()
        return carry

    lax.fori_loop(0, tm, drain, 0, unroll=8)
    y = x_ref[...] + (w1_ref[...] * a_scr[...] + w2_ref[...] * b_scr[...])
    o_ref[...] = _rms(y, fw_ref[...])


def moe_combine(x, ys, d1, d2, w1, w2, fw, *, tm):
    n, d = x.shape
    idx_spec = pl.BlockSpec((1, 1, tm), lambda i: (i, 0, 0), memory_space=pltpu.SMEM)
    col_spec = pl.BlockSpec((tm, 1), lambda i: (i, 0))
    return pl.pallas_call(
        functools.partial(_moe_combine_kernel, tm=tm),
        grid=(n // tm,),
        in_specs=[idx_spec, idx_spec, pl.BlockSpec((tm, d), lambda i: (i, 0)), col_spec, col_spec,
                  pl.BlockSpec((1, d), lambda i: (0, 0)), pl.BlockSpec(memory_space=pl.ANY)],
        out_specs=pl.BlockSpec((tm, d), lambda i: (i, 0)),
        out_shape=jax.ShapeDtypeStruct((n, d), F32),
        scratch_shapes=[pltpu.VMEM((tm, d), F32), pltpu.VMEM((tm, d), F32), pltpu.SemaphoreType.DMA(())],
        compiler_params=_params(("arbitrary",)),
        name="moe_combine",
    )(d1.reshape(n // tm, 1, tm), d2.reshape(n // tm, 1, tm), x, w1, w2, fw, ys)


def moe_final(streams, nw, wr, wg, wu, wd, fw, *, tf):
    n_exp = wg.shape[0]
    d = streams[0].shape[1]
    tms = [min(TOKEN_TILE, x.shape[0]) for x in streams]
    tile = TOKEN_TILE
    routed = [moe_route(x, nw, wr, tm=tm) for x, tm in zip(streams, tms)]
    counts = [r[7][0] for r in routed]
    total = sum(counts)
    padded = (total + tile - 1) // tile * tile
    ends = jnp.cumsum(padded)
    starts = ends - padded
    n_rows = sum(2 * x.shape[0] for x in streams) + n_exp * tile
    n_tiles = n_rows // tile
    tile_expert = jnp.minimum(
        jnp.sum(jnp.arange(n_tiles, dtype=jnp.int32)[:, None] * tile >= ends[None, :], axis=1), n_exp - 1
    ).astype(jnp.int32)
    n_active = (ends[n_exp - 1:] // tile).astype(jnp.int32)

    def slot_rows(e, rank, offset):
        table = starts + offset
        return (jnp.sum(jnp.where(e == jnp.arange(n_exp)[None, :], table[None, :], 0), axis=1, keepdims=True)
                + rank).astype(jnp.int32)

    xs = jnp.zeros((n_rows, d // 2), jnp.uint32)
    dests = []
    offset = jnp.zeros((n_exp,), jnp.int32)
    for (hp, e1, e2, w1, w2, r1, r2, cnt), tm in zip(routed, tms):
        d1, d2 = slot_rows(e1, r1, offset), slot_rows(e2, r2, offset)
        xs = moe_scatter(hp, d1, d2, xs, tm=tm)
        dests.append((d1, d2))
        offset = offset + cnt[0]
    ys = moe_experts(xs, tile_expert, n_active, wg, wu, wd, tm=tile, tf=tf)
    return [moe_combine(x, ys, d1, d2, r[3], r[4], fw, tm=tm)
            for x, r, (d1, d2), tm in zip(streams, routed, dests, tms)]


def _swap_halves(t):
    half = t.shape[-1] // 2
    return jnp.concatenate([t[..., half:], t[..., :half]], axis=-1)


def _prep_w_in(w):
    d = w.shape[0]
    k_pe = w[:, Q_LORA + KV_LORA:Q_LORA + KV_LORA + QK_ROPE]
    pad = jnp.zeros((d, LANE - QK_ROPE), w.dtype)
    return jnp.concatenate([w[:, :Q_LORA + KV_LORA], w[:, Q_LORA + KV_LORA + QK_ROPE:],
                            k_pe, pad, _swap_halves(k_pe), pad], axis=1).astype(BF16)


def _prep_w_uq(w):
    w = w.reshape(Q_LORA, MLA_HEADS, QK_NOPE + QK_ROPE)
    nope = w[..., :QK_NOPE]
    rope = w[..., QK_NOPE:]
    pad = jnp.zeros((Q_LORA, MLA_HEADS, LANE - QK_ROPE), w.dtype)
    parts = [nope, jnp.concatenate([rope, pad], -1), jnp.concatenate([_swap_halves(rope), pad], -1)]
    return jnp.concatenate([t.reshape(Q_LORA, MLA_HEADS * LANE) for t in parts], axis=1).astype(BF16)


def _rope_tables(pos):
    half = QK_ROPE // 2
    inv_freq = jnp.exp(jnp.arange(half, dtype=F32) * (-math.log(ROPE_THETA) / half))
    ang = pos.astype(F32)[:, None] * inv_freq[None, :]
    cos, sin = jnp.cos(ang), jnp.sin(ang)
    pad = jnp.zeros((pos.shape[0], LANE - QK_ROPE), F32)
    return jnp.concatenate([cos, cos, pad], -1), jnp.concatenate([-sin, sin, pad], -1)


def _row(v):
    return v.reshape(1, -1)


def kernel(x_prompt, x_sample, cache_ckv, cache_kpe, state_hgrn, attn_norm_w, w_in, q_norm_w, kv_norm_w,
           w_uq, w_uk, w_uv, hg_lower_bounds, hg_norm_w, w_out, ffn_norm_w, w_gate, w_up, w_down,
           w_router, we_gate, we_up, we_down, final_norm_w):
    depth = w_in.shape[0]
    b_p, l_p, d = x_prompt.shape
    b_s, l_s, _ = x_sample.shape
    past = cache_ckv.shape[2]
    n_p, n_s = b_p * l_p, b_s * l_s
    assert depth == 2 and d == D_MODEL

    probs = jax.nn.softmax(hg_lower_bounds.astype(F32), axis=0)
    lower_bounds = jnp.cumsum(probs, axis=0) - probs[0:1]

    cos_p, sin_p = _rope_tables(jnp.arange(l_p))
    cos_s, sin_s = _rope_tables(past + jnp.arange(l_s))
    cos_s, sin_s = jnp.tile(cos_s, (b_s, 1)), jnp.tile(sin_s, (b_s, 1))
    zero_state = jnp.zeros((b_p, HG_HEADS, HG_DK, HG_DV), F32)

    xp = x_prompt.reshape(n_p, d)
    xs = x_sample.reshape(n_s, d)
    outs = {k: [] for k in ("ckv_p", "kpe_p", "st_p", "ckv_s", "kpe_s", "st_s")}

    for l in range(depth):
        w_in_l = _prep_w_in(w_in[l])
        w_uq_l = _prep_w_uq(w_uq[l])
        w_uk_l = jnp.transpose(w_uk[l], (1, 2, 0)).astype(BF16)
        w_uv_l = jnp.transpose(w_uv[l], (1, 0, 2)).astype(BF16)
        w_out_l = w_out[l].astype(BF16)
        lb_l = _row(lower_bounds[l])
        mixers = []
        for (x, n, batch, seq, cos, sin) in ((xp, n_p, b_p, l_p, cos_p, sin_p),
                                             (xs, n_s, b_s, l_s, cos_s, sin_s)):
            tm = min(TOKEN_TILE, n)
            p = norm_matmul(x, _row(attn_norm_w[l]), w_in_l, tm=tm, tn=1024)
            q, kb, ckv, kpe = mla_prep(p, _row(q_norm_w[l]), _row(kv_norm_w[l]), w_uq_l, w_uk_l,
                                       cos, sin, tm=tm)
            if x is xp:
                o_mla = attention(q, kb, w_uv_l, batch=batch, lq=seq, lk=seq, tq=128, kb=512, causal=True)
                o_hg, st = hgrn(p, lb_l, _row(hg_norm_w[l]), zero_state, batch=batch, seq=seq,
                                chunk=CHUNK, tb=min(TOKEN_TILE, seq))
            else:
                past_k = jnp.concatenate(
                    [cache_ckv[l].astype(BF16), cache_kpe[l].astype(BF16),
                     jnp.zeros((batch, past, LANE - QK_ROPE), BF16)], axis=-1)
                keys = jnp.concatenate([past_k, kb.reshape(batch, seq, QK_PAD)], axis=1)
                lk = past + seq
                o_mla = attention(q, keys.reshape(batch * lk, QK_PAD), w_uv_l, batch=batch, lq=seq, lk=lk,
                                  tq=seq, kb=lk, causal=False)
                o_hg, st = hgrn(p, lb_l, _row(hg_norm_w[l]), state_hgrn[l], batch=batch, seq=seq,
                                chunk=seq, tb=seq)
            x1 = out_proj(x, o_mla, o_hg, w_out_l, tm=tm)
            mixers.append((x1, ckv.reshape(batch, seq, KV_LORA), kpe.reshape(batch, seq, QK_ROPE), st))
        (xp, ckv_p, kpe_p, st_p), (xs, ckv_s, kpe_s, st_s) = mixers
        for name, val in (("ckv_p", ckv_p), ("kpe_p", kpe_p), ("st_p", st_p),
                          ("ckv_s", ckv_s), ("kpe_s", kpe_s), ("st_s", st_s)):
            outs[name].append(val)

        i = l // 2
        if l % 2 == 0:
            wg, wu, wd = w_gate[i].astype(BF16), w_up[i].astype(BF16), w_down[i].astype(BF16)
            xp = ffn(xp, _row(ffn_norm_w[l]), wg, wu, wd, tm=min(TOKEN_TILE, n_p), tf=512)
            xs = ffn(xs, _row(ffn_norm_w[l]), wg, wu, wd, tm=min(TOKEN_TILE, n_s), tf=512)
        else:
            wg, wu, wd = we_gate[i].astype(BF16), we_up[i].astype(BF16), we_down[i].astype(BF16)
            xp, xs = moe_final([xp, xs], _row(ffn_norm_w[l]), w_router[i], wg, wu, wd, _row(final_norm_w),
                               tf=256)

    return (xp.reshape(b_p, l_p, d), xs.reshape(b_s, l_s, d),
            jnp.stack(outs["ckv_p"]), jnp.stack(outs["kpe_p"]), jnp.stack(outs["st_p"]),
            jnp.stack(outs["ckv_s"]), jnp.stack(outs["kpe_s"]), jnp.stack(outs["st_s"]))
```

```python
import functools
import math

import jax
import jax.numpy as jnp
from jax import lax
from jax.experimental import pallas as pl
from jax.experimental.pallas import tpu as pltpu

F32 = jnp.float32
BF16 = jnp.bfloat16

D_MODEL = 2048
CHUNK = 64
RMS_EPS = 1e-6
NEG_INF = -1e30
LB_FLOOR = 1e-30

MLA_HEADS = 8
V_HEAD = 128
QK_NOPE = 128
QK_ROPE = 64
Q_LORA = 512
KV_LORA = 256
ROPE_THETA = 10000.0
ATTN_SCALE = (QK_NOPE + QK_ROPE) ** -0.5
MLA_WIDTH = MLA_HEADS * V_HEAD

HG_HEADS = 8
HG_DK = 128
HG_DV = 128
HG_KEY = HG_HEADS * HG_DK
HG_WIDTH = HG_HEADS * HG_DV
HG_SUB = 16
HG_SAFE_EXPONENT = 80.0

N_EXPERTS = 8

LANE = 128
QK_PAD = KV_LORA + LANE

COL_CQ = 0
COL_CKV = Q_LORA
COL_HQ = Q_LORA + KV_LORA
COL_HF = COL_HQ + HG_KEY
COL_HI = COL_HF + HG_KEY
COL_HG = COL_HI + HG_WIDTH
COL_KPE = COL_HG + HG_WIDTH
IN_EXT = COL_KPE + 2 * LANE

VMEM_LIMIT = 56 * 1024 * 1024
TOKEN_TILE = 512


def _params(semantics):
    return pltpu.CompilerParams(dimension_semantics=semantics, vmem_limit_bytes=VMEM_LIMIT)


def _rms(x, w):
    return x * lax.rsqrt(jnp.mean(x * x, axis=-1, keepdims=True) + RMS_EPS) * w


def _norm_matmul_kernel(x_ref, nw_ref, w_ref, o_ref, h_scr):
    @pl.when(pl.program_id(1) == 0)
    def _():
        h_scr[...] = _rms(x_ref[...], nw_ref[...]).astype(BF16)

    o_ref[...] = jnp.dot(h_scr[...], w_ref[...], preferred_element_type=F32).astype(o_ref.dtype)


def norm_matmul(x, nw, w, *, tm, tn):
    n, d = x.shape
    cols = w.shape[1]
    return pl.pallas_call(
        _norm_matmul_kernel,
        grid=(n // tm, cols // tn),
        in_specs=[pl.BlockSpec((tm, d), lambda i, j: (i, 0)),
                  pl.BlockSpec((1, d), lambda i, j: (0, 0)),
                  pl.BlockSpec((d, tn), lambda i, j: (0, j))],
        out_specs=pl.BlockSpec((tm, tn), lambda i, j: (i, j)),
        out_shape=jax.ShapeDtypeStruct((n, cols), BF16),
        scratch_shapes=[pltpu.VMEM((tm, d), BF16)],
        compiler_params=_params(("arbitrary", "arbitrary")),
        name="in_proj",
    )(x, nw, w)


def _mla_prep_kernel(cq_ref, ckv_ref, kp_ref, qnw_ref, kvnw_ref, wuq_ref, wuk_ref, cos_ref, sin_ref,
                     q_ref, kb_ref, ckv_out_ref, kpe_out_ref):
    cos = cos_ref[...]
    sin = sin_ref[...]
    cqn = _rms(cq_ref[...].astype(F32), qnw_ref[...]).astype(BF16)
    q = jnp.dot(cqn, wuq_ref[...], preferred_element_type=F32)
    hw = MLA_HEADS * LANE
    for h in range(MLA_HEADS):
        sl = slice(h * LANE, (h + 1) * LANE)
        q_lat = jnp.dot(q[:, sl].astype(BF16), wuk_ref[h], preferred_element_type=F32)
        rope = q[:, hw + h * LANE:hw + (h + 1) * LANE] * cos + q[:, 2 * hw + h * LANE:2 * hw + (h + 1) * LANE] * sin
        q_ref[h, :, 0:KV_LORA] = q_lat.astype(BF16)
        q_ref[h, :, KV_LORA:QK_PAD] = rope.astype(BF16)
    ckv = _rms(ckv_ref[...].astype(F32), kvnw_ref[...])
    ckv_out_ref[...] = ckv
    kb_ref[:, 0:KV_LORA] = ckv.astype(BF16)
    kp = kp_ref[...].astype(F32)
    kpe = kp[:, 0:LANE] * cos + kp[:, LANE:2 * LANE] * sin
    kpe_out_ref[...] = kpe[:, 0:QK_ROPE]
    kb_ref[:, KV_LORA:QK_PAD] = kpe.astype(BF16)


def mla_prep(p, qnw, kvnw, wuq, wuk, cos, sin, *, tm):
    n = p.shape[0]
    n_pos = cos.shape[0] // tm
    return pl.pallas_call(
        _mla_prep_kernel,
        grid=(n // tm,),
        in_specs=[pl.BlockSpec((tm, Q_LORA), lambda i: (i, COL_CQ // Q_LORA)),
                  pl.BlockSpec((tm, KV_LORA), lambda i: (i, COL_CKV // KV_LORA)),
                  pl.BlockSpec((tm, 2 * LANE), lambda i: (i, COL_KPE // (2 * LANE))),
                  pl.BlockSpec((1, Q_LORA), lambda i: (0, 0)),
                  pl.BlockSpec((1, KV_LORA), lambda i: (0, 0)),
                  pl.BlockSpec(wuq.shape, lambda i: (0, 0)),
                  pl.BlockSpec(wuk.shape, lambda i: (0, 0, 0)),
                  pl.BlockSpec((tm, LANE), lambda i: (i % n_pos, 0)),
                  pl.BlockSpec((tm, LANE), lambda i: (i % n_pos, 0))],
        out_specs=[pl.BlockSpec((MLA_HEADS, tm, QK_PAD), lambda i: (0, i, 0)),
                   pl.BlockSpec((tm, QK_PAD), lambda i: (i, 0)),
                   pl.BlockSpec((tm, KV_LORA), lambda i: (i, 0)),
                   pl.BlockSpec((tm, QK_ROPE), lambda i: (i, 0))],
        out_shape=[jax.ShapeDtypeStruct((MLA_HEADS, n, QK_PAD), BF16),
                   jax.ShapeDtypeStruct((n, QK_PAD), BF16),
                   jax.ShapeDtypeStruct((n, KV_LORA), F32),
                   jax.ShapeDtypeStruct((n, QK_ROPE), F32)],
        compiler_params=_params(("arbitrary",)),
        name="mla_prep",
    )(p, p, p, qnw, kvnw, wuq, wuk, cos, sin)


def _lanes(x, n):
    if n == LANE:
        return x
    if n % LANE == 0:
        return jnp.concatenate([x] * (n // LANE), axis=1)
    return jnp.broadcast_to(x[:, 0:1], (x.shape[0], n))


def _attention_kernel(q_ref, k_ref, wuv_ref, o_ref, m_scr, l_scr, acc_scr, *, tq, kb, causal, lk, n_split):
    rows = MLA_HEADS * tq
    part_rows = rows // n_split
    q = q_ref[...].reshape(rows, QK_PAD)
    m_scr[...] = jnp.full((rows, LANE), NEG_INF, F32)
    l_scr[...] = jnp.zeros((rows, LANE), F32)
    acc_scr[...] = jnp.zeros((rows, KV_LORA), F32)

    def block(start, size, mask):
        k = k_ref[pl.ds(start, size), :]
        v = k[:, 0:KV_LORA]
        for part in range(n_split):
            r = slice(part * part_rows, (part + 1) * part_rows)
            s = lax.dot_general(q[r], k, (((1,), (1,)), ((), ())), preferred_element_type=F32) * ATTN_SCALE
            if mask is not None:
                s = jnp.where(mask, s, NEG_INF)
            m_old = m_scr[r, :]
            m_new = jnp.maximum(m_old, jnp.max(s, axis=-1, keepdims=True))
            alpha = jnp.exp(m_old - m_new)
            p = jnp.exp(s - _lanes(m_new, size))
            l_scr[r, :] = alpha * l_scr[r, :] + jnp.sum(p, axis=-1, keepdims=True)
            acc_scr[r, :] = acc_scr[r, :] * _lanes(alpha, KV_LORA) + jnp.dot(
                p.astype(BF16), v, preferred_element_type=F32)
            m_scr[r, :] = m_new

    if causal:
        q_start = pl.program_id(1) * tq
        n_main = q_start // kb

        def main_body(j, carry):
            block(pl.multiple_of(j * kb, kb), kb, None)
            return carry

        lax.fori_loop(0, n_main, main_body, 0)

        def rem_body(j, carry):
            block(pl.multiple_of(n_main * kb + j * tq, tq), tq, None)
            return carry

        lax.fori_loop(0, (q_start - n_main * kb) // tq, rem_body, 0)
        tok = jnp.bitwise_and(lax.broadcasted_iota(jnp.int32, (part_rows, tq), 0), tq - 1)
        col = lax.broadcasted_iota(jnp.int32, (part_rows, tq), 1)
        block(pl.multiple_of(q_start, tq), tq, col // CHUNK <= tok // CHUNK)
    else:
        for j in range(lk // kb):
            block(j * kb, kb, None)

    o = (acc_scr[...] / _lanes(l_scr[...], KV_LORA)).astype(BF16)
    for h in range(MLA_HEADS):
        o_ref[:, h * V_HEAD:(h + 1) * V_HEAD] = jnp.dot(
            o[h * tq:(h + 1) * tq], wuv_ref[h], preferred_element_type=F32).astype(o_ref.dtype)


def attention(q, k, wuv, *, batch, lq, lk, tq, kb, causal):
    nq = lq // tq
    rows = MLA_HEADS * tq
    kern = functools.partial(_attention_kernel, tq=tq, kb=kb, causal=causal, lk=lk,
                             n_split=2 if causal else 1)
    return pl.pallas_call(
        kern,
        grid=(batch, nq),
        in_specs=[pl.BlockSpec((MLA_HEADS, tq, QK_PAD), lambda b, i: (0, b * nq + i, 0)),
                  pl.BlockSpec((lk, QK_PAD), lambda b, i: (b, 0)),
                  pl.BlockSpec(wuv.shape, lambda b, i: (0, 0, 0))],
        out_specs=pl.BlockSpec((tq, MLA_WIDTH), lambda b, i: (b * nq + i, 0)),
        out_shape=jax.ShapeDtypeStruct((batch * lq, MLA_WIDTH), BF16),
        scratch_shapes=[pltpu.VMEM((rows, LANE), F32), pltpu.VMEM((rows, LANE), F32),
                        pltpu.VMEM((rows, KV_LORA), F32)],
        compiler_params=_params(("arbitrary", "arbitrary")),
        name="attention_causal" if causal else "attention_full",
    )(q, k, wuv)


def _sigmoid(x):
    return 1.0 / (1.0 + jnp.exp(-x))


def _hgrn_kernel(hq_ref, hf_ref, hi_ref, hg_ref, lb_ref, nw_ref, s0_ref, o_ref, sfin_ref,
                 st_scr, oacc_scr, g_scr, q_scr, k_scr, *, chunk, n_chunks):
    t = pl.program_id(2)

    @pl.when(t == 0)
    def _():
        st_scr[...] = s0_ref[0, 0].T

    lb = lb_ref[...]
    log_lb = jnp.log(jnp.maximum(lb, LB_FLOOR))
    log_1m_lb = jnp.log1p(-lb)
    one_m_lb = 1.0 - lb
    tri = (lax.broadcasted_iota(jnp.int32, (chunk, chunk), 0)
           >= lax.broadcasted_iota(jnp.int32, (chunk, chunk), 1))
    tri_f32 = tri.astype(F32)
    ones = jnp.ones((HG_DK, HG_DV), BF16)
    row_id = lax.broadcasted_iota(jnp.int32, (chunk, HG_DK), 0)
    n_sub = chunk // HG_SUB

    hf = hf_ref[...].astype(F32)
    log_sig = jnp.minimum(hf, 0.0) - jnp.log1p(jnp.exp(-jnp.abs(hf)))
    b = log_1m_lb + log_sig
    log_f = jnp.maximum(log_lb, b) + jnp.log1p(jnp.exp(-jnp.abs(log_lb - b)))
    k_scr[...] = one_m_lb * _sigmoid(-hf)
    hq = hq_ref[...].astype(F32)
    q_scr[...] = hq * _sigmoid(hq)
    g_min = None
    for c in range(n_chunks):
        sl = slice(c * chunk, (c + 1) * chunk)
        g = jnp.dot(tri_f32, log_f[sl], precision=lax.Precision.HIGHEST, preferred_element_type=F32)
        g_scr[sl, :] = g
        g_end = g[chunk - 1:chunk]
        g_min = g_end if g_min is None else jnp.minimum(g_min, g_end)
    factorable = jnp.min(g_min) >= -HG_SAFE_EXPONENT

    def state_step(c, g, k, v16):
        g_last = g[chunk - 1:chunk]
        kh = k * jnp.exp(g_last - g)
        st_scr[...] = st_scr[...] * jnp.exp(g_last) + lax.dot_general(
            v16, kh.astype(BF16), (((0,), (0,)), ((), ())), preferred_element_type=F32)

    def chunk_factored(c):
        sl = slice(c * chunk, (c + 1) * chunk)
        g, q, k = g_scr[sl, :], q_scr[sl, :], k_scr[sl, :]
        v16 = hi_ref[sl, :]
        qd = (q * jnp.exp(g)).astype(BF16)
        kd = (k * jnp.exp(-g)).astype(BF16)
        a = lax.dot_general(qd, kd, (((1,), (1,)), ((), ())), preferred_element_type=F32)
        a = jnp.where(tri, a, 0.0).astype(BF16)
        oacc_scr[sl, :] = (lax.dot_general(qd, st_scr[...].astype(BF16), (((1,), (1,)), ((), ())),
                                           preferred_element_type=F32)
                           + jnp.dot(a, v16, preferred_element_type=F32))
        state_step(c, g, k, v16)

    def chunk_exact(c):
        sl = slice(c * chunk, (c + 1) * chunk)
        oacc = oacc_scr.at[sl, :]
        g, q, k = g_scr[sl, :], q_scr[sl, :], k_scr[sl, :]
        v16 = hi_ref[sl, :]
        v = v16.astype(F32)

        oacc[...] = lax.dot_general((q * jnp.exp(g)).astype(BF16), st_scr[...].astype(BF16),
                                    (((1,), (1,)), ((), ())), preferred_element_type=F32)
        for i in range(1, n_sub):
            r = i * HG_SUB
            g_edge = g[r - 1:r]
            qt = q[r:r + HG_SUB] * jnp.exp(g[r:r + HG_SUB] - g_edge)
            kt = k[0:r] * jnp.exp(g_edge - g[0:r])
            a = lax.dot_general(qt.astype(BF16), kt.astype(BF16), (((1,), (1,)), ((), ())),
                                preferred_element_type=F32)
            oacc[r:r + HG_SUB, :] += jnp.dot(a.astype(BF16), v16[0:r], preferred_element_type=F32)
        for grp in range(chunk // 8):
            r0 = grp * 8
            r1 = (r0 // HG_SUB + 1) * HG_SUB
            n = r1 - r0
            parts = []
            for s in range(r0, r0 + 8):
                d = jnp.where(row_id[r0:r1] >= s, g[r0:r1] - g[s:s + 1], NEG_INF)
                parts.append(jnp.exp(d) * q[r0:r1] * k[s:s + 1])
            sums = jnp.dot(jnp.concatenate(parts, axis=0).astype(BF16), ones, preferred_element_type=F32)
            upd = sums[0:n] * v[r0:r0 + 1]
            for u in range(1, 8):
                upd = upd + sums[u * n:(u + 1) * n] * v[r0 + u:r0 + u + 1]
            oacc[r0:r1, :] += upd
        state_step(c, g, k, v16)

    @pl.when(factorable)
    def _():
        for c in range(n_chunks):
            chunk_factored(c)

    @pl.when(jnp.logical_not(factorable))
    def _():
        for c in range(n_chunks):
            chunk_exact(c)

    hg = hg_ref[...].astype(F32)
    o_ref[...] = (_rms(oacc_scr[...], nw_ref[...]) * (hg * _sigmoid(hg))).astype(o_ref.dtype)

    @pl.when(t == pl.num_programs(2) - 1)
    def _():
        sfin_ref[0, 0] = st_scr[...].T


def hgrn(p, lb, nw, s0, *, batch, seq, chunk, tb):
    nt = seq // tb
    kern = functools.partial(_hgrn_kernel, chunk=chunk, n_chunks=tb // chunk)

    def col(base):
        return lambda b, h, t: (b * nt + t, base // LANE + h)

    return pl.pallas_call(
        kern,
        grid=(batch, HG_HEADS, nt),
        in_specs=[pl.BlockSpec((tb, LANE), col(COL_HQ)),
                  pl.BlockSpec((tb, LANE), col(COL_HF)),
                  pl.BlockSpec((tb, LANE), col(COL_HI)),
                  pl.BlockSpec((tb, LANE), col(COL_HG)),
                  pl.BlockSpec((1, HG_DK), lambda b, h, t: (0, h)),
                  pl.BlockSpec((1, HG_DV), lambda b, h, t: (0, 0)),
                  pl.BlockSpec((1, 1, HG_DK, HG_DV), lambda b, h, t: (b, h, 0, 0))],
        out_specs=[pl.BlockSpec((tb, HG_DV), lambda b, h, t: (b * nt + t, h)),
                   pl.BlockSpec((1, 1, HG_DK, HG_DV), lambda b, h, t: (b, h, 0, 0))],
        out_shape=[jax.ShapeDtypeStruct((batch * seq, HG_WIDTH), BF16),
                   jax.ShapeDtypeStruct((batch, HG_HEADS, HG_DK, HG_DV), F32)],
        scratch_shapes=[pltpu.VMEM((HG_DV, HG_DK), F32), pltpu.VMEM((tb, HG_DV), F32),
                        pltpu.VMEM((tb, HG_DK), F32), pltpu.VMEM((tb, HG_DK), F32),
                        pltpu.VMEM((tb, HG_DK), F32)],
        compiler_params=_params(("arbitrary", "arbitrary", "arbitrary")),
        name="hgrn",
    )(p, p, p, p, lb, nw, s0)


def _out_proj_kernel(x_ref, a_ref, b_ref, w_ref, o_ref):
    o_ref[...] = (x_ref[...]
                  + jnp.dot(a_ref[...], w_ref[0:MLA_WIDTH, :], preferred_element_type=F32)
                  + jnp.dot(b_ref[...], w_ref[MLA_WIDTH:MLA_WIDTH + HG_WIDTH, :], preferred_element_type=F32))


def out_proj(x, a, b, w, *, tm):
    n, d = x.shape
    return pl.pallas_call(
        _out_proj_kernel,
        grid=(n // tm,),
        in_specs=[pl.BlockSpec((tm, d), lambda i: (i, 0)),
                  pl.BlockSpec((tm, MLA_WIDTH), lambda i: (i, 0)),
                  pl.BlockSpec((tm, HG_WIDTH), lambda i: (i, 0)),
                  pl.BlockSpec(w.shape, lambda i: (0, 0))],
        out_specs=pl.BlockSpec((tm, d), lambda i: (i, 0)),
        out_shape=jax.ShapeDtypeStruct((n, d), F32),
        compiler_params=_params(("arbitrary",)),
        name="out_proj",
    )(x, a, b, w)


def _silu(x):
    return x * _sigmoid(x)


def _ffn_kernel(x_ref, nw_ref, wg_ref, wu_ref, wd_ref, o_ref, h_scr, acc_scr):
    j = pl.program_id(1)

    @pl.when(j == 0)
    def _():
        h_scr[...] = _rms(x_ref[...], nw_ref[...]).astype(BF16)
        acc_scr[...] = jnp.zeros_like(acc_scr)

    h = h_scr[...]
    a = _silu(jnp.dot(h, wg_ref[...], preferred_element_type=F32)) * jnp.dot(
        h, wu_ref[...], preferred_element_type=F32)
    acc_scr[...] += jnp.dot(a.astype(BF16), wd_ref[...], preferred_element_type=F32)

    @pl.when(j == pl.num_programs(1) - 1)
    def _():
        o_ref[...] = x_ref[...] + acc_scr[...]


def ffn(x, nw, wg, wu, wd, *, tm, tf):
    n, d = x.shape
    dff = wg.shape[1]
    return pl.pallas_call(
        _ffn_kernel,
        grid=(n // tm, dff // tf),
        in_specs=[pl.BlockSpec((tm, d), lambda i, j: (i, 0)),
                  pl.BlockSpec((1, d), lambda i, j: (0, 0)),
                  pl.BlockSpec((d, tf), lambda i, j: (0, j)),
                  pl.BlockSpec((d, tf), lambda i, j: (0, j)),
                  pl.BlockSpec((tf, d), lambda i, j: (j, 0))],
        out_specs=pl.BlockSpec((tm, d), lambda i, j: (i, 0)),
        out_shape=jax.ShapeDtypeStruct((n, d), F32),
        scratch_shapes=[pltpu.VMEM((tm, d), BF16), pltpu.VMEM((tm, d), F32)],
        compiler_params=_params(("arbitrary", "arbitrary")),
        name="ffn",
    )(x, nw, wg, wu, wd)


def _moe_route_kernel(x_ref, nw_ref, wr_ref, hp_ref, e1_ref, e2_ref, w1_ref, w2_ref, r1_ref, r2_ref,
                      cnt_ref, cnt_scr):
    @pl.when(pl.program_id(0) == 0)
    def _():
        cnt_scr[...] = jnp.zeros_like(cnt_scr)

    tm = x_ref.shape[0]
    half = x_ref.shape[1] // 2
    h = _rms(x_ref[...], nw_ref[...])
    logits = jnp.dot(h, wr_ref[...], precision=lax.Precision.HIGHEST, preferred_element_type=F32)
    ids = lax.broadcasted_iota(jnp.int32, logits.shape, 1).astype(F32)
    m1 = jnp.max(logits, axis=-1, keepdims=True)
    i1 = jnp.min(jnp.where(logits == m1, ids, float(N_EXPERTS)), axis=-1, keepdims=True)
    rest = jnp.where(ids == i1, -jnp.inf, logits)
    m2 = jnp.max(rest, axis=-1, keepdims=True)
    i2 = jnp.min(jnp.where(rest == m2, ids, float(N_EXPERTS)), axis=-1, keepdims=True)
    t = jnp.exp(m2 - m1)
    e1_ref[...] = i1.astype(jnp.int32)
    e2_ref[...] = i2.astype(jnp.int32)
    w1_ref[...] = 1.0 / (1.0 + t)
    w2_ref[...] = t / (1.0 + t)
    oh1 = (ids == i1).astype(F32)
    oh2 = (ids == i2).astype(F32)
    both = oh1 + oh2
    before = (lax.broadcasted_iota(jnp.int32, (tm, tm), 0)
              > lax.broadcasted_iota(jnp.int32, (tm, tm), 1)).astype(BF16)
    prefix = jnp.dot(before, both.astype(BF16), preferred_element_type=F32) + cnt_scr[...]
    r1_ref[...] = jnp.sum(prefix * oh1, axis=-1, keepdims=True).astype(jnp.int32)
    r2_ref[...] = jnp.sum(prefix * oh2, axis=-1, keepdims=True).astype(jnp.int32)
    cnt_scr[...] += jnp.sum(both, axis=0, keepdims=True)
    cnt_ref[...] = cnt_scr[...].astype(jnp.int32)
    hb = h.astype(BF16).astype(F32)
    hi = pltpu.bitcast(hb[:, 0:half], jnp.uint32)
    lo = pltpu.bitcast(hb[:, half:2 * half], jnp.uint32)
    hp_ref[...] = hi | (lo >> 16)


def moe_route(x, nw, wr, *, tm):
    n, d = x.shape
    col = lambda dt: jax.ShapeDtypeStruct((n, 1), dt)
    col_spec = pl.BlockSpec((tm, 1), lambda i: (i, 0))
    return pl.pallas_call(
        _moe_route_kernel,
        grid=(n // tm,),
        in_specs=[pl.BlockSpec((tm, d), lambda i: (i, 0)),
                  pl.BlockSpec((1, d), lambda i: (0, 0)),
                  pl.BlockSpec(wr.shape, lambda i: (0, 0))],
        out_specs=[pl.BlockSpec((tm, d // 2), lambda i: (i, 0))] + [col_spec] * 6
                  + [pl.BlockSpec((1, N_EXPERTS), lambda i: (0, 0))],
        out_shape=[jax.ShapeDtypeStruct((n, d // 2), jnp.uint32), col(jnp.int32), col(jnp.int32),
                   col(F32), col(F32), col(jnp.int32), col(jnp.int32),
                   jax.ShapeDtypeStruct((1, N_EXPERTS), jnp.int32)],
        scratch_shapes=[pltpu.VMEM((1, N_EXPERTS), F32)],
        compiler_params=_params(("arbitrary",)),
        name="moe_route",
    )(x, nw, wr)


def _row_copy(src, src_row, dst, dst_row, sem):
    return pltpu.make_async_copy(src.at[pl.ds(src_row, 1), :], dst.at[pl.ds(dst_row, 1), :], sem)


def _moe_scatter_kernel(d1_ref, d2_ref, hp_ref, xs_in_hbm, xs_hbm, sem, *, tm):
    del xs_in_hbm

    def issue(r, carry):
        _row_copy(hp_ref, r, xs_hbm, d1_ref[0, 0, r], sem).start()
        _row_copy(hp_ref, r, xs_hbm, d2_ref[0, 0, r], sem).start()
        return carry

    lax.fori_loop(0, tm, issue, 0, unroll=8)

    def drain(r, carry):
        _row_copy(hp_ref, r, xs_hbm, d1_ref[0, 0, r], sem).wait()
        _row_copy(hp_ref, r, xs_hbm, d2_ref[0, 0, r], sem).wait()
        return carry

    lax.fori_loop(0, tm, drain, 0, unroll=8)


def moe_scatter(hp, d1, d2, xs, *, tm):
    n = hp.shape[0]
    idx_spec = pl.BlockSpec((1, 1, tm), lambda i: (i, 0, 0), memory_space=pltpu.SMEM)
    any_spec = pl.BlockSpec(memory_space=pl.ANY)
    return pl.pallas_call(
        functools.partial(_moe_scatter_kernel, tm=tm),
        grid=(n // tm,),
        in_specs=[idx_spec, idx_spec, pl.BlockSpec((tm, hp.shape[1]), lambda i: (i, 0)), any_spec],
        out_specs=any_spec,
        out_shape=jax.ShapeDtypeStruct(xs.shape, xs.dtype),
        scratch_shapes=[pltpu.SemaphoreType.DMA(())],
        input_output_aliases={3: 0},
        compiler_params=_params(("arbitrary",)),
        name="moe_scatter",
    )(d1.reshape(n // tm, 1, tm), d2.reshape(n // tm, 1, tm), hp, xs)


def _moe_expert_kernel(te_ref, na_ref, xs_ref, wg_ref, wu_ref, wd_ref, ys_ref, h_scr, acc_scr):
    i = pl.program_id(0)
    j = pl.program_id(1)
    last = pl.num_programs(1) - 1
    active = i < na_ref[0]
    half = xs_ref.shape[1]

    @pl.when(active & (j == 0))
    def _():
        packed = xs_ref[...]
        h_scr[:, 0:half] = pltpu.bitcast(packed & jnp.uint32(0xFFFF0000), F32).astype(BF16)
        h_scr[:, half:2 * half] = pltpu.bitcast(packed << 16, F32).astype(BF16)
        acc_scr[...] = jnp.zeros_like(acc_scr)

    @pl.when(active)
    def _():
        h = h_scr[...]
        a = _silu(jnp.dot(h, wg_ref[...], preferred_element_type=F32)) * jnp.dot(
            h, wu_ref[...], preferred_element_type=F32)
        acc_scr[...] += jnp.dot(a.astype(BF16), wd_ref[...], preferred_element_type=F32)

    @pl.when(j == last)
    def _():
        ys_ref[...] = jnp.where(active, acc_scr[...], 0.0)


def moe_experts(xs, tile_expert, n_active, wg, wu, wd, *, tm, tf):
    rows, half = xs.shape
    d = 2 * half
    nf = wg.shape[2] // tf

    def w_col(i, j, te, na):
        return (te[i], 0, jnp.where(i < na[0], j, nf - 1))

    def w_row(i, j, te, na):
        return (te[i], jnp.where(i < na[0], j, nf - 1), 0)

    return pl.pallas_call(
        _moe_expert_kernel,
        grid_spec=pltpu.PrefetchScalarGridSpec(
            num_scalar_prefetch=2,
            grid=(rows // tm, nf),
            in_specs=[pl.BlockSpec((tm, half), lambda i, j, te, na: (i, 0)),
                      pl.BlockSpec((None, d, tf), w_col),
                      pl.BlockSpec((None, d, tf), w_col),
                      pl.BlockSpec((None, tf, d), w_row)],
            out_specs=pl.BlockSpec((tm, d), lambda i, j, te, na: (i, 0)),
            scratch_shapes=[pltpu.VMEM((tm, d), BF16), pltpu.VMEM((tm, d), F32)]),
        out_shape=jax.ShapeDtypeStruct((rows, d), F32),
        compiler_params=_params(("arbitrary", "arbitrary")),
        name="moe_experts",
    )(tile_expert, n_active, xs, wg, wu, wd)


def _moe_combine_kernel(d1_ref, d2_ref, x_ref, w1_ref, w2_ref, fw_ref, ys_hbm, o_ref, a_scr, b_scr, sem, *, tm):
    def issue(r, carry):
        _row_copy(ys_hbm, d1_ref[0, 0, r], a_scr, r, sem).start()
        _row_copy(ys_hbm, d2_ref[0, 0, r], b_scr, r, sem).start()
        return carry

    lax.fori_loop(0, tm, issue, 0, unroll=8)

    def drain(r, carry):
        _row_copy(ys_hbm, d1_ref[0, 0, r], a_scr, r, sem).wait()
        _row_copy(ys_hbm, d2_ref[0, 0, r], b_scr, r, sem).wait()
        return carry

    lax.fori_loop(0, tm, drain, 0, unroll=8)
    y = x_ref[...] + (w1_ref[...] * a_scr[...] + w2_ref[...] * b_scr[...])
    o_ref[...] = _rms(y, fw_ref[...])


def moe_combine(x, ys, d1, d2, w1, w2, fw, *, tm):
    n, d = x.shape
    idx_spec = pl.BlockSpec((1, 1, tm), lambda i: (i, 0, 0), memory_space=pltpu.SMEM)
    col_spec = pl.BlockSpec((tm, 1), lambda i: (i, 0))
    return pl.pallas_call(
        functools.partial(_moe_combine_kernel, tm=tm),
        grid=(n // tm,),
        in_specs=[idx_spec, idx_spec, pl.BlockSpec((tm, d), lambda i: (i, 0)), col_spec, col_spec,
                  pl.BlockSpec((1, d), lambda i: (0, 0)), pl.BlockSpec(memory_space=pl.ANY)],
        out_specs=pl.BlockSpec((tm, d), lambda i: (i, 0)),
        out_shape=jax.ShapeDtypeStruct((n, d), F32),
        scratch_shapes=[pltpu.VMEM((tm, d), F32), pltpu.VMEM((tm, d), F32), pltpu.SemaphoreType.DMA(())],
        compiler_params=_params(("arbitrary",)),
        name="moe_combine",
    )(d1.reshape(n // tm, 1, tm), d2.reshape(n // tm, 1, tm), x, w1, w2, fw, ys)


def moe_final(streams, nw, wr, wg, wu, wd, fw, *, tf):
    n_exp = wg.shape[0]
    d = streams[0].shape[1]
    tms = [min(TOKEN_TILE, x.shape[0]) for x in streams]
    tile = TOKEN_TILE
    routed = [moe_route(x, nw, wr, tm=tm) for x, tm in zip(streams, tms)]
    counts = [r[7][0] for r in routed]
    total = sum(counts)
    padded = (total + tile - 1) // tile * tile
    ends = jnp.cumsum(padded)
    starts = ends - padded
    n_rows = sum(2 * x.shape[0] for x in streams) + n_exp * tile
    n_tiles = n_rows // tile
    tile_expert = jnp.minimum(
        jnp.sum(jnp.arange(n_tiles, dtype=jnp.int32)[:, None] * tile >= ends[None, :], axis=1), n_exp - 1
    ).astype(jnp.int32)
    n_active = (ends[n_exp - 1:] // tile).astype(jnp.int32)

    def slot_rows(e, rank, offset):
        table = starts + offset
        return (jnp.sum(jnp.where(e == jnp.arange(n_exp)[None, :], table[None, :], 0), axis=1, keepdims=True)
                + rank).astype(jnp.int32)

    xs = jnp.zeros((n_rows, d // 2), jnp.uint32)
    dests = []
    offset = jnp.zeros((n_exp,), jnp.int32)
    for (hp, e1, e2, w1, w2, r1, r2, cnt), tm in zip(routed, tms):
        d1, d2 = slot_rows(e1, r1, offset), slot_rows(e2, r2, offset)
        xs = moe_scatter(hp, d1, d2, xs, tm=tm)
        dests.append((d1, d2))
        offset = offset + cnt[0]
    ys = moe_experts(xs, tile_expert, n_active, wg, wu, wd, tm=tile, tf=tf)
    return [moe_combine(x, ys, d1, d2, r[3], r[4], fw, tm=tm)
            for x, r, (d1, d2), tm in zip(streams, routed, dests, tms)]


def _swap_halves(t):
    half = t.shape[-1] // 2
    return jnp.concatenate([t[..., half:], t[..., :half]], axis=-1)


def _prep_w_in(w):
    d = w.shape[0]
    k_pe = w[:, Q_LORA + KV_LORA:Q_LORA + KV_LORA + QK_ROPE]
    pad = jnp.zeros((d, LANE - QK_ROPE), w.dtype)
    return jnp.concatenate([w[:, :Q_LORA + KV_LORA], w[:, Q_LORA + KV_LORA + QK_ROPE:],
                            k_pe, pad, _swap_halves(k_pe), pad], axis=1).astype(BF16)


def _prep_w_uq(w):
    w = w.reshape(Q_LORA, MLA_HEADS, QK_NOPE + QK_ROPE)
    nope = w[..., :QK_NOPE]
    rope = w[..., QK_NOPE:]
    pad = jnp.zeros((Q_LORA, MLA_HEADS, LANE - QK_ROPE), w.dtype)
    parts = [nope, jnp.concatenate([rope, pad], -1), jnp.concatenate([_swap_halves(rope), pad], -1)]
    return jnp.concatenate([t.reshape(Q_LORA, MLA_HEADS * LANE) for t in parts], axis=1).astype(BF16)


def _rope_tables(pos):
    half = QK_ROPE // 2
    inv_freq = jnp.exp(jnp.arange(half, dtype=F32) * (-math.log(ROPE_THETA) / half))
    ang = pos.astype(F32)[:, None] * inv_freq[None, :]
    cos, sin = jnp.cos(ang), jnp.sin(ang)
    pad = jnp.zeros((pos.shape[0], LANE - QK_ROPE), F32)
    return jnp.concatenate([cos, cos, pad], -1), jnp.concatenate([-sin, sin, pad], -1)


def _row(v):
    return v.reshape(1, -1)


def kernel(x_prompt, x_sample, cache_ckv, cache_kpe, state_hgrn, attn_norm_w, w_in, q_norm_w, kv_norm_w,
           w_uq, w_uk, w_uv, hg_lower_bounds, hg_norm_w, w_out, ffn_norm_w, w_gate, w_up, w_down,
           w_router, we_gate, we_up, we_down, final_norm_w):
    depth = w_in.shape[0]
    b_p, l_p, d = x_prompt.shape
    b_s, l_s, _ = x_sample.shape
    past = cache_ckv.shape[2]
    n_p, n_s = b_p * l_p, b_s * l_s
    assert depth == 2 and d == D_MODEL

    probs = jax.nn.softmax(hg_lower_bounds.astype(F32), axis=0)
    lower_bounds = jnp.cumsum(probs, axis=0) - probs[0:1]

    cos_p, sin_p = _rope_tables(jnp.arange(l_p))
    cos_s, sin_s = _rope_tables(past + jnp.arange(l_s))
    cos_s, sin_s = jnp.tile(cos_s, (b_s, 1)), jnp.tile(sin_s, (b_s, 1))
    zero_state = jnp.zeros((b_p, HG_HEADS, HG_DK, HG_DV), F32)

    xp = x_prompt.reshape(n_p, d)
    xs = x_sample.reshape(n_s, d)
    outs = {k: [] for k in ("ckv_p", "kpe_p", "st_p", "ckv_s", "kpe_s", "st_s")}

    for l in range(depth):
        w_in_l = _prep_w_in(w_in[l])
        w_uq_l = _prep_w_uq(w_uq[l])
        w_uk_l = jnp.transpose(w_uk[l], (1, 2, 0)).astype(BF16)
        w_uv_l = jnp.transpose(w_uv[l], (1, 0, 2)).astype(BF16)
        w_out_l = w_out[l].astype(BF16)
        lb_l = _row(lower_bounds[l])
        mixers = []
        for (x, n, batch, seq, cos, sin) in ((xp, n_p, b_p, l_p, cos_p, sin_p),
                                             (xs, n_s, b_s, l_s, cos_s, sin_s)):
            tm = min(TOKEN_TILE, n)
            p = norm_matmul(x, _row(attn_norm_w[l]), w_in_l, tm=tm, tn=1024)
            q, kb, ckv, kpe = mla_prep(p, _row(q_norm_w[l]), _row(kv_norm_w[l]), w_uq_l, w_uk_l,
                                       cos, sin, tm=tm)
            if x is xp:
                o_mla = attention(q, kb, w_uv_l, batch=batch, lq=seq, lk=seq, tq=128, kb=512, causal=True)
                o_hg, st = hgrn(p, lb_l, _row(hg_norm_w[l]), zero_state, batch=batch, seq=seq,
                                chunk=CHUNK, tb=min(TOKEN_TILE, seq))
            else:
                past_k = jnp.concatenate(
                    [cache_ckv[l].astype(BF16), cache_kpe[l].astype(BF16),
                     jnp.zeros((batch, past, LANE - QK_ROPE), BF16)], axis=-1)
                keys = jnp.concatenate([past_k, kb.reshape(batch, seq, QK_PAD)], axis=1)
                lk = past + seq
                o_mla = attention(q, keys.reshape(batch * lk, QK_PAD), w_uv_l, batch=batch, lq=seq, lk=lk,
                                  tq=seq, kb=lk, causal=False)
                o_hg, st = hgrn(p, lb_l, _row(hg_norm_w[l]), state_hgrn[l], batch=batch, seq=seq,
                                chunk=seq, tb=seq)
            x1 = out_proj(x, o_mla, o_hg, w_out_l, tm=tm)
            mixers.append((x1, ckv.reshape(batch, seq, KV_LORA), kpe.reshape(batch, seq, QK_ROPE), st))
        (xp, ckv_p, kpe_p, st_p), (xs, ckv_s, kpe_s, st_s) = mixers
        for name, val in (("ckv_p", ckv_p), ("kpe_p", kpe_p), ("st_p", st_p),
                          ("ckv_s", ckv_s), ("kpe_s", kpe_s), ("st_s", st_s)):
            outs[name].append(val)

        i = l // 2
        if l % 2 == 0:
            wg, wu, wd = w_gate[i].astype(BF16), w_up[i].astype(BF16), w_down[i].astype(BF16)
            xp = ffn(xp, _row(ffn_norm_w[l]), wg, wu, wd, tm=min(TOKEN_TILE, n_p), tf=512)
            xs = ffn(xs, _row(ffn_norm_w[l]), wg, wu, wd, tm=min(TOKEN_TILE, n_s), tf=512)
        else:
            wg, wu, wd = we_gate[i].astype(BF16), we_up[i].astype(BF16), we_down[i].astype(BF16)
            xp, xs = moe_final([xp, xs], _row(ffn_norm_w[l]), w_router[i], wg, wu, wd, _row(final_norm_w),
                               tf=256)

    return (xp.reshape(b_p, l_p, d), xs.reshape(b_s, l_s, d),
            jnp.stack(outs["ckv_p"]), jnp.stack(outs["kpe_p"]), jnp.stack(outs["st_p"]),
            jnp.stack(outs["ckv_s"]), jnp.stack(outs["kpe_s"]), jnp.stack(outs["st_s"]))
```

```python
import functools
import math

import jax
import jax.numpy as jnp
from jax import lax
from jax.experimental import pallas as pl
from jax.experimental.pallas import tpu as pltpu

F32 = jnp.float32
BF16 = jnp.bfloat16

D_MODEL = 2048
CHUNK = 64
RMS_EPS = 1e-6
NEG_INF = -1e30
LB_FLOOR = 1e-30

MLA_HEADS = 8
V_HEAD = 128
QK_NOPE = 128
QK_ROPE = 64
Q_LORA = 512
KV_LORA = 256
ROPE_THETA = 10000.0
ATTN_SCALE = (QK_NOPE + QK_ROPE) ** -0.5
EXP2_SCALE = ATTN_SCALE * math.log2(math.e)
MLA_WIDTH = MLA_HEADS * V_HEAD

HG_HEADS = 8
HG_DK = 128
HG_DV = 128
HG_KEY = HG_HEADS * HG_DK
HG_WIDTH = HG_HEADS * HG_DV
HG_SUB = 16
HG_SAFE_EXPONENT = 80.0

N_EXPERTS = 8

LANE = 128
QK_PAD = KV_LORA + LANE

COL_CQ = 0
COL_CKV = Q_LORA
COL_HQ = Q_LORA + KV_LORA
COL_HF = COL_HQ + HG_KEY
COL_HI = COL_HF + HG_KEY
COL_HG = COL_HI + HG_WIDTH
COL_KPE = COL_HG + HG_WIDTH
IN_EXT = COL_KPE + 2 * LANE

VMEM_LIMIT = 56 * 1024 * 1024
TOKEN_TILE = 512


def _params(semantics):
    return pltpu.CompilerParams(dimension_semantics=semantics, vmem_limit_bytes=VMEM_LIMIT)


def _rms(x, w):
    return x * lax.rsqrt(jnp.mean(x * x, axis=-1, keepdims=True) + RMS_EPS) * w


def _norm_matmul_kernel(x_ref, nw_ref, w_ref, o_ref, h_scr):
    @pl.when(pl.program_id(1) == 0)
    def _():
        h_scr[...] = _rms(x_ref[...], nw_ref[...]).astype(BF16)

    o_ref[...] = jnp.dot(h_scr[...], w_ref[...], preferred_element_type=F32).astype(o_ref.dtype)


def norm_matmul(x, nw, w, *, tm, tn):
    n, d = x.shape
    cols = w.shape[1]
    return pl.pallas_call(
        _norm_matmul_kernel,
        grid=(n // tm, cols // tn),
        in_specs=[pl.BlockSpec((tm, d), lambda i, j: (i, 0)),
                  pl.BlockSpec((1, d), lambda i, j: (0, 0)),
                  pl.BlockSpec((d, tn), lambda i, j: (0, j))],
        out_specs=pl.BlockSpec((tm, tn), lambda i, j: (i, j)),
        out_shape=jax.ShapeDtypeStruct((n, cols), BF16),
        scratch_shapes=[pltpu.VMEM((tm, d), BF16)],
        compiler_params=_params(("arbitrary", "arbitrary")),
        name="in_proj",
    )(x, nw, w)


def _mla_prep_kernel(cq_ref, ckv_ref, kp_ref, qnw_ref, kvnw_ref, wuq_ref, wuk_ref, cos_ref, sin_ref,
                     q_ref, kb_ref, ckv_out_ref, kpe_out_ref):
    cos = cos_ref[...]
    sin = sin_ref[...]
    cqn = _rms(cq_ref[...].astype(F32), qnw_ref[...]).astype(BF16)
    q = jnp.dot(cqn, wuq_ref[...], preferred_element_type=F32)
    hw = MLA_HEADS * LANE
    for h in range(MLA_HEADS):
        sl = slice(h * LANE, (h + 1) * LANE)
        q_lat = jnp.dot(q[:, sl].astype(BF16), wuk_ref[h], preferred_element_type=F32)
        rope = q[:, hw + h * LANE:hw + (h + 1) * LANE] * cos + q[:, 2 * hw + h * LANE:2 * hw + (h + 1) * LANE] * sin
        q_ref[h, :, 0:KV_LORA] = q_lat.astype(BF16)
        q_ref[h, :, KV_LORA:QK_PAD] = rope.astype(BF16)
    ckv = _rms(ckv_ref[...].astype(F32), kvnw_ref[...])
    ckv_out_ref[...] = ckv
    kb_ref[:, 0:KV_LORA] = ckv.astype(BF16)
    kp = kp_ref[...].astype(F32)
    kpe = kp[:, 0:LANE] * cos + kp[:, LANE:2 * LANE] * sin
    kpe_out_ref[...] = kpe[:, 0:QK_ROPE]
    kb_ref[:, KV_LORA:QK_PAD] = kpe.astype(BF16)


def mla_prep(p, qnw, kvnw, wuq, wuk, cos, sin, *, tm):
    n = p.shape[0]
    n_pos = cos.shape[0] // tm
    return pl.pallas_call(
        _mla_prep_kernel,
        grid=(n // tm,),
        in_specs=[pl.BlockSpec((tm, Q_LORA), lambda i: (i, COL_CQ // Q_LORA)),
                  pl.BlockSpec((tm, KV_LORA), lambda i: (i, COL_CKV // KV_LORA)),
                  pl.BlockSpec((tm, 2 * LANE), lambda i: (i, COL_KPE // (2 * LANE))),
                  pl.BlockSpec((1, Q_LORA), lambda i: (0, 0)),
                  pl.BlockSpec((1, KV_LORA), lambda i: (0, 0)),
                  pl.BlockSpec(wuq.shape, lambda i: (0, 0)),
                  pl.BlockSpec(wuk.shape, lambda i: (0, 0, 0)),
                  pl.BlockSpec((tm, LANE), lambda i: (i % n_pos, 0)),
                  pl.BlockSpec((tm, LANE), lambda i: (i % n_pos, 0))],
        out_specs=[pl.BlockSpec((MLA_HEADS, tm, QK_PAD), lambda i: (0, i, 0)),
                   pl.BlockSpec((tm, QK_PAD), lambda i: (i, 0)),
                   pl.BlockSpec((tm, KV_LORA), lambda i: (i, 0)),
                   pl.BlockSpec((tm, QK_ROPE), lambda i: (i, 0))],
        out_shape=[jax.ShapeDtypeStruct((MLA_HEADS, n, QK_PAD), BF16),
                   jax.ShapeDtypeStruct((n, QK_PAD), BF16),
                   jax.ShapeDtypeStruct((n, KV_LORA), F32),
                   jax.ShapeDtypeStruct((n, QK_ROPE), F32)],
        compiler_params=_params(("arbitrary",)),
        name="mla_prep",
    )(p, p, p, qnw, kvnw, wuq, wuk, cos, sin)


def _lanes(x, n):
    if n == LANE:
        return x
    if n % LANE == 0:
        return jnp.concatenate([x] * (n // LANE), axis=1)
    return jnp.broadcast_to(x[:, 0:1], (x.shape[0], n))


def _attention_kernel(q_ref, k_ref, wuv_ref, o_ref, m_scr, l_scr, acc_scr, *, tq, kb, causal, lk, n_split):
    rows = MLA_HEADS * tq
    part_rows = rows // n_split
    q = q_ref[...].reshape(rows, QK_PAD)
    m_scr[...] = jnp.full((rows, LANE), NEG_INF, F32)
    l_scr[...] = jnp.zeros((rows, LANE), F32)
    acc_scr[...] = jnp.zeros((rows, KV_LORA), F32)

    def block(start, size, mask):
        k = k_ref[pl.ds(start, size), :]
        v = k[:, 0:KV_LORA]
        for part in range(n_split):
            r = slice(part * part_rows, (part + 1) * part_rows)
            s = lax.dot_general(q[r], k, (((1,), (1,)), ((), ())), preferred_element_type=F32)
            if mask is not None:
                s = jnp.where(mask, s, NEG_INF)
            m_old = m_scr[r, :]
            m_new = jnp.maximum(m_old, jnp.max(s, axis=-1, keepdims=True))
            alpha = jnp.exp2((m_old - m_new) * EXP2_SCALE)
            p = jnp.exp2((s - _lanes(m_new, size)) * EXP2_SCALE)
            l_scr[r, :] = alpha * l_scr[r, :] + jnp.sum(p, axis=-1, keepdims=True)
            acc_scr[r, :] = acc_scr[r, :] * _lanes(alpha, KV_LORA) + jnp.dot(
                p.astype(BF16), v, preferred_element_type=F32)
            m_scr[r, :] = m_new

    if causal:
        q_start = pl.program_id(1) * tq
        n_wide = q_start // (2 * kb)

        def wide_body(j, carry):
            block(pl.multiple_of(j * 2 * kb, 2 * kb), 2 * kb, None)
            return carry

        lax.fori_loop(0, n_wide, wide_body, 0)
        n_main = q_start // kb

        @pl.when(n_main > 2 * n_wide)
        def _():
            block(pl.multiple_of(n_wide * 2 * kb, 2 * kb), kb, None)

        n_before = (q_start - n_main * kb) // tq
        for r in range(kb // tq):
            width = (r + 1) * tq

            @pl.when(n_before == r)
            def _(r=r, width=width):
                tok = jnp.bitwise_and(lax.broadcasted_iota(jnp.int32, (part_rows, width), 0), tq - 1)
                col = lax.broadcasted_iota(jnp.int32, (part_rows, width), 1)
                block(pl.multiple_of(n_main * kb, kb), width, col < r * tq + (tok // CHUNK + 1) * CHUNK)
    else:
        for j in range(lk // kb):
            block(j * kb, kb, None)

    o = (acc_scr[...] / _lanes(l_scr[...], KV_LORA)).astype(BF16)
    for h in range(MLA_HEADS):
        o_ref[:, h * V_HEAD:(h + 1) * V_HEAD] = jnp.dot(
            o[h * tq:(h + 1) * tq], wuv_ref[h], preferred_element_type=F32).astype(o_ref.dtype)


def attention(q, k, wuv, *, batch, lq, lk, tq, kb, causal):
    nq = lq // tq
    rows = MLA_HEADS * tq
    kern = functools.partial(_attention_kernel, tq=tq, kb=kb, causal=causal, lk=lk,
                             n_split=2 if causal else 1)
    return pl.pallas_call(
        kern,
        grid=(batch, nq),
        in_specs=[pl.BlockSpec((MLA_HEADS, tq, QK_PAD), lambda b, i: (0, b * nq + i, 0)),
                  pl.BlockSpec((lk, QK_PAD), lambda b, i: (b, 0)),
                  pl.BlockSpec(wuv.shape, lambda b, i: (0, 0, 0))],
        out_specs=pl.BlockSpec((tq, MLA_WIDTH), lambda b, i: (b * nq + i, 0)),
        out_shape=jax.ShapeDtypeStruct((batch * lq, MLA_WIDTH), BF16),
        scratch_shapes=[pltpu.VMEM((rows, LANE), F32), pltpu.VMEM((rows, LANE), F32),
                        pltpu.VMEM((rows, KV_LORA), F32)],
        compiler_params=_params(("arbitrary", "arbitrary")),
        name="attention_causal" if causal else "attention_full",
    )(q, k, wuv)


def _sigmoid(x):
    return 1.0 / (1.0 + jnp.exp(-x))


def _hgrn_kernel(hq_ref, hf_ref, hi_ref, hg_ref, lb_ref, nw_ref, s0_ref, o_ref, sfin_ref,
                 st_scr, oacc_scr, g_scr, q_scr, k_scr, *, chunk, n_chunks, n_seq):
    t = pl.program_id(2)

    @pl.when(t == 0)
    def _():
        for sq in range(n_seq):
            st_scr[sq] = s0_ref[sq, 0].T

    lb = lb_ref[...]
    lb_floor = jnp.maximum(lb, LB_FLOOR)
    one_m_lb = 1.0 - lb
    tri = (lax.broadcasted_iota(jnp.int32, (chunk, chunk), 0)
           >= lax.broadcasted_iota(jnp.int32, (chunk, chunk), 1))
    tri_bf16 = jnp.where(tri, 1.0, 0.0).astype(BF16)
    ones = jnp.ones((HG_DK, HG_DV), BF16)
    row_id = lax.broadcasted_iota(jnp.int32, (chunk, HG_DK), 0)
    n_sub = chunk // HG_SUB

    hf = hf_ref[...].astype(F32)
    e = jnp.exp(-jnp.abs(hf))
    big = 1.0 / (1.0 + e)
    small = e * big
    pos = hf >= 0.0
    log_f = jnp.log(lb_floor + one_m_lb * jnp.where(pos, big, small))
    k_scr[...] = one_m_lb * jnp.where(pos, small, big)
    hq = hq_ref[...].astype(F32)
    q_scr[...] = hq * _sigmoid(hq)
    lf_a = log_f.astype(BF16)
    rest = log_f - lf_a.astype(F32)
    lf_b = rest.astype(BF16)
    lf_c = (rest - lf_b.astype(F32)).astype(BF16)
    g_min = None
    for c in range(n_seq * n_chunks):
        sl = slice(c * chunk, (c + 1) * chunk)
        parts = jnp.dot(tri_bf16, jnp.concatenate([lf_a[sl], lf_b[sl], lf_c[sl]], axis=1),
                        preferred_element_type=F32)
        g = parts[:, 0:HG_DK] + parts[:, HG_DK:2 * HG_DK] + parts[:, 2 * HG_DK:3 * HG_DK]
        g_scr[sl, :] = g
        g_end = g[chunk - 1:chunk]
        g_min = g_end if g_min is None else jnp.minimum(g_min, g_end)
    factorable = jnp.min(g_min) >= -HG_SAFE_EXPONENT

    def state_step(c, g, k, v16):
        st = st_scr.at[c // n_chunks]
        g_last = g[chunk - 1:chunk]
        kh = k * jnp.exp(g_last - g)
        st[...] = st[...] * jnp.exp(g_last) + lax.dot_general(
            v16, kh.astype(BF16), (((0,), (0,)), ((), ())), preferred_element_type=F32)

    def chunk_factored(c):
        sl = slice(c * chunk, (c + 1) * chunk)
        g, q, k = g_scr[sl, :], q_scr[sl, :], k_scr[sl, :]
        v16 = hi_ref[sl, :]
        qd = (q * jnp.exp(g)).astype(BF16)
        kd = (k * jnp.exp(-g)).astype(BF16)
        a = lax.dot_general(qd, kd, (((1,), (1,)), ((), ())), preferred_element_type=F32)
        a = jnp.where(tri, a, 0.0).astype(BF16)
        oacc_scr[sl, :] = (lax.dot_general(qd, st_scr[c // n_chunks].astype(BF16), (((1,), (1,)), ((), ())),
                                           preferred_element_type=F32)
                           + jnp.dot(a, v16, preferred_element_type=F32))
        state_step(c, g, k, v16)

    def chunk_exact(c):
        sl = slice(c * chunk, (c + 1) * chunk)
        oacc = oacc_scr.at[sl, :]
        g, q, k = g_scr[sl, :], q_scr[sl, :], k_scr[sl, :]
        v16 = hi_ref[sl, :]
        v = v16.astype(F32)

        oacc[...] = lax.dot_general((q * jnp.exp(g)).astype(BF16), st_scr[c // n_chunks].astype(BF16),
                                    (((1,), (1,)), ((), ())), preferred_element_type=F32)
        for i in range(1, n_sub):
            r = i * HG_SUB
            g_edge = g[r - 1:r]
            qt = q[r:r + HG_SUB] * jnp.exp(g[r:r + HG_SUB] - g_edge)
            kt = k[0:r] * jnp.exp(g_edge - g[0:r])
            a = lax.dot_general(qt.astype(BF16), kt.astype(BF16), (((1,), (1,)), ((), ())),
                                preferred_element_type=F32)
            oacc[r:r + HG_SUB, :] += jnp.dot(a.astype(BF16), v16[0:r], preferred_element_type=F32)
        for grp in range(chunk // 8):
            r0 = grp * 8
            r1 = (r0 // HG_SUB + 1) * HG_SUB
            n = r1 - r0
            parts = []
            for s in range(r0, r0 + 8):
                d = jnp.where(row_id[r0:r1] >= s, g[r0:r1] - g[s:s + 1], NEG_INF)
                parts.append(jnp.exp(d) * q[r0:r1] * k[s:s + 1])
            sums = jnp.dot(jnp.concatenate(parts, axis=0).astype(BF16), ones, preferred_element_type=F32)
            upd = sums[0:n] * v[r0:r0 + 1]
            for u in range(1, 8):
                upd = upd + sums[u * n:(u + 1) * n] * v[r0 + u:r0 + u + 1]
            oacc[r0:r1, :] += upd
        state_step(c, g, k, v16)

    @pl.when(factorable)
    def _():
        for c in range(n_seq * n_chunks):
            chunk_factored(c)

    @pl.when(jnp.logical_not(factorable))
    def _():
        for c in range(n_seq * n_chunks):
            chunk_exact(c)

    hg = hg_ref[...].astype(F32)
    o_ref[...] = (_rms(oacc_scr[...], nw_ref[...]) * (hg * _sigmoid(hg))).astype(o_ref.dtype)

    @pl.when(t == pl.num_programs(2) - 1)
    def _():
        for sq in range(n_seq):
            sfin_ref[sq, 0] = st_scr[sq].T


def hgrn(p, lb, nw, s0, *, batch, seq, chunk, tb, n_seq=1):
    nt = max(seq // tb, 1)
    assert tb == n_seq * seq or (n_seq == 1 and seq % tb == 0)
    kern = functools.partial(_hgrn_kernel, chunk=chunk, n_chunks=tb // (n_seq * chunk), n_seq=n_seq)

    def col(base):
        return lambda b, h, t: (b * nt + t, base // LANE + h)

    return pl.pallas_call(
        kern,
        grid=(batch // n_seq, HG_HEADS, nt),
        in_specs=[pl.BlockSpec((tb, LANE), col(COL_HQ)),
                  pl.BlockSpec((tb, LANE), col(COL_HF)),
                  pl.BlockSpec((tb, LANE), col(COL_HI)),
                  pl.BlockSpec((tb, LANE), col(COL_HG)),
                  pl.BlockSpec((1, HG_DK), lambda b, h, t: (0, h)),
                  pl.BlockSpec((1, HG_DV), lambda b, h, t: (0, 0)),
                  pl.BlockSpec((n_seq, 1, HG_DK, HG_DV), lambda b, h, t: (b, h, 0, 0))],
        out_specs=[pl.BlockSpec((tb, HG_DV), lambda b, h, t: (b * nt + t, h)),
                   pl.BlockSpec((n_seq, 1, HG_DK, HG_DV), lambda b, h, t: (b, h, 0, 0))],
        out_shape=[jax.ShapeDtypeStruct((batch * seq, HG_WIDTH), BF16),
                   jax.ShapeDtypeStruct((batch, HG_HEADS, HG_DK, HG_DV), F32)],
        scratch_shapes=[pltpu.VMEM((n_seq, HG_DV, HG_DK), F32), pltpu.VMEM((tb, HG_DV), F32),
                        pltpu.VMEM((tb, HG_DK), F32), pltpu.VMEM((tb, HG_DK), F32),
                        pltpu.VMEM((tb, HG_DK), F32)],
        compiler_params=_params(("arbitrary", "arbitrary", "arbitrary")),
        name="hgrn",
    )(p, p, p, p, lb, nw, s0)


def _out_proj_kernel(x_ref, a_ref, b_ref, w_ref, o_ref):
    o_ref[...] = (x_ref[...]
                  + jnp.dot(a_ref[...], w_ref[0:MLA_WIDTH, :], preferred_element_type=F32)
                  + jnp.dot(b_ref[...], w_ref[MLA_WIDTH:MLA_WIDTH + HG_WIDTH, :], preferred_element_type=F32))


def out_proj(x, a, b, w, *, tm):
    n, d = x.shape
    return pl.pallas_call(
        _out_proj_kernel,
        grid=(n // tm,),
        in_specs=[pl.BlockSpec((tm, d), lambda i: (i, 0)),
                  pl.BlockSpec((tm, MLA_WIDTH), lambda i: (i, 0)),
                  pl.BlockSpec((tm, HG_WIDTH), lambda i: (i, 0)),
                  pl.BlockSpec(w.shape, lambda i: (0, 0))],
        out_specs=pl.BlockSpec((tm, d), lambda i: (i, 0)),
        out_shape=jax.ShapeDtypeStruct((n, d), F32),
        compiler_params=_params(("arbitrary",)),
        name="out_proj",
    )(x, a, b, w)


def _silu(x):
    return x * _sigmoid(x)


def _ffn_kernel(x_ref, nw_ref, wg_ref, wu_ref, wd_ref, o_ref, h_scr, acc_scr):
    j = pl.program_id(1)

    @pl.when(j == 0)
    def _():
        h_scr[...] = _rms(x_ref[...], nw_ref[...]).astype(BF16)
        acc_scr[...] = jnp.zeros_like(acc_scr)

    h = h_scr[...]
    a = _silu(jnp.dot(h, wg_ref[...], preferred_element_type=F32)) * jnp.dot(
        h, wu_ref[...], preferred_element_type=F32)
    acc_scr[...] += jnp.dot(a.astype(BF16), wd_ref[...], preferred_element_type=F32)

    @pl.when(j == pl.num_programs(1) - 1)
    def _():
        o_ref[...] = x_ref[...] + acc_scr[...]


def ffn(x, nw, wg, wu, wd, *, tm, tf):
    n, d = x.shape
    dff = wg.shape[1]
    return pl.pallas_call(
        _ffn_kernel,
        grid=(n // tm, dff // tf),
        in_specs=[pl.BlockSpec((tm, d), lambda i, j: (i, 0)),
                  pl.BlockSpec((1, d), lambda i, j: (0, 0)),
                  pl.BlockSpec((d, tf), lambda i, j: (0, j)),
                  pl.BlockSpec((d, tf), lambda i, j: (0, j)),
                  pl.BlockSpec((tf, d), lambda i, j: (j, 0))],
        out_specs=pl.BlockSpec((tm, d), lambda i, j: (i, 0)),
        out_shape=jax.ShapeDtypeStruct((n, d), F32),
        scratch_shapes=[pltpu.VMEM((tm, d), BF16), pltpu.VMEM((tm, d), F32)],
        compiler_params=_params(("arbitrary", "arbitrary")),
        name="ffn",
    )(x, nw, wg, wu, wd)


def _moe_route_kernel(x_ref, nw_ref, wr_ref, hp_ref, e1_ref, e2_ref, w1_ref, w2_ref, r1_ref, r2_ref,
                      cnt_ref, cnt_scr):
    @pl.when(pl.program_id(0) == 0)
    def _():
        cnt_scr[...] = jnp.zeros_like(cnt_scr)

    tm = x_ref.shape[0]
    half = x_ref.shape[1] // 2
    h = _rms(x_ref[...], nw_ref[...])
    logits = jnp.dot(h, wr_ref[...], precision=lax.Precision.HIGHEST, preferred_element_type=F32)
    ids = lax.broadcasted_iota(jnp.int32, logits.shape, 1).astype(F32)
    m1 = jnp.max(logits, axis=-1, keepdims=True)
    i1 = jnp.min(jnp.where(logits == m1, ids, float(N_EXPERTS)), axis=-1, keepdims=True)
    rest = jnp.where(ids == i1, -jnp.inf, logits)
    m2 = jnp.max(rest, axis=-1, keepdims=True)
    i2 = jnp.min(jnp.where(rest == m2, ids, float(N_EXPERTS)), axis=-1, keepdims=True)
    t = jnp.exp(m2 - m1)
    e1_ref[...] = i1.astype(jnp.int32)
    e2_ref[...] = i2.astype(jnp.int32)
    w1_ref[...] = 1.0 / (1.0 + t)
    w2_ref[...] = t / (1.0 + t)
    oh1 = (ids == i1).astype(F32)
    oh2 = (ids == i2).astype(F32)
    both = oh1 + oh2
    before = (lax.broadcasted_iota(jnp.int32, (tm, tm), 0)
              > lax.broadcasted_iota(jnp.int32, (tm, tm), 1)).astype(BF16)
    prefix = jnp.dot(before, both.astype(BF16), preferred_element_type=F32) + cnt_scr[...]
    r1_ref[...] = jnp.sum(prefix * oh1, axis=-1, keepdims=True).astype(jnp.int32)
    r2_ref[...] = jnp.sum(prefix * oh2, axis=-1, keepdims=True).astype(jnp.int32)
    cnt_scr[...] += jnp.sum(both, axis=0, keepdims=True)
    cnt_ref[...] = cnt_scr[...].astype(jnp.int32)
    hb = h.astype(BF16).astype(F32)
    hi = pltpu.bitcast(hb[:, 0:half], jnp.uint32)
    lo = pltpu.bitcast(hb[:, half:2 * half], jnp.uint32)
    hp_ref[...] = hi | (lo >> 16)


def moe_route(x, nw, wr, *, tm):
    n, d = x.shape
    col = lambda dt: jax.ShapeDtypeStruct((n, 1), dt)
    col_spec = pl.BlockSpec((tm, 1), lambda i: (i, 0))
    return pl.pallas_call(
        _moe_route_kernel,
        grid=(n // tm,),
        in_specs=[pl.BlockSpec((tm, d), lambda i: (i, 0)),
                  pl.BlockSpec((1, d), lambda i: (0, 0)),
                  pl.BlockSpec(wr.shape, lambda i: (0, 0))],
        out_specs=[pl.BlockSpec((tm, d // 2), lambda i: (i, 0))] + [col_spec] * 6
                  + [pl.BlockSpec((1, N_EXPERTS), lambda i: (0, 0))],
        out_shape=[jax.ShapeDtypeStruct((n, d // 2), jnp.uint32), col(jnp.int32), col(jnp.int32),
                   col(F32), col(F32), col(jnp.int32), col(jnp.int32),
                   jax.ShapeDtypeStruct((1, N_EXPERTS), jnp.int32)],
        scratch_shapes=[pltpu.VMEM((1, N_EXPERTS), F32)],
        compiler_params=_params(("arbitrary",)),
        name="moe_route",
    )(x, nw, wr)


def _row_copy(src, src_row, dst, dst_row, sem):
    return pltpu.make_async_copy(src.at[pl.ds(src_row, 1), :], dst.at[pl.ds(dst_row, 1), :], sem)


def _moe_scatter_kernel(d1_ref, d2_ref, hp_ref, xs_in_hbm, xs_hbm, sem, *, tm):
    del xs_in_hbm

    def issue(r, carry):
        _row_copy(hp_ref, r, xs_hbm, d1_ref[0, 0, r], sem).start()
        _row_copy(hp_ref, r, xs_hbm, d2_ref[0, 0, r], sem).start()
        return carry

    lax.fori_loop(0, tm, issue, 0, unroll=8)

    def drain(r, carry):
        _row_copy(hp_ref, r, xs_hbm, d1_ref[0, 0, r], sem).wait()
        _row_copy(hp_ref, r, xs_hbm, d2_ref[0, 0, r], sem).wait()
        return carry

    lax.fori_loop(0, tm, drain, 0, unroll=8)


def moe_scatter(hp, d1, d2, xs, *, tm):
    n = hp.shape[0]
    idx_spec = pl.BlockSpec((1, 1, tm), lambda i: (i, 0, 0), memory_space=pltpu.SMEM)
    any_spec = pl.BlockSpec(memory_space=pl.ANY)
    return pl.pallas_call(
        functools.partial(_moe_scatter_kernel, tm=tm),
        grid=(n // tm,),
        in_specs=[idx_spec, idx_spec, pl.BlockSpec((tm, hp.shape[1]), lambda i: (i, 0)), any_spec],
        out_specs=any_spec,
        out_shape=jax.ShapeDtypeStruct(xs.shape, xs.dtype),
        scratch_shapes=[pltpu.SemaphoreType.DMA(())],
        input_output_aliases={3: 0},
        compiler_params=_params(("arbitrary",)),
        name="moe_scatter",
    )(d1.reshape(n // tm, 1, tm), d2.reshape(n // tm, 1, tm), hp, xs)


def _moe_expert_kernel(te_ref, na_ref, xs_ref, wg_ref, wu_ref, wd_ref, ys_ref, h_scr, acc_scr):
    i = pl.program_id(0)
    j = pl.program_id(1)
    last = pl.num_programs(1) - 1
    active = i < na_ref[0]
    half = xs_ref.shape[1]

    @pl.when(active & (j == 0))
    def _():
        packed = xs_ref[...]
        h_scr[:, 0:half] = pltpu.bitcast(packed & jnp.uint32(0xFFFF0000), F32).astype(BF16)
        h_scr[:, half:2 * half] = pltpu.bitcast(packed << 16, F32).astype(BF16)
        acc_scr[...] = jnp.zeros_like(acc_scr)

    @pl.when(active)
    def _():
        h = h_scr[...]
        a = _silu(jnp.dot(h, wg_ref[...], preferred_element_type=F32)) * jnp.dot(
            h, wu_ref[...], preferred_element_type=F32)
        acc_scr[...] += jnp.dot(a.astype(BF16), wd_ref[...], preferred_element_type=F32)

    @pl.when(j == last)
    def _():
        ys_ref[...] = jnp.where(active, acc_scr[...], 0.0)


def moe_experts(xs, tile_expert, n_active, wg, wu, wd, *, tm, tf):
    rows, half = xs.shape
    d = 2 * half
    nf = wg.shape[2] // tf

    def w_col(i, j, te, na):
        return (te[i], 0, jnp.where(i < na[0], j, nf - 1))

    def w_row(i, j, te, na):
        return (te[i], jnp.where(i < na[0], j, nf - 1), 0)

    return pl.pallas_call(
        _moe_expert_kernel,
        grid_spec=pltpu.PrefetchScalarGridSpec(
            num_scalar_prefetch=2,
            grid=(rows // tm, nf),
            in_specs=[pl.BlockSpec((tm, half), lambda i, j, te, na: (i, 0)),
                      pl.BlockSpec((None, d, tf), w_col),
                      pl.BlockSpec((None, d, tf), w_col),
                      pl.BlockSpec((None, tf, d), w_row)],
            out_specs=pl.BlockSpec((tm, d), lambda i, j, te, na: (i, 0)),
            scratch_shapes=[pltpu.VMEM((tm, d), BF16), pltpu.VMEM((tm, d), F32)]),
        out_shape=jax.ShapeDtypeStruct((rows, d), F32),
        compiler_params=_params(("arbitrary", "arbitrary")),
        name="moe_experts",
    )(tile_expert, n_active, xs, wg, wu, wd)


def _moe_combine_kernel(d1_ref, d2_ref, x_ref, w1_ref, w2_ref, fw_ref, ys_hbm, o_ref, a_scr, b_scr, sem, *, tm):
    def issue(r, carry):
        _row_copy(ys_hbm, d1_ref[0, 0, r], a_scr, r, sem).start()
        _row_copy(ys_hbm, d2_ref[0, 0, r], b_scr, r, sem).start()
        return carry

    lax.fori_loop(0, tm, issue, 0, unroll=8)

    def drain(r, carry):
        _row_copy(ys_hbm, d1_ref[0, 0, r], a_scr, r, sem).wait()
        _row_copy(ys_hbm, d2_ref[0, 0, r], b_scr, r, sem).wait()
        return carry

    lax.fori_loop(0, tm, drain, 0, unroll=8)
    y = x_ref[...] + (w1_ref[...] * a_scr[...] + w2_ref[...] * b_scr[...])
    o_ref[...] = _rms(y, fw_ref[...])


def moe_combine(x, ys, d1, d2, w1, w2, fw, *, tm):
    n, d = x.shape
    idx_spec = pl.BlockSpec((1, 1, tm), lambda i: (i, 0, 0), memory_space=pltpu.SMEM)
    col_spec = pl.BlockSpec((tm, 1), lambda i: (i, 0))
    return pl.pallas_call(
        functools.partial(_moe_combine_kernel, tm=tm),
        grid=(n // tm,),
        in_specs=[idx_spec, idx_spec, pl.BlockSpec((tm, d), lambda i: (i, 0)), col_spec, col_spec,
                  pl.BlockSpec((1, d), lambda i: (0, 0)), pl.BlockSpec(memory_space=pl.ANY)],
        out_specs=pl.BlockSpec((tm, d), lambda i: (i, 0)),
        out_shape=jax.ShapeDtypeStruct((n, d), F32),
        scratch_shapes=[pltpu.VMEM((tm, d), F32), pltpu.VMEM((tm, d), F32), pltpu.SemaphoreType.DMA(())],
        compiler_params=_params(("arbitrary",)),
        name="moe_combine",
    )(d1.reshape(n // tm, 1, tm), d2.reshape(n // tm, 1, tm), x, w1, w2, fw, ys)


def moe_final(streams, nw, wr, wg, wu, wd, fw, *, tf):
    n_exp = wg.shape[0]
    d = streams[0].shape[1]
    tms = [min(TOKEN_TILE, x.shape[0]) for x in streams]
    tile = TOKEN_TILE
    routed = [moe_route(x, nw, wr, tm=tm) for x, tm in zip(streams, tms)]
    counts = [r[7][0] for r in routed]
    total = sum(counts)
    padded = (total + tile - 1) // tile * tile
    ends = jnp.cumsum(padded)
    starts = ends - padded
    n_rows = sum(2 * x.shape[0] for x in streams) + n_exp * tile
    n_tiles = n_rows // tile
    tile_expert = jnp.minimum(
        jnp.sum(jnp.arange(n_tiles, dtype=jnp.int32)[:, None] * tile >= ends[None, :], axis=1), n_exp - 1
    ).astype(jnp.int32)
    n_active = (ends[n_exp - 1:] // tile).astype(jnp.int32)

    def slot_rows(e, rank, offset):
        table = starts + offset
        return (jnp.sum(jnp.where(e == jnp.arange(n_exp)[None, :], table[None, :], 0), axis=1, keepdims=True)
                + rank).astype(jnp.int32)

    xs = jnp.zeros((n_rows, d // 2), jnp.uint32)
    dests = []
    offset = jnp.zeros((n_exp,), jnp.int32)
    for (hp, e1, e2, w1, w2, r1, r2, cnt), tm in zip(routed, tms):
        d1, d2 = slot_rows(e1, r1, offset), slot_rows(e2, r2, offset)
        xs = moe_scatter(hp, d1, d2, xs, tm=tm)
        dests.append((d1, d2))
        offset = offset + cnt[0]
    ys = moe_experts(xs, tile_expert, n_active, wg, wu, wd, tm=tile, tf=tf)
    return [moe_combine(x, ys, d1, d2, r[3], r[4], fw, tm=tm)
            for x, r, (d1, d2), tm in zip(streams, routed, dests, tms)]


def _swap_halves(t):
    half = t.shape[-1] // 2
    return jnp.concatenate([t[..., half:], t[..., :half]], axis=-1)


def _prep_w_in(w):
    d = w.shape[0]
    k_pe = w[:, Q_LORA + KV_LORA:Q_LORA + KV_LORA + QK_ROPE]
    pad = jnp.zeros((d, LANE - QK_ROPE), w.dtype)
    return jnp.concatenate([w[:, :Q_LORA + KV_LORA], w[:, Q_LORA + KV_LORA + QK_ROPE:],
                            k_pe, pad, _swap_halves(k_pe), pad], axis=1).astype(BF16)


def _prep_w_uq(w):
    w = w.reshape(Q_LORA, MLA_HEADS, QK_NOPE + QK_ROPE)
    nope = w[..., :QK_NOPE]
    rope = w[..., QK_NOPE:]
    pad = jnp.zeros((Q_LORA, MLA_HEADS, LANE - QK_ROPE), w.dtype)
    parts = [nope, jnp.concatenate([rope, pad], -1), jnp.concatenate([_swap_halves(rope), pad], -1)]
    return jnp.concatenate([t.reshape(Q_LORA, MLA_HEADS * LANE) for t in parts], axis=1).astype(BF16)


def _rope_tables(pos):
    half = QK_ROPE // 2
    inv_freq = jnp.exp(jnp.arange(half, dtype=F32) * (-math.log(ROPE_THETA) / half))
    ang = pos.astype(F32)[:, None] * inv_freq[None, :]
    cos, sin = jnp.cos(ang), jnp.sin(ang)
    pad = jnp.zeros((pos.shape[0], LANE - QK_ROPE), F32)
    return jnp.concatenate([cos, cos, pad], -1), jnp.concatenate([-sin, sin, pad], -1)


def _row(v):
    return v.reshape(1, -1)


def kernel(x_prompt, x_sample, cache_ckv, cache_kpe, state_hgrn, attn_norm_w, w_in, q_norm_w, kv_norm_w,
           w_uq, w_uk, w_uv, hg_lower_bounds, hg_norm_w, w_out, ffn_norm_w, w_gate, w_up, w_down,
           w_router, we_gate, we_up, we_down, final_norm_w):
    depth = w_in.shape[0]
    b_p, l_p, d = x_prompt.shape
    b_s, l_s, _ = x_sample.shape
    past = cache_ckv.shape[2]
    n_p, n_s = b_p * l_p, b_s * l_s
    assert depth == 2 and d == D_MODEL

    probs = jax.nn.softmax(hg_lower_bounds.astype(F32), axis=0)
    lower_bounds = jnp.cumsum(probs, axis=0) - probs[0:1]

    cos_p, sin_p = _rope_tables(jnp.arange(l_p))
    cos_s, sin_s = _rope_tables(past + jnp.arange(l_s))
    cos_s, sin_s = jnp.tile(cos_s, (b_s, 1)), jnp.tile(sin_s, (b_s, 1))
    zero_state = jnp.zeros((b_p, HG_HEADS, HG_DK, HG_DV), F32)

    xp = x_prompt.reshape(n_p, d)
    xs = x_sample.reshape(n_s, d)
    outs = {k: [] for k in ("ckv_p", "kpe_p", "st_p", "ckv_s", "kpe_s", "st_s")}

    for l in range(depth):
        w_in_l = _prep_w_in(w_in[l])
        w_uq_l = _prep_w_uq(w_uq[l])
        w_uk_l = jnp.transpose(w_uk[l], (1, 2, 0)).astype(BF16)
        w_uv_l = jnp.transpose(w_uv[l], (1, 0, 2)).astype(BF16)
        w_out_l = w_out[l].astype(BF16)
        lb_l = _row(lower_bounds[l])
        mixers = []
        for (x, n, batch, seq, cos, sin) in ((xp, n_p, b_p, l_p, cos_p, sin_p),
                                             (xs, n_s, b_s, l_s, cos_s, sin_s)):
            tm = min(TOKEN_TILE, n)
            p = norm_matmul(x, _row(attn_norm_w[l]), w_in_l, tm=tm, tn=1024)
            q, kb, ckv, kpe = mla_prep(p, _row(q_norm_w[l]), _row(kv_norm_w[l]), w_uq_l, w_uk_l,
                                       cos, sin, tm=tm)
            if x is xp:
                o_mla = attention(q, kb, w_uv_l, batch=batch, lq=seq, lk=seq, tq=128, kb=512, causal=True)
                o_hg, st = hgrn(p, lb_l, _row(hg_norm_w[l]), zero_state, batch=batch, seq=seq,
                                chunk=CHUNK, tb=min(TOKEN_TILE, seq))
            else:
                past_k = jnp.concatenate(
                    [cache_ckv[l].astype(BF16), cache_kpe[l].astype(BF16),
                     jnp.zeros((batch, past, LANE - QK_ROPE), BF16)], axis=-1)
                keys = jnp.concatenate([past_k, kb.reshape(batch, seq, QK_PAD)], axis=1)
                lk = past + seq
                o_mla = attention(q, keys.reshape(batch * lk, QK_PAD), w_uv_l, batch=batch, lq=seq, lk=lk,
                                  tq=seq, kb=lk, causal=False)
                n_seq = math.gcd(batch, 8)
                o_hg, st = hgrn(p, lb_l, _row(hg_norm_w[l]), state_hgrn[l], batch=batch, seq=seq,
                                chunk=seq, tb=n_seq * seq, n_seq=n_seq)
            x1 = out_proj(x, o_mla, o_hg, w_out_l, tm=tm)
            mixers.append((x1, ckv.reshape(batch, seq, KV_LORA), kpe.reshape(batch, seq, QK_ROPE), st))
        (xp, ckv_p, kpe_p, st_p), (xs, ckv_s, kpe_s, st_s) = mixers
        for name, val in (("ckv_p", ckv_p), ("kpe_p", kpe_p), ("st_p", st_p),
                          ("ckv_s", ckv_s), ("kpe_s", kpe_s), ("st_s", st_s)):
            outs[name].append(val)

        i = l // 2
        if l % 2 == 0:
            wg, wu, wd = w_gate[i].astype(BF16), w_up[i].astype(BF16), w_down[i].astype(BF16)
            xp = ffn(xp, _row(ffn_norm_w[l]), wg, wu, wd, tm=min(TOKEN_TILE, n_p), tf=512)
            xs = ffn(xs, _row(ffn_norm_w[l]), wg, wu, wd, tm=min(TOKEN_TILE, n_s), tf=512)
        else:
            wg, wu, wd = we_gate[i].astype(BF16), we_up[i].astype(BF16), we_down[i].astype(BF16)
            xp, xs = moe_final([xp, xs], _row(ffn_norm_w[l]), w_router[i], wg, wu, wd, _row(final_norm_w),
                               tf=256)

    return (xp.reshape(b_p, l_p, d), xs.reshape(b_s, l_s, d),
            jnp.stack(outs["ckv_p"]), jnp.stack(outs["kpe_p"]), jnp.stack(outs["st_p"]),
            jnp.stack(outs["ckv_s"]), jnp.stack(outs["kpe_s"]), jnp.stack(outs["st_s"]))
```

```python
import functools
import math

import jax
import jax.numpy as jnp
from jax import lax
from jax.experimental import pallas as pl
from jax.experimental.pallas import tpu as pltpu

F32 = jnp.float32
BF16 = jnp.bfloat16

D_MODEL = 2048
CHUNK = 64
RMS_EPS = 1e-6
NEG_INF = -1e30
LB_FLOOR = 1e-30

MLA_HEADS = 8
V_HEAD = 128
QK_NOPE = 128
QK_ROPE = 64
Q_LORA = 512
KV_LORA = 256
ROPE_THETA = 10000.0
ATTN_SCALE = (QK_NOPE + QK_ROPE) ** -0.5
EXP2_SCALE = ATTN_SCALE * math.log2(math.e)
MLA_WIDTH = MLA_HEADS * V_HEAD

HG_HEADS = 8
HG_DK = 128
HG_DV = 128
HG_KEY = HG_HEADS * HG_DK
HG_WIDTH = HG_HEADS * HG_DV
HG_SUB = 16
HG_SAFE_EXPONENT = 80.0

N_EXPERTS = 8

LANE = 128
QK_PAD = KV_LORA + LANE

COL_CQ = 0
COL_CKV = Q_LORA
COL_HQ = Q_LORA + KV_LORA
COL_HF = COL_HQ + HG_KEY
COL_HI = COL_HF + HG_KEY
COL_HG = COL_HI + HG_WIDTH
COL_KPE = COL_HG + HG_WIDTH
IN_EXT = COL_KPE + 2 * LANE

VMEM_LIMIT = 56 * 1024 * 1024
TOKEN_TILE = 512
EXPERT_CHUNK = 256
FFN_CHUNK = 512


def _params(semantics):
    return pltpu.CompilerParams(dimension_semantics=semantics, vmem_limit_bytes=VMEM_LIMIT)


def _rms(x, w):
    return x * lax.rsqrt(jnp.mean(x * x, axis=-1, keepdims=True) + RMS_EPS) * w


def _norm_matmul_kernel(x_ref, nw_ref, w_hbm, o_ref, w_vmem, sem, *, tn):
    @pl.when(pl.program_id(0) == 0)
    def _():
        cp = pltpu.make_async_copy(w_hbm, w_vmem, sem)
        cp.start()
        cp.wait()

    h = _rms(x_ref[...], nw_ref[...]).astype(BF16)
    for c in range(o_ref.shape[1] // tn):
        o_ref[:, c * tn:(c + 1) * tn] = jnp.dot(
            h, w_vmem[:, c * tn:(c + 1) * tn], preferred_element_type=F32).astype(o_ref.dtype)


def norm_matmul(x, nw, w, *, tm, tn):
    n, d = x.shape
    cols = w.shape[1]
    return pl.pallas_call(
        functools.partial(_norm_matmul_kernel, tn=tn),
        grid=(n // tm,),
        in_specs=[pl.BlockSpec((tm, d), lambda i: (i, 0)),
                  pl.BlockSpec((1, d), lambda i: (0, 0)),
                  pl.BlockSpec(memory_space=pl.ANY)],
        out_specs=pl.BlockSpec((tm, cols), lambda i: (i, 0)),
        out_shape=jax.ShapeDtypeStruct((n, cols), BF16),
        scratch_shapes=[pltpu.VMEM((d, cols), BF16), pltpu.SemaphoreType.DMA(())],
        compiler_params=_params(("arbitrary",)),
        name="in_proj",
    )(x, nw, w)


def _mla_prep_kernel(cq_ref, ckv_ref, kp_ref, qnw_ref, kvnw_ref, wuq_ref, wuk_ref, cos_ref, sin_ref,
                     q_ref, kb_ref, ckv_out_ref, kpe_out_ref):
    cos = cos_ref[...]
    sin = sin_ref[...]
    cqn = _rms(cq_ref[...].astype(F32), qnw_ref[...]).astype(BF16)
    q = jnp.dot(cqn, wuq_ref[...], preferred_element_type=F32)
    hw = MLA_HEADS * LANE
    for h in range(MLA_HEADS):
        sl = slice(h * LANE, (h + 1) * LANE)
        q_lat = jnp.dot(q[:, sl].astype(BF16), wuk_ref[h], preferred_element_type=F32)
        rope = q[:, hw + h * LANE:hw + (h + 1) * LANE] * cos + q[:, 2 * hw + h * LANE:2 * hw + (h + 1) * LANE] * sin
        q_ref[h, :, 0:KV_LORA] = q_lat.astype(BF16)
        q_ref[h, :, KV_LORA:QK_PAD] = rope.astype(BF16)
    ckv = _rms(ckv_ref[...].astype(F32), kvnw_ref[...])
    ckv_out_ref[...] = ckv
    kb_ref[:, 0:KV_LORA] = ckv.astype(BF16)
    kp = kp_ref[...].astype(F32)
    kpe = kp[:, 0:LANE] * cos + kp[:, LANE:2 * LANE] * sin
    kpe_out_ref[...] = kpe[:, 0:QK_ROPE]
    kb_ref[:, KV_LORA:QK_PAD] = kpe.astype(BF16)


def mla_prep(p, qnw, kvnw, wuq, wuk, cos, sin, *, tm):
    n = p.shape[0]
    n_pos = cos.shape[0] // tm
    return pl.pallas_call(
        _mla_prep_kernel,
        grid=(n // tm,),
        in_specs=[pl.BlockSpec((tm, Q_LORA), lambda i: (i, COL_CQ // Q_LORA)),
                  pl.BlockSpec((tm, KV_LORA), lambda i: (i, COL_CKV // KV_LORA)),
                  pl.BlockSpec((tm, 2 * LANE), lambda i: (i, COL_KPE // (2 * LANE))),
                  pl.BlockSpec((1, Q_LORA), lambda i: (0, 0)),
                  pl.BlockSpec((1, KV_LORA), lambda i: (0, 0)),
                  pl.BlockSpec(wuq.shape, lambda i: (0, 0)),
                  pl.BlockSpec(wuk.shape, lambda i: (0, 0, 0)),
                  pl.BlockSpec((tm, LANE), lambda i: (i % n_pos, 0)),
                  pl.BlockSpec((tm, LANE), lambda i: (i % n_pos, 0))],
        out_specs=[pl.BlockSpec((MLA_HEADS, tm, QK_PAD), lambda i: (0, i, 0)),
                   pl.BlockSpec((tm, QK_PAD), lambda i: (i, 0)),
                   pl.BlockSpec((tm, KV_LORA), lambda i: (i, 0)),
                   pl.BlockSpec((tm, QK_ROPE), lambda i: (i, 0))],
        out_shape=[jax.ShapeDtypeStruct((MLA_HEADS, n, QK_PAD), BF16),
                   jax.ShapeDtypeStruct((n, QK_PAD), BF16),
                   jax.ShapeDtypeStruct((n, KV_LORA), F32),
                   jax.ShapeDtypeStruct((n, QK_ROPE), F32)],
        compiler_params=_params(("arbitrary",)),
        name="mla_prep",
    )(p, p, p, qnw, kvnw, wuq, wuk, cos, sin)


def _lanes(x, n):
    if n == LANE:
        return x
    if n % LANE == 0:
        return jnp.concatenate([x] * (n // LANE), axis=1)
    return jnp.broadcast_to(x[:, 0:1], (x.shape[0], n))


def _attention_kernel(q_ref, k_ref, wuv_ref, o_ref, m_scr, l_scr, acc_scr, *, tq, kb, causal, lk, n_split):
    rows = MLA_HEADS * tq
    part_rows = rows // n_split
    q = q_ref[...].reshape(rows, QK_PAD)
    m_scr[...] = jnp.full((rows, LANE), NEG_INF, F32)
    l_scr[...] = jnp.zeros((rows, LANE), F32)
    acc_scr[...] = jnp.zeros((rows, KV_LORA), F32)

    def block(start, size, mask):
        k = k_ref[pl.ds(start, size), :]
        v = k[:, 0:KV_LORA]
        scores = [lax.dot_general(q[part * part_rows:(part + 1) * part_rows], k, (((1,), (1,)), ((), ())),
                                  preferred_element_type=F32) for part in range(n_split)]
        for part in range(n_split):
            r = slice(part * part_rows, (part + 1) * part_rows)
            s = scores[part]
            if mask is not None:
                s = jnp.where(mask, s, NEG_INF)
            m_old = m_scr[r, :]
            m_new = jnp.maximum(m_old, jnp.max(s, axis=-1, keepdims=True))
            alpha = jnp.exp2((m_old - m_new) * EXP2_SCALE)
            p = jnp.exp2((s - _lanes(m_new, size)) * EXP2_SCALE)
            l_scr[r, :] = alpha * l_scr[r, :] + jnp.sum(p, axis=-1, keepdims=True)
            acc_scr[r, :] = acc_scr[r, :] * _lanes(alpha, KV_LORA) + jnp.dot(
                p.astype(BF16), v, preferred_element_type=F32)
            m_scr[r, :] = m_new

    if causal:
        q_start = pl.program_id(1) * tq
        n_wide = q_start // (2 * kb)

        def wide_body(j, carry):
            block(pl.multiple_of(j * 2 * kb, 2 * kb), 2 * kb, None)
            return carry

        lax.fori_loop(0, n_wide, wide_body, 0)
        n_main = q_start // kb

        @pl.when(n_main > 2 * n_wide)
        def _():
            block(pl.multiple_of(n_wide * 2 * kb, 2 * kb), kb, None)

        n_before = (q_start - n_main * kb) // tq
        for r in range(kb // tq):
            width = (r + 1) * tq

            @pl.when(n_before == r)
            def _(r=r, width=width):
                tok = jnp.bitwise_and(lax.broadcasted_iota(jnp.int32, (part_rows, width), 0), tq - 1)
                col = lax.broadcasted_iota(jnp.int32, (part_rows, width), 1)
                block(pl.multiple_of(n_main * kb, kb), width, col < r * tq + (tok // CHUNK + 1) * CHUNK)
    else:
        for j in range(lk // kb):
            block(j * kb, kb, None)

    o = (acc_scr[...] / _lanes(l_scr[...], KV_LORA)).astype(BF16)
    for h in range(MLA_HEADS):
        o_ref[:, h * V_HEAD:(h + 1) * V_HEAD] = jnp.dot(
            o[h * tq:(h + 1) * tq], wuv_ref[h], preferred_element_type=F32).astype(o_ref.dtype)


def attention(q, k, wuv, *, batch, lq, lk, tq, kb, causal):
    nq = lq // tq
    rows = MLA_HEADS * tq
    kern = functools.partial(_attention_kernel, tq=tq, kb=kb, causal=causal, lk=lk,
                             n_split=4 if causal else 1)
    return pl.pallas_call(
        kern,
        grid=(batch, nq),
        in_specs=[pl.BlockSpec((MLA_HEADS, tq, QK_PAD), lambda b, i: (0, b * nq + i, 0)),
                  pl.BlockSpec((lk, QK_PAD), lambda b, i: (b, 0)),
                  pl.BlockSpec(wuv.shape, lambda b, i: (0, 0, 0))],
        out_specs=pl.BlockSpec((tq, MLA_WIDTH), lambda b, i: (b * nq + i, 0)),
        out_shape=jax.ShapeDtypeStruct((batch * lq, MLA_WIDTH), BF16),
        scratch_shapes=[pltpu.VMEM((rows, LANE), F32), pltpu.VMEM((rows, LANE), F32),
                        pltpu.VMEM((rows, KV_LORA), F32)],
        compiler_params=_params(("arbitrary", "arbitrary")),
        name="attention_causal" if causal else "attention_full",
    )(q, k, wuv)


def _sigmoid(x):
    return 1.0 / (1.0 + jnp.exp(-x))


def _hgrn_kernel(hq_ref, hf_ref, hi_ref, hg_ref, lb_ref, nw_ref, s0_ref, o_ref, sfin_ref,
                 st_scr, oacc_scr, g_scr, q_scr, k_scr, *, chunk, n_chunks, n_seq):
    t = pl.program_id(2)

    @pl.when(t == 0)
    def _():
        for sq in range(n_seq):
            st_scr[sq] = s0_ref[sq, 0].T

    lb = lb_ref[...]
    lb_floor = jnp.maximum(lb, LB_FLOOR)
    one_m_lb = 1.0 - lb
    tri = (lax.broadcasted_iota(jnp.int32, (chunk, chunk), 0)
           >= lax.broadcasted_iota(jnp.int32, (chunk, chunk), 1))
    tri_bf16 = jnp.where(tri, 1.0, 0.0).astype(BF16)
    ones = jnp.ones((HG_DK, HG_DV), BF16)
    row_id = lax.broadcasted_iota(jnp.int32, (chunk, HG_DK), 0)
    n_sub = chunk // HG_SUB

    hf = hf_ref[...].astype(F32)
    e = jnp.exp(-jnp.abs(hf))
    big = 1.0 / (1.0 + e)
    small = e * big
    pos = hf >= 0.0
    log_f = jnp.log(lb_floor + one_m_lb * jnp.where(pos, big, small))
    k_scr[...] = one_m_lb * jnp.where(pos, small, big)
    hq = hq_ref[...].astype(F32)
    q_scr[...] = hq * _sigmoid(hq)
    lf_a = log_f.astype(BF16)
    rest = log_f - lf_a.astype(F32)
    lf_b = rest.astype(BF16)
    lf_c = (rest - lf_b.astype(F32)).astype(BF16)
    g_min = None
    for c in range(n_seq * n_chunks):
        sl = slice(c * chunk, (c + 1) * chunk)
        parts = jnp.dot(tri_bf16, jnp.concatenate([lf_a[sl], lf_b[sl], lf_c[sl]], axis=1),
                        preferred_element_type=F32)
        g = parts[:, 0:HG_DK] + parts[:, HG_DK:2 * HG_DK] + parts[:, 2 * HG_DK:3 * HG_DK]
        g_scr[sl, :] = g
        g_end = g[chunk - 1:chunk]
        g_min = g_end if g_min is None else jnp.minimum(g_min, g_end)
    factorable = jnp.min(g_min) >= -HG_SAFE_EXPONENT

    def state_step(c, g, k, v16):
        st = st_scr.at[c // n_chunks]
        g_last = g[chunk - 1:chunk]
        kh = k * jnp.exp(g_last - g)
        st[...] = st[...] * jnp.exp(g_last) + lax.dot_general(
            v16, kh.astype(BF16), (((0,), (0,)), ((), ())), preferred_element_type=F32)

    def chunk_factored(c):
        sl = slice(c * chunk, (c + 1) * chunk)
        g, q, k = g_scr[sl, :], q_scr[sl, :], k_scr[sl, :]
        v16 = hi_ref[sl, :]
        qd = (q * jnp.exp(g)).astype(BF16)
        kd = (k * jnp.exp(-g)).astype(BF16)
        a = lax.dot_general(qd, kd, (((1,), (1,)), ((), ())), preferred_element_type=F32)
        a = jnp.where(tri, a, 0.0).astype(BF16)
        oacc_scr[sl, :] = (lax.dot_general(qd, st_scr[c // n_chunks].astype(BF16), (((1,), (1,)), ((), ())),
                                           preferred_element_type=F32)
                           + jnp.dot(a, v16, preferred_element_type=F32))
        state_step(c, g, k, v16)

    def chunk_exact(c):
        sl = slice(c * chunk, (c + 1) * chunk)
        oacc = oacc_scr.at[sl, :]
        g, q, k = g_scr[sl, :], q_scr[sl, :], k_scr[sl, :]
        v16 = hi_ref[sl, :]
        v = v16.astype(F32)

        oacc[...] = lax.dot_general((q * jnp.exp(g)).astype(BF16), st_scr[c // n_chunks].astype(BF16),
                                    (((1,), (1,)), ((), ())), preferred_element_type=F32)
        for i in range(1, n_sub):
            r = i * HG_SUB
            g_edge = g[r - 1:r]
            qt = q[r:r + HG_SUB] * jnp.exp(g[r:r + HG_SUB] - g_edge)
            kt = k[0:r] * jnp.exp(g_edge - g[0:r])
            a = lax.dot_general(qt.astype(BF16), kt.astype(BF16), (((1,), (1,)), ((), ())),
                                preferred_element_type=F32)
            oacc[r:r + HG_SUB, :] += jnp.dot(a.astype(BF16), v16[0:r], preferred_element_type=F32)
        for grp in range(chunk // 8):
            r0 = grp * 8
            r1 = (r0 // HG_SUB + 1) * HG_SUB
            n = r1 - r0
            parts = []
            for s in range(r0, r0 + 8):
                d = jnp.where(row_id[r0:r1] >= s, g[r0:r1] - g[s:s + 1], NEG_INF)
                parts.append(jnp.exp(d) * q[r0:r1] * k[s:s + 1])
            sums = jnp.dot(jnp.concatenate(parts, axis=0).astype(BF16), ones, preferred_element_type=F32)
            upd = sums[0:n] * v[r0:r0 + 1]
            for u in range(1, 8):
                upd = upd + sums[u * n:(u + 1) * n] * v[r0 + u:r0 + u + 1]
            oacc[r0:r1, :] += upd
        state_step(c, g, k, v16)

    @pl.when(factorable)
    def _():
        for c in range(n_seq * n_chunks):
            chunk_factored(c)

    @pl.when(jnp.logical_not(factorable))
    def _():
        for c in range(n_seq * n_chunks):
            chunk_exact(c)

    hg = hg_ref[...].astype(F32)
    o_ref[...] = (_rms(oacc_scr[...], nw_ref[...]) * (hg * _sigmoid(hg))).astype(o_ref.dtype)

    @pl.when(t == pl.num_programs(2) - 1)
    def _():
        for sq in range(n_seq):
            sfin_ref[sq, 0] = st_scr[sq].T


def hgrn(p, lb, nw, s0, *, batch, seq, chunk, tb, n_seq=1):
    nt = max(seq // tb, 1)
    assert tb == n_seq * seq or (n_seq == 1 and seq % tb == 0)
    kern = functools.partial(_hgrn_kernel, chunk=chunk, n_chunks=tb // (n_seq * chunk), n_seq=n_seq)

    def col(base):
        return lambda b, h, t: (b * nt + t, base // LANE + h)

    return pl.pallas_call(
        kern,
        grid=(batch // n_seq, HG_HEADS, nt),
        in_specs=[pl.BlockSpec((tb, LANE), col(COL_HQ)),
                  pl.BlockSpec((tb, LANE), col(COL_HF)),
                  pl.BlockSpec((tb, LANE), col(COL_HI)),
                  pl.BlockSpec((tb, LANE), col(COL_HG)),
                  pl.BlockSpec((1, HG_DK), lambda b, h, t: (0, h)),
                  pl.BlockSpec((1, HG_DV), lambda b, h, t: (0, 0)),
                  pl.BlockSpec((n_seq, 1, HG_DK, HG_DV), lambda b, h, t: (b, h, 0, 0))],
        out_specs=[pl.BlockSpec((tb, HG_DV), lambda b, h, t: (b * nt + t, h)),
                   pl.BlockSpec((n_seq, 1, HG_DK, HG_DV), lambda b, h, t: (b, h, 0, 0))],
        out_shape=[jax.ShapeDtypeStruct((batch * seq, HG_WIDTH), BF16),
                   jax.ShapeDtypeStruct((batch, HG_HEADS, HG_DK, HG_DV), F32)],
        scratch_shapes=[pltpu.VMEM((n_seq, HG_DV, HG_DK), F32), pltpu.VMEM((tb, HG_DV), F32),
                        pltpu.VMEM((tb, HG_DK), F32), pltpu.VMEM((tb, HG_DK), F32),
                        pltpu.VMEM((tb, HG_DK), F32)],
        compiler_params=_params(("arbitrary", "arbitrary", "arbitrary")),
        name="hgrn",
    )(p, p, p, p, lb, nw, s0)


def _out_proj_kernel(x_ref, a_ref, b_ref, w_ref, o_ref):
    o_ref[...] = (x_ref[...]
                  + jnp.dot(a_ref[...], w_ref[0:MLA_WIDTH, :], preferred_element_type=F32)
                  + jnp.dot(b_ref[...], w_ref[MLA_WIDTH:MLA_WIDTH + HG_WIDTH, :], preferred_element_type=F32))


def out_proj(x, a, b, w, *, tm):
    n, d = x.shape
    return pl.pallas_call(
        _out_proj_kernel,
        grid=(n // tm,),
        in_specs=[pl.BlockSpec((tm, d), lambda i: (i, 0)),
                  pl.BlockSpec((tm, MLA_WIDTH), lambda i: (i, 0)),
                  pl.BlockSpec((tm, HG_WIDTH), lambda i: (i, 0)),
                  pl.BlockSpec(w.shape, lambda i: (0, 0))],
        out_specs=pl.BlockSpec((tm, d), lambda i: (i, 0)),
        out_shape=jax.ShapeDtypeStruct((n, d), F32),
        compiler_params=_params(("arbitrary",)),
        name="out_proj",
    )(x, a, b, w)


def _silu(x):
    return x * _sigmoid(x)


def _swiglu_chunk(h, wg, wu, wd):
    a = _silu(jnp.dot(h, wg, preferred_element_type=F32)) * jnp.dot(h, wu, preferred_element_type=F32)
    return jnp.dot(a.astype(BF16), wd, preferred_element_type=F32)


def _ffn_kernel(x_ref, nw_ref, wg_hbm, wu_hbm, wd_hbm, o_ref, h_scr, acc_scr, wg_buf, wu_buf, wd_buf, sem, *, nf):
    i = pl.program_id(0)
    n_tiles = pl.num_programs(0)

    def chunk_copies(j, slot):
        return (pltpu.make_async_copy(wg_hbm.at[j], wg_buf.at[slot], sem.at[0, slot]),
                pltpu.make_async_copy(wu_hbm.at[j], wu_buf.at[slot], sem.at[1, slot]),
                pltpu.make_async_copy(wd_hbm.at[j], wd_buf.at[slot], sem.at[2, slot]))

    @pl.when(i == 0)
    def _():
        for cp in chunk_copies(0, 0):
            cp.start()

    h_scr[...] = _rms(x_ref[...], nw_ref[...]).astype(BF16)
    acc_scr[...] = jnp.zeros_like(acc_scr)

    def body(j, carry):
        slot = (i * nf + j) % 2

        @pl.when(j + 1 < nf)
        def _():
            for cp in chunk_copies(j + 1, 1 - slot):
                cp.start()

        @pl.when((j + 1 == nf) & (i + 1 < n_tiles))
        def _():
            for cp in chunk_copies(0, 1 - slot):
                cp.start()

        for cp in chunk_copies(j, slot):
            cp.wait()
        acc_scr[...] += _swiglu_chunk(h_scr[...], wg_buf[slot], wu_buf[slot], wd_buf[slot])
        return carry

    lax.fori_loop(0, nf, body, 0)
    o_ref[...] = x_ref[...] + acc_scr[...]


def ffn(x, nw, wg, wu, wd, *, tm):
    n, d = x.shape
    nf, _, tf = wg.shape
    any_spec = pl.BlockSpec(memory_space=pl.ANY)
    return pl.pallas_call(
        functools.partial(_ffn_kernel, nf=nf),
        grid=(n // tm,),
        in_specs=[pl.BlockSpec((tm, d), lambda i: (i, 0)),
                  pl.BlockSpec((1, d), lambda i: (0, 0)), any_spec, any_spec, any_spec],
        out_specs=pl.BlockSpec((tm, d), lambda i: (i, 0)),
        out_shape=jax.ShapeDtypeStruct((n, d), F32),
        scratch_shapes=[pltpu.VMEM((tm, d), BF16), pltpu.VMEM((tm, d), F32),
                        pltpu.VMEM((2, d, tf), BF16), pltpu.VMEM((2, d, tf), BF16),
                        pltpu.VMEM((2, tf, d), BF16), pltpu.SemaphoreType.DMA((3, 2))],
        compiler_params=_params(("arbitrary",)),
        name="ffn",
    )(x, nw, wg, wu, wd)


def _moe_route_kernel(x_ref, nw_ref, wr_ref, hp_ref, e1_ref, e2_ref, w1_ref, w2_ref, r1_ref, r2_ref,
                      cnt_ref, cnt_scr):
    @pl.when(pl.program_id(0) == 0)
    def _():
        cnt_scr[...] = jnp.zeros_like(cnt_scr)

    tm = x_ref.shape[0]
    half = x_ref.shape[1] // 2
    h = _rms(x_ref[...], nw_ref[...])
    hb = h.astype(BF16)
    h_lo = (h - hb.astype(F32)).astype(BF16)
    wr = wr_ref[...]
    w_hi = wr.astype(BF16)
    w_lo = (wr - w_hi.astype(F32)).astype(BF16)
    logits = jnp.dot(hb, w_hi, preferred_element_type=F32) + (
        jnp.dot(hb, w_lo, preferred_element_type=F32) + jnp.dot(h_lo, w_hi, preferred_element_type=F32))
    ids = lax.broadcasted_iota(jnp.int32, logits.shape, 1).astype(F32)
    m1 = jnp.max(logits, axis=-1, keepdims=True)
    i1 = jnp.min(jnp.where(logits == m1, ids, float(N_EXPERTS)), axis=-1, keepdims=True)
    rest = jnp.where(ids == i1, -jnp.inf, logits)
    m2 = jnp.max(rest, axis=-1, keepdims=True)
    i2 = jnp.min(jnp.where(rest == m2, ids, float(N_EXPERTS)), axis=-1, keepdims=True)
    t = jnp.exp(m2 - m1)
    e1_ref[...] = i1.astype(jnp.int32)
    e2_ref[...] = i2.astype(jnp.int32)
    w1_ref[...] = 1.0 / (1.0 + t)
    w2_ref[...] = t / (1.0 + t)
    oh1 = (ids == i1).astype(F32)
    oh2 = (ids == i2).astype(F32)
    both = oh1 + oh2
    before = (lax.broadcasted_iota(jnp.int32, (tm, tm), 0)
              > lax.broadcasted_iota(jnp.int32, (tm, tm), 1)).astype(BF16)
    prefix = jnp.dot(before, both.astype(BF16), preferred_element_type=F32) + cnt_scr[...]
    r1_ref[...] = jnp.sum(prefix * oh1, axis=-1, keepdims=True).astype(jnp.int32)
    r2_ref[...] = jnp.sum(prefix * oh2, axis=-1, keepdims=True).astype(jnp.int32)
    cnt_scr[...] += jnp.sum(both, axis=0, keepdims=True)
    cnt_ref[...] = cnt_scr[...].astype(jnp.int32)
    hb32 = hb.astype(F32)
    hi = pltpu.bitcast(hb32[:, 0:half], jnp.uint32)
    lo = pltpu.bitcast(hb32[:, half:2 * half], jnp.uint32)
    hp_ref[...] = hi | (lo >> 16)


def moe_route(x, nw, wr, *, tm):
    n, d = x.shape
    col = lambda dt: jax.ShapeDtypeStruct((n, 1), dt)
    col_spec = pl.BlockSpec((tm, 1), lambda i: (i, 0))
    return pl.pallas_call(
        _moe_route_kernel,
        grid=(n // tm,),
        in_specs=[pl.BlockSpec((tm, d), lambda i: (i, 0)),
                  pl.BlockSpec((1, d), lambda i: (0, 0)),
                  pl.BlockSpec(wr.shape, lambda i: (0, 0))],
        out_specs=[pl.BlockSpec((tm, d // 2), lambda i: (i, 0))] + [col_spec] * 6
                  + [pl.BlockSpec((1, N_EXPERTS), lambda i: (0, 0))],
        out_shape=[jax.ShapeDtypeStruct((n, d // 2), jnp.uint32), col(jnp.int32), col(jnp.int32),
                   col(F32), col(F32), col(jnp.int32), col(jnp.int32),
                   jax.ShapeDtypeStruct((1, N_EXPERTS), jnp.int32)],
        scratch_shapes=[pltpu.VMEM((1, N_EXPERTS), F32)],
        compiler_params=_params(("arbitrary",)),
        name="moe_route",
    )(x, nw, wr)


def _row_copy(src, src_row, dst, dst_row, sem):
    return pltpu.make_async_copy(src.at[pl.ds(src_row, 1), :], dst.at[pl.ds(dst_row, 1), :], sem)


def _moe_scatter_kernel(d1_ref, d2_ref, hp_ref, xs_in_hbm, xs_hbm, sem, *, tm):
    del xs_in_hbm

    def issue(r, carry):
        _row_copy(hp_ref, r, xs_hbm, d1_ref[0, 0, r], sem).start()
        _row_copy(hp_ref, r, xs_hbm, d2_ref[0, 0, r], sem).start()
        return carry

    lax.fori_loop(0, tm, issue, 0, unroll=8)

    def drain(r, carry):
        _row_copy(hp_ref, r, xs_hbm, d1_ref[0, 0, r], sem).wait()
        _row_copy(hp_ref, r, xs_hbm, d2_ref[0, 0, r], sem).wait()
        return carry

    lax.fori_loop(0, tm, drain, 0, unroll=8)


def moe_scatter(hp, d1, d2, xs, *, tm):
    n = hp.shape[0]
    idx_spec = pl.BlockSpec((1, 1, tm), lambda i: (i, 0, 0), memory_space=pltpu.SMEM)
    any_spec = pl.BlockSpec(memory_space=pl.ANY)
    return pl.pallas_call(
        functools.partial(_moe_scatter_kernel, tm=tm),
        grid=(n // tm,),
        in_specs=[idx_spec, idx_spec, pl.BlockSpec((tm, hp.shape[1]), lambda i: (i, 0)), any_spec],
        out_specs=any_spec,
        out_shape=jax.ShapeDtypeStruct(xs.shape, xs.dtype),
        scratch_shapes=[pltpu.SemaphoreType.DMA(())],
        input_output_aliases={3: 0},
        compiler_params=_params(("arbitrary",)),
        name="moe_scatter",
    )(d1.reshape(n // tm, 1, tm), d2.reshape(n // tm, 1, tm), hp, xs)


def _moe_expert_kernel(te_ref, na_ref, xs_ref, wg_hbm, wu_hbm, wd_hbm, ys_ref, h_scr, wg_buf, wu_buf, wd_buf, sem,
                       *, nf):
    i = pl.program_id(0)
    n_act = na_ref[0]
    active = i < n_act
    half = xs_ref.shape[1]

    def chunk_copies(tile, j, slot):
        e = te_ref[tile]
        return (pltpu.make_async_copy(wg_hbm.at[e, j], wg_buf.at[slot], sem.at[0, slot]),
                pltpu.make_async_copy(wu_hbm.at[e, j], wu_buf.at[slot], sem.at[1, slot]),
                pltpu.make_async_copy(wd_hbm.at[e, j], wd_buf.at[slot], sem.at[2, slot]))

    @pl.when(active)
    def _():
        @pl.when(i == 0)
        def _():
            for cp in chunk_copies(0, 0, 0):
                cp.start()

        packed = xs_ref[...]
        h_scr[:, 0:half] = pltpu.bitcast(packed & jnp.uint32(0xFFFF0000), F32).astype(BF16)
        h_scr[:, half:2 * half] = pltpu.bitcast(packed << 16, F32).astype(BF16)
        ys_ref[...] = jnp.zeros_like(ys_ref)

        def body(j, carry):
            slot = (i * nf + j) % 2

            @pl.when(j + 1 < nf)
            def _():
                for cp in chunk_copies(i, j + 1, 1 - slot):
                    cp.start()

            @pl.when((j + 1 == nf) & (i + 1 < n_act))
            def _():
                for cp in chunk_copies(i + 1, 0, 1 - slot):
                    cp.start()

            for cp in chunk_copies(i, j, slot):
                cp.wait()
            ys_ref[...] += _swiglu_chunk(h_scr[...], wg_buf[slot], wu_buf[slot], wd_buf[slot])
            return carry

        lax.fori_loop(0, nf, body, 0)

    @pl.when(jnp.logical_not(active))
    def _():
        ys_ref[...] = jnp.zeros_like(ys_ref)


def moe_experts(xs, tile_expert, n_active, wg, wu, wd, *, tm):
    rows, half = xs.shape
    d = 2 * half
    _, nf, _, tf = wg.shape
    any_spec = pl.BlockSpec(memory_space=pl.ANY)
    return pl.pallas_call(
        functools.partial(_moe_expert_kernel, nf=nf),
        grid_spec=pltpu.PrefetchScalarGridSpec(
            num_scalar_prefetch=2,
            grid=(rows // tm,),
            in_specs=[pl.BlockSpec((tm, half), lambda i, te, na: (i, 0)), any_spec, any_spec, any_spec],
            out_specs=pl.BlockSpec((tm, d), lambda i, te, na: (i, 0)),
            scratch_shapes=[pltpu.VMEM((tm, d), BF16),
                            pltpu.VMEM((2, d, tf), BF16), pltpu.VMEM((2, d, tf), BF16),
                            pltpu.VMEM((2, tf, d), BF16), pltpu.SemaphoreType.DMA((3, 2))]),
        out_shape=jax.ShapeDtypeStruct((rows, d), F32),
        compiler_params=_params(("arbitrary",)),
        name="moe_experts",
    )(tile_expert, n_active, xs, wg, wu, wd)


def _moe_combine_kernel(d1_ref, d2_ref, x_ref, w1_ref, w2_ref, fw_ref, ys_hbm, o_ref, a_scr, b_scr, sem, *, tm):
    def issue(r, carry):
        _row_copy(ys_hbm, d1_ref[0, 0, r], a_scr, r, sem).start()
        _row_copy(ys_hbm, d2_ref[0, 0, r], b_scr, r, sem).start()
        return carry

    lax.fori_loop(0, tm, issue, 0, unroll=8)

    def drain(r, carry):
        _row_copy(ys_hbm, d1_ref[0, 0, r], a_scr, r, sem).wait()
        _row_copy(ys_hbm, d2_ref[0, 0, r], b_scr, r, sem).wait()
        return carry

    lax.fori_loop(0, tm, drain, 0, unroll=8)
    y = x_ref[...] + (w1_ref[...] * a_scr[...] + w2_ref[...] * b_scr[...])
    o_ref[...] = _rms(y, fw_ref[...])


def moe_combine(x, ys, d1, d2, w1, w2, fw, *, tm):
    n, d = x.shape
    idx_spec = pl.BlockSpec((1, 1, tm), lambda i: (i, 0, 0), memory_space=pltpu.SMEM)
    col_spec = pl.BlockSpec((tm, 1), lambda i: (i, 0))
    return pl.pallas_call(
        functools.partial(_moe_combine_kernel, tm=tm),
        grid=(n // tm,),
        in_specs=[idx_spec, idx_spec, pl.BlockSpec((tm, d), lambda i: (i, 0)), col_spec, col_spec,
                  pl.BlockSpec((1, d), lambda i: (0, 0)), pl.BlockSpec(memory_space=pl.ANY)],
        out_specs=pl.BlockSpec((tm, d), lambda i: (i, 0)),
        out_shape=jax.ShapeDtypeStruct((n, d), F32),
        scratch_shapes=[pltpu.VMEM((tm, d), F32), pltpu.VMEM((tm, d), F32), pltpu.SemaphoreType.DMA(())],
        compiler_params=_params(("arbitrary",)),
        name="moe_combine",
    )(d1.reshape(n // tm, 1, tm), d2.reshape(n // tm, 1, tm), x, w1, w2, fw, ys)


def moe_final(streams, nw, wr, wg, wu, wd, fw):
    n_exp = wg.shape[0]
    d = streams[0].shape[1]
    tms = [min(TOKEN_TILE, x.shape[0]) for x in streams]
    tile = TOKEN_TILE
    routed = [moe_route(x, nw, wr, tm=tm) for x, tm in zip(streams, tms)]
    counts = [r[7][0] for r in routed]
    total = sum(counts)
    padded = (total + tile - 1) // tile * tile
    ends = jnp.cumsum(padded)
    starts = ends - padded
    n_rows = sum(2 * x.shape[0] for x in streams) + n_exp * tile
    n_tiles = n_rows // tile
    tile_expert = jnp.minimum(
        jnp.sum(jnp.arange(n_tiles, dtype=jnp.int32)[:, None] * tile >= ends[None, :], axis=1), n_exp - 1
    ).astype(jnp.int32)
    n_active = (ends[n_exp - 1:] // tile).astype(jnp.int32)

    def slot_rows(e, rank, offset):
        table = starts + offset
        return (jnp.sum(jnp.where(e == jnp.arange(n_exp)[None, :], table[None, :], 0), axis=1, keepdims=True)
                + rank).astype(jnp.int32)

    xs = jnp.zeros((n_rows, d // 2), jnp.uint32)
    dests = []
    offset = jnp.zeros((n_exp,), jnp.int32)
    for (hp, e1, e2, w1, w2, r1, r2, cnt), tm in zip(routed, tms):
        d1, d2 = slot_rows(e1, r1, offset), slot_rows(e2, r2, offset)
        xs = moe_scatter(hp, d1, d2, xs, tm=tm)
        dests.append((d1, d2))
        offset = offset + cnt[0]
    ys = moe_experts(xs, tile_expert, n_active, wg, wu, wd, tm=tile)
    return [moe_combine(x, ys, d1, d2, r[3], r[4], fw, tm=tm)
            for x, r, (d1, d2), tm in zip(streams, routed, dests, tms)]


def _swap_halves(t):
    half = t.shape[-1] // 2
    return jnp.concatenate([t[..., half:], t[..., :half]], axis=-1)


def _prep_w_in(w):
    d = w.shape[0]
    k_pe = w[:, Q_LORA + KV_LORA:Q_LORA + KV_LORA + QK_ROPE]
    pad = jnp.zeros((d, LANE - QK_ROPE), w.dtype)
    return jnp.concatenate([w[:, :Q_LORA + KV_LORA], w[:, Q_LORA + KV_LORA + QK_ROPE:],
                            k_pe, pad, _swap_halves(k_pe), pad], axis=1).astype(BF16)


def _prep_w_uq(w):
    w = w.reshape(Q_LORA, MLA_HEADS, QK_NOPE + QK_ROPE)
    nope = w[..., :QK_NOPE]
    rope = w[..., QK_NOPE:]
    pad = jnp.zeros((Q_LORA, MLA_HEADS, LANE - QK_ROPE), w.dtype)
    parts = [nope, jnp.concatenate([rope, pad], -1), jnp.concatenate([_swap_halves(rope), pad], -1)]
    return jnp.concatenate([t.reshape(Q_LORA, MLA_HEADS * LANE) for t in parts], axis=1).astype(BF16)


def _rope_tables(pos):
    half = QK_ROPE // 2
    inv_freq = jnp.exp(jnp.arange(half, dtype=F32) * (-math.log(ROPE_THETA) / half))
    ang = pos.astype(F32)[:, None] * inv_freq[None, :]
    cos, sin = jnp.cos(ang), jnp.sin(ang)
    pad = jnp.zeros((pos.shape[0], LANE - QK_ROPE), F32)
    return jnp.concatenate([cos, cos, pad], -1), jnp.concatenate([-sin, sin, pad], -1)


def _row(v):
    return v.reshape(1, -1)


def kernel(x_prompt, x_sample, cache_ckv, cache_kpe, state_hgrn, attn_norm_w, w_in, q_norm_w, kv_norm_w,
           w_uq, w_uk, w_uv, hg_lower_bounds, hg_norm_w, w_out, ffn_norm_w, w_gate, w_up, w_down,
           w_router, we_gate, we_up, we_down, final_norm_w):
    depth = w_in.shape[0]
    b_p, l_p, d = x_prompt.shape
    b_s, l_s, _ = x_sample.shape
    past = cache_ckv.shape[2]
    n_p, n_s = b_p * l_p, b_s * l_s
    assert depth == 2 and d == D_MODEL

    probs = jax.nn.softmax(hg_lower_bounds.astype(F32), axis=0)
    lower_bounds = jnp.cumsum(probs, axis=0) - probs[0:1]

    cos_p, sin_p = _rope_tables(jnp.arange(l_p))
    cos_s, sin_s = _rope_tables(past + jnp.arange(l_s))
    cos_s, sin_s = jnp.tile(cos_s, (b_s, 1)), jnp.tile(sin_s, (b_s, 1))
    zero_state = jnp.zeros((b_p, HG_HEADS, HG_DK, HG_DV), F32)

    xp = x_prompt.reshape(n_p, d)
    xs = x_sample.reshape(n_s, d)
    outs = {k: [] for k in ("ckv_p", "kpe_p", "st_p", "ckv_s", "kpe_s", "st_s")}

    for l in range(depth):
        w_in_l = _prep_w_in(w_in[l])
        w_uq_l = _prep_w_uq(w_uq[l])
        w_uk_l = jnp.transpose(w_uk[l], (1, 2, 0)).astype(BF16)
        w_uv_l = jnp.transpose(w_uv[l], (1, 0, 2)).astype(BF16)
        w_out_l = w_out[l].astype(BF16)
        lb_l = _row(lower_bounds[l])
        mixers = []
        for (x, n, batch, seq, cos, sin) in ((xp, n_p, b_p, l_p, cos_p, sin_p),
                                             (xs, n_s, b_s, l_s, cos_s, sin_s)):
            tm = min(TOKEN_TILE, n)
            p = norm_matmul(x, _row(attn_norm_w[l]), w_in_l, tm=tm, tn=1024)
            q, kb, ckv, kpe = mla_prep(p, _row(q_norm_w[l]), _row(kv_norm_w[l]), w_uq_l, w_uk_l,
                                       cos, sin, tm=tm)
            if x is xp:
                o_mla = attention(q, kb, w_uv_l, batch=batch, lq=seq, lk=seq, tq=128, kb=512, causal=True)
                o_hg, st = hgrn(p, lb_l, _row(hg_norm_w[l]), zero_state, batch=batch, seq=seq,
                                chunk=CHUNK, tb=min(TOKEN_TILE, seq))
            else:
                past_k = jnp.concatenate(
                    [cache_ckv[l].astype(BF16), cache_kpe[l].astype(BF16),
                     jnp.zeros((batch, past, LANE - QK_ROPE), BF16)], axis=-1)
                keys = jnp.concatenate([past_k, kb.reshape(batch, seq, QK_PAD)], axis=1)
                lk = past + seq
                o_mla = attention(q, keys.reshape(batch * lk, QK_PAD), w_uv_l, batch=batch, lq=seq, lk=lk,
                                  tq=seq, kb=lk, causal=False)
                n_seq = math.gcd(batch, 8)
                o_hg, st = hgrn(p, lb_l, _row(hg_norm_w[l]), state_hgrn[l], batch=batch, seq=seq,
                                chunk=seq, tb=n_seq * seq, n_seq=n_seq)
            x1 = out_proj(x, o_mla, o_hg, w_out_l, tm=tm)
            mixers.append((x1, ckv.reshape(batch, seq, KV_LORA), kpe.reshape(batch, seq, QK_ROPE), st))
        (xp, ckv_p, kpe_p, st_p), (xs, ckv_s, kpe_s, st_s) = mixers
        for name, val in (("ckv_p", ckv_p), ("kpe_p", kpe_p), ("st_p", st_p),
                          ("ckv_s", ckv_s), ("kpe_s", kpe_s), ("st_s", st_s)):
            outs[name].append(val)

        i = l // 2
        if l % 2 == 0:
            d_ff = w_gate[i].shape[1]
            nf = d_ff // FFN_CHUNK

            def col_chunks(w):
                return jnp.transpose(w.astype(BF16).reshape(d, nf, FFN_CHUNK), (1, 0, 2))

            wg, wu = col_chunks(w_gate[i]), col_chunks(w_up[i])
            wd = w_down[i].astype(BF16).reshape(nf, FFN_CHUNK, d)
            xp = ffn(xp, _row(ffn_norm_w[l]), wg, wu, wd, tm=min(TOKEN_TILE, n_p))
            xs = ffn(xs, _row(ffn_norm_w[l]), wg, wu, wd, tm=min(TOKEN_TILE, n_s))
        else:
            n_exp, _, moe_ff = we_gate[i].shape
            nf = moe_ff // EXPERT_CHUNK

            def col_chunks(w):
                return jnp.transpose(w.astype(BF16).reshape(n_exp, d, nf, EXPERT_CHUNK), (0, 2, 1, 3))

            wg, wu = col_chunks(we_gate[i]), col_chunks(we_up[i])
            wd = we_down[i].astype(BF16).reshape(n_exp, nf, EXPERT_CHUNK, d)
            xp, xs = moe_final([xp, xs], _row(ffn_norm_w[l]), w_router[i], wg, wu, wd, _row(final_norm_w))

    return (xp.reshape(b_p, l_p, d), xs.reshape(b_s, l_s, d),
            jnp.stack(outs["ckv_p"]), jnp.stack(outs["kpe_p"]), jnp.stack(outs["st_p"]),
            jnp.stack(outs["ckv_s"]), jnp.stack(outs["kpe_s"]), jnp.stack(outs["st_s"]))
```

```python
import functools
import math

import jax
import jax.numpy as jnp
from jax import lax
from jax.experimental import pallas as pl
from jax.experimental.pallas import tpu as pltpu

F32 = jnp.float32
BF16 = jnp.bfloat16

D_MODEL = 2048
CHUNK = 64
RMS_EPS = 1e-6
NEG_INF = -1e30
LB_FLOOR = 1e-30

MLA_HEADS = 8
V_HEAD = 128
QK_NOPE = 128
QK_ROPE = 64
Q_LORA = 512
KV_LORA = 256
ROPE_THETA = 10000.0
ATTN_SCALE = (QK_NOPE + QK_ROPE) ** -0.5
EXP2_SCALE = ATTN_SCALE * math.log2(math.e)
MLA_WIDTH = MLA_HEADS * V_HEAD

HG_HEADS = 8
HG_DK = 128
HG_DV = 128
HG_KEY = HG_HEADS * HG_DK
HG_WIDTH = HG_HEADS * HG_DV
HG_SUB = 16
HG_SAFE_EXPONENT = 80.0
HG_BLOCK = 2048

N_EXPERTS = 8

LANE = 128
QK_PAD = KV_LORA + LANE

COL_CQ = 0
COL_CKV = Q_LORA
COL_HQ = Q_LORA + KV_LORA
COL_HF = COL_HQ + HG_KEY
COL_HI = COL_HF + HG_KEY
COL_HG = COL_HI + HG_WIDTH
COL_KPE = COL_HG + HG_WIDTH
IN_EXT = COL_KPE + 2 * LANE

VMEM_LIMIT = 56 * 1024 * 1024
TOKEN_TILE = 512
EXPERT_CHUNK = 256
FFN_CHUNK = 512


def _params(semantics):
    return pltpu.CompilerParams(dimension_semantics=semantics, vmem_limit_bytes=VMEM_LIMIT)


def _rms(x, w):
    return x * lax.rsqrt(jnp.mean(x * x, axis=-1, keepdims=True) + RMS_EPS) * w


def _norm_matmul_kernel(x_ref, nw_ref, w_hbm, o_ref, w_vmem, sem, *, tn):
    @pl.when(pl.program_id(0) == 0)
    def _():
        cp = pltpu.make_async_copy(w_hbm, w_vmem, sem)
        cp.start()
        cp.wait()

    h = _rms(x_ref[...], nw_ref[...]).astype(BF16)
    for c in range(o_ref.shape[1] // tn):
        o_ref[:, c * tn:(c + 1) * tn] = jnp.dot(
            h, w_vmem[:, c * tn:(c + 1) * tn], preferred_element_type=F32).astype(o_ref.dtype)


def norm_matmul(x, nw, w, *, tm, tn):
    n, d = x.shape
    cols = w.shape[1]
    return pl.pallas_call(
        functools.partial(_norm_matmul_kernel, tn=tn),
        grid=(n // tm,),
        in_specs=[pl.BlockSpec((tm, d), lambda i: (i, 0)),
                  pl.BlockSpec((1, d), lambda i: (0, 0)),
                  pl.BlockSpec(memory_space=pl.ANY)],
        out_specs=pl.BlockSpec((tm, cols), lambda i: (i, 0)),
        out_shape=jax.ShapeDtypeStruct((n, cols), BF16),
        scratch_shapes=[pltpu.VMEM((d, cols), BF16), pltpu.SemaphoreType.DMA(())],
        compiler_params=_params(("arbitrary",)),
        name="in_proj",
    )(x, nw, w)


def _mla_prep_kernel(cq_ref, ckv_ref, kp_ref, qnw_ref, kvnw_ref, wuq_ref, wuk_ref, cos_ref, sin_ref,
                     q_ref, kb_ref, ckv_out_ref, kpe_out_ref):
    cos = cos_ref[...]
    sin = sin_ref[...]
    cqn = _rms(cq_ref[...].astype(F32), qnw_ref[...]).astype(BF16)
    q = jnp.dot(cqn, wuq_ref[...], preferred_element_type=F32)
    hw = MLA_HEADS * LANE
    for h in range(MLA_HEADS):
        sl = slice(h * LANE, (h + 1) * LANE)
        q_lat = jnp.dot(q[:, sl].astype(BF16), wuk_ref[h], preferred_element_type=F32)
        rope = q[:, hw + h * LANE:hw + (h + 1) * LANE] * cos + q[:, 2 * hw + h * LANE:2 * hw + (h + 1) * LANE] * sin
        q_ref[h, :, 0:KV_LORA] = q_lat.astype(BF16)
        q_ref[h, :, KV_LORA:QK_PAD] = rope.astype(BF16)
    ckv = _rms(ckv_ref[...].astype(F32), kvnw_ref[...])
    ckv_out_ref[...] = ckv
    kb_ref[:, 0:KV_LORA] = ckv.astype(BF16)
    kp = kp_ref[...].astype(F32)
    kpe = kp[:, 0:LANE] * cos + kp[:, LANE:2 * LANE] * sin
    kpe_out_ref[...] = kpe[:, 0:QK_ROPE]
    kb_ref[:, KV_LORA:QK_PAD] = kpe.astype(BF16)


def mla_prep(p, qnw, kvnw, wuq, wuk, cos, sin, *, tm):
    n = p.shape[0]
    n_pos = cos.shape[0] // tm
    return pl.pallas_call(
        _mla_prep_kernel,
        grid=(n // tm,),
        in_specs=[pl.BlockSpec((tm, Q_LORA), lambda i: (i, COL_CQ // Q_LORA)),
                  pl.BlockSpec((tm, KV_LORA), lambda i: (i, COL_CKV // KV_LORA)),
                  pl.BlockSpec((tm, 2 * LANE), lambda i: (i, COL_KPE // (2 * LANE))),
                  pl.BlockSpec((1, Q_LORA), lambda i: (0, 0)),
                  pl.BlockSpec((1, KV_LORA), lambda i: (0, 0)),
                  pl.BlockSpec(wuq.shape, lambda i: (0, 0)),
                  pl.BlockSpec(wuk.shape, lambda i: (0, 0, 0)),
                  pl.BlockSpec((tm, LANE), lambda i: (i % n_pos, 0)),
                  pl.BlockSpec((tm, LANE), lambda i: (i % n_pos, 0))],
        out_specs=[pl.BlockSpec((MLA_HEADS, tm, QK_PAD), lambda i: (0, i, 0)),
                   pl.BlockSpec((tm, QK_PAD), lambda i: (i, 0)),
                   pl.BlockSpec((tm, KV_LORA), lambda i: (i, 0)),
                   pl.BlockSpec((tm, QK_ROPE), lambda i: (i, 0))],
        out_shape=[jax.ShapeDtypeStruct((MLA_HEADS, n, QK_PAD), BF16),
                   jax.ShapeDtypeStruct((n, QK_PAD), BF16),
                   jax.ShapeDtypeStruct((n, KV_LORA), F32),
                   jax.ShapeDtypeStruct((n, QK_ROPE), F32)],
        compiler_params=_params(("arbitrary",)),
        name="mla_prep",
    )(p, p, p, qnw, kvnw, wuq, wuk, cos, sin)


def _lanes(x, n):
    if n == LANE:
        return x
    if n % LANE == 0:
        return jnp.concatenate([x] * (n // LANE), axis=1)
    return jnp.broadcast_to(x[:, 0:1], (x.shape[0], n))


def _attention_kernel(q_ref, k_ref, wuv_ref, o_ref, m_scr, l_scr, acc_scr, *, tq, kb, causal, lk, n_split):
    rows = MLA_HEADS * tq
    part_rows = rows // n_split
    q = q_ref[...].reshape(rows, QK_PAD)
    m_scr[...] = jnp.full((rows, LANE), NEG_INF, F32)
    l_scr[...] = jnp.zeros((rows, LANE), F32)
    acc_scr[...] = jnp.zeros((rows, KV_LORA), F32)

    def block(start, size, mask):
        k = k_ref[pl.ds(start, size), :]
        v = k[:, 0:KV_LORA]
        scores = [lax.dot_general(q[part * part_rows:(part + 1) * part_rows], k, (((1,), (1,)), ((), ())),
                                  preferred_element_type=F32) for part in range(n_split)]
        for part in range(n_split):
            r = slice(part * part_rows, (part + 1) * part_rows)
            s = scores[part]
            if mask is not None:
                s = jnp.where(mask, s, NEG_INF)
            m_old = m_scr[r, :]
            m_new = jnp.maximum(m_old, jnp.max(s, axis=-1, keepdims=True))
            alpha = jnp.exp2((m_old - m_new) * EXP2_SCALE)
            p = jnp.exp2((s - _lanes(m_new, size)) * EXP2_SCALE)
            l_scr[r, :] = alpha * l_scr[r, :] + jnp.sum(p, axis=-1, keepdims=True)
            acc_scr[r, :] = acc_scr[r, :] * _lanes(alpha, KV_LORA) + jnp.dot(
                p.astype(BF16), v, preferred_element_type=F32)
            m_scr[r, :] = m_new

    if causal:
        q_start = pl.program_id(1) * tq
        n_wide = q_start // (2 * kb)

        def wide_body(j, carry):
            block(pl.multiple_of(j * 2 * kb, 2 * kb), 2 * kb, None)
            return carry

        lax.fori_loop(0, n_wide, wide_body, 0)
        n_main = q_start // kb

        @pl.when(n_main > 2 * n_wide)
        def _():
            block(pl.multiple_of(n_wide * 2 * kb, 2 * kb), kb, None)

        n_before = (q_start - n_main * kb) // tq
        for r in range(kb // tq):
            width = (r + 1) * tq

            @pl.when(n_before == r)
            def _(r=r, width=width):
                tok = jnp.bitwise_and(lax.broadcasted_iota(jnp.int32, (part_rows, width), 0), tq - 1)
                col = lax.broadcasted_iota(jnp.int32, (part_rows, width), 1)
                block(pl.multiple_of(n_main * kb, kb), width, col < r * tq + (tok // CHUNK + 1) * CHUNK)
    else:
        for j in range(lk // kb):
            block(j * kb, kb, None)

    o = (acc_scr[...] / _lanes(l_scr[...], KV_LORA)).astype(BF16)
    for h in range(MLA_HEADS):
        o_ref[:, h * V_HEAD:(h + 1) * V_HEAD] = jnp.dot(
            o[h * tq:(h + 1) * tq], wuv_ref[h], preferred_element_type=F32).astype(o_ref.dtype)


def attention(q, k, wuv, *, batch, lq, lk, tq, kb, causal):
    nq = lq // tq
    rows = MLA_HEADS * tq
    kern = functools.partial(_attention_kernel, tq=tq, kb=kb, causal=causal, lk=lk,
                             n_split=4 if causal else 1)
    return pl.pallas_call(
        kern,
        grid=(batch, nq),
        in_specs=[pl.BlockSpec((MLA_HEADS, tq, QK_PAD), lambda b, i: (0, b * nq + i, 0)),
                  pl.BlockSpec((lk, QK_PAD), lambda b, i: (b, 0)),
                  pl.BlockSpec(wuv.shape, lambda b, i: (0, 0, 0))],
        out_specs=pl.BlockSpec((tq, MLA_WIDTH), lambda b, i: (b * nq + i, 0)),
        out_shape=jax.ShapeDtypeStruct((batch * lq, MLA_WIDTH), BF16),
        scratch_shapes=[pltpu.VMEM((rows, LANE), F32), pltpu.VMEM((rows, LANE), F32),
                        pltpu.VMEM((rows, KV_LORA), F32)],
        compiler_params=_params(("arbitrary", "arbitrary")),
        name="attention_causal" if causal else "attention_full",
    )(q, k, wuv)


def _sigmoid(x):
    return 1.0 / (1.0 + jnp.exp(-x))


def _hgrn_kernel(hq_ref, hf_ref, hi_ref, hg_ref, lb_ref, nw_ref, s0_ref, o_ref, sfin_ref,
                 st_scr, oacc_scr, g_scr, q_scr, k_scr, *, chunk, n_chunks, n_seq):
    t = pl.program_id(2)

    @pl.when(t == 0)
    def _():
        for sq in range(n_seq):
            st_scr[sq] = s0_ref[sq, 0].T

    lb = lb_ref[...]
    lb_floor = jnp.maximum(lb, LB_FLOOR)
    one_m_lb = 1.0 - lb
    tri = (lax.broadcasted_iota(jnp.int32, (chunk, chunk), 0)
           >= lax.broadcasted_iota(jnp.int32, (chunk, chunk), 1))
    tri_bf16 = jnp.where(tri, 1.0, 0.0).astype(BF16)
    ones = jnp.ones((HG_DK, HG_DV), BF16)
    row_id = lax.broadcasted_iota(jnp.int32, (chunk, HG_DK), 0)
    n_sub = chunk // HG_SUB

    hf = hf_ref[...].astype(F32)
    e = jnp.exp(-jnp.abs(hf))
    big = 1.0 / (1.0 + e)
    small = e * big
    pos = hf >= 0.0
    log_f = jnp.log(lb_floor + one_m_lb * jnp.where(pos, big, small))
    k_scr[...] = one_m_lb * jnp.where(pos, small, big)
    hq = hq_ref[...].astype(F32)
    q_scr[...] = hq * _sigmoid(hq)
    lf_a = log_f.astype(BF16)
    rest = log_f - lf_a.astype(F32)
    lf_b = rest.astype(BF16)
    lf_c = (rest - lf_b.astype(F32)).astype(BF16)
    g_min = None
    for c in range(n_seq * n_chunks):
        sl = slice(c * chunk, (c + 1) * chunk)
        parts = jnp.dot(tri_bf16, jnp.concatenate([lf_a[sl], lf_b[sl], lf_c[sl]], axis=1),
                        preferred_element_type=F32)
        g = parts[:, 0:HG_DK] + parts[:, HG_DK:2 * HG_DK] + parts[:, 2 * HG_DK:3 * HG_DK]
        g_scr[sl, :] = g
        g_end = g[chunk - 1:chunk]
        g_min = g_end if g_min is None else jnp.minimum(g_min, g_end)
    factorable = jnp.min(g_min) >= -HG_SAFE_EXPONENT

    def state_step(c, g, k, v16):
        st = st_scr.at[c // n_chunks]
        g_last = g[chunk - 1:chunk]
        kh = k * jnp.exp(g_last - g)
        st[...] = st[...] * jnp.exp(g_last) + lax.dot_general(
            v16, kh.astype(BF16), (((0,), (0,)), ((), ())), preferred_element_type=F32)

    def chunk_factored(c):
        sl = slice(c * chunk, (c + 1) * chunk)
        g, q, k = g_scr[sl, :], q_scr[sl, :], k_scr[sl, :]
        v16 = hi_ref[sl, :]
        qd = (q * jnp.exp(g)).astype(BF16)
        kd = (k * jnp.exp(-g)).astype(BF16)
        a = lax.dot_general(qd, kd, (((1,), (1,)), ((), ())), preferred_element_type=F32)
        a = jnp.where(tri, a, 0.0).astype(BF16)
        oacc_scr[sl, :] = (lax.dot_general(qd, st_scr[c // n_chunks].astype(BF16), (((1,), (1,)), ((), ())),
                                           preferred_element_type=F32)
                           + jnp.dot(a, v16, preferred_element_type=F32))
        state_step(c, g, k, v16)

    def chunk_exact(c):
        sl = slice(c * chunk, (c + 1) * chunk)
        oacc = oacc_scr.at[sl, :]
        g, q, k = g_scr[sl, :], q_scr[sl, :], k_scr[sl, :]
        v16 = hi_ref[sl, :]
        v = v16.astype(F32)

        oacc[...] = lax.dot_general((q * jnp.exp(g)).astype(BF16), st_scr[c // n_chunks].astype(BF16),
                                    (((1,), (1,)), ((), ())), preferred_element_type=F32)
        for i in range(1, n_sub):
            r = i * HG_SUB
            g_edge = g[r - 1:r]
            qt = q[r:r + HG_SUB] * jnp.exp(g[r:r + HG_SUB] - g_edge)
            kt = k[0:r] * jnp.exp(g_edge - g[0:r])
            a = lax.dot_general(qt.astype(BF16), kt.astype(BF16), (((1,), (1,)), ((), ())),
                                preferred_element_type=F32)
            oacc[r:r + HG_SUB, :] += jnp.dot(a.astype(BF16), v16[0:r], preferred_element_type=F32)
        for grp in range(chunk // 8):
            r0 = grp * 8
            r1 = (r0 // HG_SUB + 1) * HG_SUB
            n = r1 - r0
            parts = []
            for s in range(r0, r0 + 8):
                d = jnp.where(row_id[r0:r1] >= s, g[r0:r1] - g[s:s + 1], NEG_INF)
                parts.append(jnp.exp(d) * q[r0:r1] * k[s:s + 1])
            sums = jnp.dot(jnp.concatenate(parts, axis=0).astype(BF16), ones, preferred_element_type=F32)
            upd = sums[0:n] * v[r0:r0 + 1]
            for u in range(1, 8):
                upd = upd + sums[u * n:(u + 1) * n] * v[r0 + u:r0 + u + 1]
            oacc[r0:r1, :] += upd
        state_step(c, g, k, v16)

    @pl.when(factorable)
    def _():
        for c in range(n_seq * n_chunks):
            chunk_factored(c)

    @pl.when(jnp.logical_not(factorable))
    def _():
        for c in range(n_seq * n_chunks):
            chunk_exact(c)

    hg = hg_ref[...].astype(F32)
    o_ref[...] = (_rms(oacc_scr[...], nw_ref[...]) * (hg * _sigmoid(hg))).astype(o_ref.dtype)

    @pl.when(t == pl.num_programs(2) - 1)
    def _():
        for sq in range(n_seq):
            sfin_ref[sq, 0] = st_scr[sq].T


def hgrn(p, lb, nw, s0, *, batch, seq, chunk, tb, n_seq=1):
    nt = max(seq // tb, 1)
    assert tb == n_seq * seq or (n_seq == 1 and seq % tb == 0)
    kern = functools.partial(_hgrn_kernel, chunk=chunk, n_chunks=tb // (n_seq * chunk), n_seq=n_seq)

    def col(base):
        return lambda b, h, t: (b * nt + t, base // LANE + h)

    return pl.pallas_call(
        kern,
        grid=(batch // n_seq, HG_HEADS, nt),
        in_specs=[pl.BlockSpec((tb, LANE), col(COL_HQ)),
                  pl.BlockSpec((tb, LANE), col(COL_HF)),
                  pl.BlockSpec((tb, LANE), col(COL_HI)),
                  pl.BlockSpec((tb, LANE), col(COL_HG)),
                  pl.BlockSpec((1, HG_DK), lambda b, h, t: (0, h)),
                  pl.BlockSpec((1, HG_DV), lambda b, h, t: (0, 0)),
                  pl.BlockSpec((n_seq, 1, HG_DK, HG_DV), lambda b, h, t: (b, h, 0, 0))],
        out_specs=[pl.BlockSpec((tb, HG_DV), lambda b, h, t: (b * nt + t, h)),
                   pl.BlockSpec((n_seq, 1, HG_DK, HG_DV), lambda b, h, t: (b, h, 0, 0))],
        out_shape=[jax.ShapeDtypeStruct((batch * seq, HG_WIDTH), BF16),
                   jax.ShapeDtypeStruct((batch, HG_HEADS, HG_DK, HG_DV), F32)],
        scratch_shapes=[pltpu.VMEM((n_seq, HG_DV, HG_DK), F32), pltpu.VMEM((tb, HG_DV), F32),
                        pltpu.VMEM((tb, HG_DK), F32), pltpu.VMEM((tb, HG_DK), F32),
                        pltpu.VMEM((tb, HG_DK), F32)],
        compiler_params=_params(("arbitrary", "arbitrary", "arbitrary")),
        name="hgrn",
    )(p, p, p, p, lb, nw, s0)


def _out_proj_kernel(x_ref, a_ref, b_ref, w_ref, o_ref):
    o_ref[...] = (x_ref[...]
                  + jnp.dot(a_ref[...], w_ref[0:MLA_WIDTH, :], preferred_element_type=F32)
                  + jnp.dot(b_ref[...], w_ref[MLA_WIDTH:MLA_WIDTH + HG_WIDTH, :], preferred_element_type=F32))


def out_proj(x, a, b, w, *, tm):
    n, d = x.shape
    return pl.pallas_call(
        _out_proj_kernel,
        grid=(n // tm,),
        in_specs=[pl.BlockSpec((tm, d), lambda i: (i, 0)),
                  pl.BlockSpec((tm, MLA_WIDTH), lambda i: (i, 0)),
                  pl.BlockSpec((tm, HG_WIDTH), lambda i: (i, 0)),
                  pl.BlockSpec(w.shape, lambda i: (0, 0))],
        out_specs=pl.BlockSpec((tm, d), lambda i: (i, 0)),
        out_shape=jax.ShapeDtypeStruct((n, d), F32),
        compiler_params=_params(("arbitrary",)),
        name="out_proj",
    )(x, a, b, w)


def _silu(x):
    return x * _sigmoid(x)


def _swiglu_chunk(h, wg, wu, wd):
    a = _silu(jnp.dot(h, wg, preferred_element_type=F32)) * jnp.dot(h, wu, preferred_element_type=F32)
    return jnp.dot(a.astype(BF16), wd, preferred_element_type=F32)


def _ffn_kernel(x_ref, nw_ref, wg_hbm, wu_hbm, wd_hbm, o_ref, h_scr, acc_scr, wg_buf, wu_buf, wd_buf, sem, *, nf):
    i = pl.program_id(0)
    n_tiles = pl.num_programs(0)

    tf = wd_buf.shape[1]

    def chunk_copies(j, slot):
        cols = pl.ds(pl.multiple_of(j * tf, tf), tf)
        return (pltpu.make_async_copy(wg_hbm.at[:, cols], wg_buf.at[slot], sem.at[0, slot]),
                pltpu.make_async_copy(wu_hbm.at[:, cols], wu_buf.at[slot], sem.at[1, slot]),
                pltpu.make_async_copy(wd_hbm.at[cols, :], wd_buf.at[slot], sem.at[2, slot]))

    @pl.when(i == 0)
    def _():
        for cp in chunk_copies(0, 0):
            cp.start()

    h_scr[...] = _rms(x_ref[...], nw_ref[...]).astype(BF16)
    acc_scr[...] = jnp.zeros_like(acc_scr)

    def body(j, carry):
        slot = (i * nf + j) % 2

        @pl.when(j + 1 < nf)
        def _():
            for cp in chunk_copies(j + 1, 1 - slot):
                cp.start()

        @pl.when((j + 1 == nf) & (i + 1 < n_tiles))
        def _():
            for cp in chunk_copies(0, 1 - slot):
                cp.start()

        for cp in chunk_copies(j, slot):
            cp.wait()
        acc_scr[...] += _swiglu_chunk(h_scr[...], wg_buf[slot], wu_buf[slot], wd_buf[slot])
        return carry

    lax.fori_loop(0, nf, body, 0)
    o_ref[...] = x_ref[...] + acc_scr[...]


def ffn(x, nw, wg, wu, wd, *, tm, tf):
    n, d = x.shape
    nf = wg.shape[1] // tf
    any_spec = pl.BlockSpec(memory_space=pl.ANY)
    return pl.pallas_call(
        functools.partial(_ffn_kernel, nf=nf),
        grid=(n // tm,),
        in_specs=[pl.BlockSpec((tm, d), lambda i: (i, 0)),
                  pl.BlockSpec((1, d), lambda i: (0, 0)), any_spec, any_spec, any_spec],
        out_specs=pl.BlockSpec((tm, d), lambda i: (i, 0)),
        out_shape=jax.ShapeDtypeStruct((n, d), F32),
        scratch_shapes=[pltpu.VMEM((tm, d), BF16), pltpu.VMEM((tm, d), F32),
                        pltpu.VMEM((2, d, tf), BF16), pltpu.VMEM((2, d, tf), BF16),
                        pltpu.VMEM((2, tf, d), BF16), pltpu.SemaphoreType.DMA((3, 2))],
        compiler_params=_params(("arbitrary",)),
        name="ffn",
    )(x, nw, wg, wu, wd)


def _moe_route_kernel(x_ref, nw_ref, wr_ref, hp_ref, e1_ref, e2_ref, w1_ref, w2_ref, r1_ref, r2_ref,
                      cnt_ref, cnt_scr):
    @pl.when(pl.program_id(0) == 0)
    def _():
        cnt_scr[...] = jnp.zeros_like(cnt_scr)

    tm = x_ref.shape[0]
    half = x_ref.shape[1] // 2
    h = _rms(x_ref[...], nw_ref[...])
    hb = h.astype(BF16)
    h_lo = (h - hb.astype(F32)).astype(BF16)
    wr = wr_ref[...]
    w_hi = wr.astype(BF16)
    w_lo = (wr - w_hi.astype(F32)).astype(BF16)
    logits = jnp.dot(hb, w_hi, preferred_element_type=F32) + (
        jnp.dot(hb, w_lo, preferred_element_type=F32) + jnp.dot(h_lo, w_hi, preferred_element_type=F32))
    ids = lax.broadcasted_iota(jnp.int32, logits.shape, 1).astype(F32)
    m1 = jnp.max(logits, axis=-1, keepdims=True)
    i1 = jnp.min(jnp.where(logits == m1, ids, float(N_EXPERTS)), axis=-1, keepdims=True)
    rest = jnp.where(ids == i1, -jnp.inf, logits)
    m2 = jnp.max(rest, axis=-1, keepdims=True)
    i2 = jnp.min(jnp.where(rest == m2, ids, float(N_EXPERTS)), axis=-1, keepdims=True)
    t = jnp.exp(m2 - m1)
    e1_ref[...] = i1.astype(jnp.int32)
    e2_ref[...] = i2.astype(jnp.int32)
    w1_ref[...] = 1.0 / (1.0 + t)
    w2_ref[...] = t / (1.0 + t)
    oh1 = (ids == i1).astype(F32)
    oh2 = (ids == i2).astype(F32)
    both = oh1 + oh2
    before = (lax.broadcasted_iota(jnp.int32, (tm, tm), 0)
              > lax.broadcasted_iota(jnp.int32, (tm, tm), 1)).astype(BF16)
    prefix = jnp.dot(before, both.astype(BF16), preferred_element_type=F32) + cnt_scr[...]
    r1_ref[...] = jnp.sum(prefix * oh1, axis=-1, keepdims=True).astype(jnp.int32)
    r2_ref[...] = jnp.sum(prefix * oh2, axis=-1, keepdims=True).astype(jnp.int32)
    cnt_scr[...] += jnp.sum(both, axis=0, keepdims=True)
    cnt_ref[...] = cnt_scr[...].astype(jnp.int32)
    hb32 = hb.astype(F32)
    hi = pltpu.bitcast(hb32[:, 0:half], jnp.uint32)
    lo = pltpu.bitcast(hb32[:, half:2 * half], jnp.uint32)
    hp_ref[...] = hi | (lo >> 16)


def moe_route(x, nw, wr, *, tm):
    n, d = x.shape
    col = lambda dt: jax.ShapeDtypeStruct((n, 1), dt)
    col_spec = pl.BlockSpec((tm, 1), lambda i: (i, 0))
    return pl.pallas_call(
        _moe_route_kernel,
        grid=(n // tm,),
        in_specs=[pl.BlockSpec((tm, d), lambda i: (i, 0)),
                  pl.BlockSpec((1, d), lambda i: (0, 0)),
                  pl.BlockSpec(wr.shape, lambda i: (0, 0))],
        out_specs=[pl.BlockSpec((tm, d // 2), lambda i: (i, 0))] + [col_spec] * 6
                  + [pl.BlockSpec((1, N_EXPERTS), lambda i: (0, 0))],
        out_shape=[jax.ShapeDtypeStruct((n, d // 2), jnp.uint32), col(jnp.int32), col(jnp.int32),
                   col(F32), col(F32), col(jnp.int32), col(jnp.int32),
                   jax.ShapeDtypeStruct((1, N_EXPERTS), jnp.int32)],
        scratch_shapes=[pltpu.VMEM((1, N_EXPERTS), F32)],
        compiler_params=_params(("arbitrary",)),
        name="moe_route",
    )(x, nw, wr)


def _row_copy(src, src_row, dst, dst_row, sem):
    return pltpu.make_async_copy(src.at[pl.ds(src_row, 1), :], dst.at[pl.ds(dst_row, 1), :], sem)


def _moe_scatter_kernel(d1_ref, d2_ref, hp_ref, xs_in_hbm, xs_hbm, sem, *, tm):
    del xs_in_hbm

    def issue(r, carry):
        _row_copy(hp_ref, r, xs_hbm, d1_ref[0, 0, r], sem).start()
        _row_copy(hp_ref, r, xs_hbm, d2_ref[0, 0, r], sem).start()
        return carry

    lax.fori_loop(0, tm, issue, 0, unroll=8)

    def drain(r, carry):
        _row_copy(hp_ref, r, xs_hbm, d1_ref[0, 0, r], sem).wait()
        _row_copy(hp_ref, r, xs_hbm, d2_ref[0, 0, r], sem).wait()
        return carry

    lax.fori_loop(0, tm, drain, 0, unroll=8)


def moe_scatter(hp, d1, d2, xs, *, tm):
    n = hp.shape[0]
    idx_spec = pl.BlockSpec((1, 1, tm), lambda i: (i, 0, 0), memory_space=pltpu.SMEM)
    any_spec = pl.BlockSpec(memory_space=pl.ANY)
    return pl.pallas_call(
        functools.partial(_moe_scatter_kernel, tm=tm),
        grid=(n // tm,),
        in_specs=[idx_spec, idx_spec, pl.BlockSpec((tm, hp.shape[1]), lambda i: (i, 0)), any_spec],
        out_specs=any_spec,
        out_shape=jax.ShapeDtypeStruct(xs.shape, xs.dtype),
        scratch_shapes=[pltpu.SemaphoreType.DMA(())],
        input_output_aliases={3: 0},
        compiler_params=_params(("arbitrary",)),
        name="moe_scatter",
    )(d1.reshape(n // tm, 1, tm), d2.reshape(n // tm, 1, tm), hp, xs)


def _moe_expert_kernel(te_ref, na_ref, xs_ref, wg_hbm, wu_hbm, wd_hbm, ys_ref, h_scr, wg_buf, wu_buf, wd_buf, sem,
                       *, nf):
    i = pl.program_id(0)
    n_act = na_ref[0]
    active = i < n_act
    half = xs_ref.shape[1]

    tf = wd_buf.shape[1]

    def chunk_copies(tile, j, slot):
        e = te_ref[tile]
        cols = pl.ds(pl.multiple_of(j * tf, tf), tf)
        return (pltpu.make_async_copy(wg_hbm.at[e, :, cols], wg_buf.at[slot], sem.at[0, slot]),
                pltpu.make_async_copy(wu_hbm.at[e, :, cols], wu_buf.at[slot], sem.at[1, slot]),
                pltpu.make_async_copy(wd_hbm.at[e, cols, :], wd_buf.at[slot], sem.at[2, slot]))

    @pl.when(active)
    def _():
        @pl.when(i == 0)
        def _():
            for cp in chunk_copies(0, 0, 0):
                cp.start()

        packed = xs_ref[...]
        h_scr[:, 0:half] = pltpu.bitcast(packed & jnp.uint32(0xFFFF0000), F32).astype(BF16)
        h_scr[:, half:2 * half] = pltpu.bitcast(packed << 16, F32).astype(BF16)
        ys_ref[...] = jnp.zeros_like(ys_ref)

        def body(j, carry):
            slot = (i * nf + j) % 2

            @pl.when(j + 1 < nf)
            def _():
                for cp in chunk_copies(i, j + 1, 1 - slot):
                    cp.start()

            @pl.when((j + 1 == nf) & (i + 1 < n_act))
            def _():
                for cp in chunk_copies(i + 1, 0, 1 - slot):
                    cp.start()

            for cp in chunk_copies(i, j, slot):
                cp.wait()
            ys_ref[...] += _swiglu_chunk(h_scr[...], wg_buf[slot], wu_buf[slot], wd_buf[slot])
            return carry

        lax.fori_loop(0, nf, body, 0)

    @pl.when(jnp.logical_not(active))
    def _():
        ys_ref[...] = jnp.zeros_like(ys_ref)


def moe_experts(xs, tile_expert, n_active, wg, wu, wd, *, tm, tf):
    rows, half = xs.shape
    d = 2 * half
    nf = wg.shape[2] // tf
    any_spec = pl.BlockSpec(memory_space=pl.ANY)
    return pl.pallas_call(
        functools.partial(_moe_expert_kernel, nf=nf),
        grid_spec=pltpu.PrefetchScalarGridSpec(
            num_scalar_prefetch=2,
            grid=(rows // tm,),
            in_specs=[pl.BlockSpec((tm, half), lambda i, te, na: (i, 0)), any_spec, any_spec, any_spec],
            out_specs=pl.BlockSpec((tm, d), lambda i, te, na: (i, 0)),
            scratch_shapes=[pltpu.VMEM((tm, d), BF16),
                            pltpu.VMEM((2, d, tf), BF16), pltpu.VMEM((2, d, tf), BF16),
                            pltpu.VMEM((2, tf, d), BF16), pltpu.SemaphoreType.DMA((3, 2))]),
        out_shape=jax.ShapeDtypeStruct((rows, d), F32),
        compiler_params=_params(("arbitrary",)),
        name="moe_experts",
    )(tile_expert, n_active, xs, wg, wu, wd)


def _moe_combine_kernel(d1_ref, d2_ref, x_ref, w1_ref, w2_ref, fw_ref, ys_hbm, o_ref, a_scr, b_scr, sem, *, tm):
    def issue(r, carry):
        _row_copy(ys_hbm, d1_ref[0, 0, r], a_scr, r, sem).start()
        _row_copy(ys_hbm, d2_ref[0, 0, r], b_scr, r, sem).start()
        return carry

    lax.fori_loop(0, tm, issue, 0, unroll=8)

    def drain(r, carry):
        _row_copy(ys_hbm, d1_ref[0, 0, r], a_scr, r, sem).wait()
        _row_copy(ys_hbm, d2_ref[0, 0, r], b_scr, r, sem).wait()
        return carry

    lax.fori_loop(0, tm, drain, 0, unroll=8)
    y = x_ref[...] + (w1_ref[...] * a_scr[...] + w2_ref[...] * b_scr[...])
    o_ref[...] = _rms(y, fw_ref[...])


def moe_combine(x, ys, d1, d2, w1, w2, fw, *, tm):
    n, d = x.shape
    idx_spec = pl.BlockSpec((1, 1, tm), lambda i: (i, 0, 0), memory_space=pltpu.SMEM)
    col_spec = pl.BlockSpec((tm, 1), lambda i: (i, 0))
    return pl.pallas_call(
        functools.partial(_moe_combine_kernel, tm=tm),
        grid=(n // tm,),
        in_specs=[idx_spec, idx_spec, pl.BlockSpec((tm, d), lambda i: (i, 0)), col_spec, col_spec,
                  pl.BlockSpec((1, d), lambda i: (0, 0)), pl.BlockSpec(memory_space=pl.ANY)],
        out_specs=pl.BlockSpec((tm, d), lambda i: (i, 0)),
        out_shape=jax.ShapeDtypeStruct((n, d), F32),
        scratch_shapes=[pltpu.VMEM((tm, d), F32), pltpu.VMEM((tm, d), F32), pltpu.SemaphoreType.DMA(())],
        compiler_params=_params(("arbitrary",)),
        name="moe_combine",
    )(d1.reshape(n // tm, 1, tm), d2.reshape(n // tm, 1, tm), x, w1, w2, fw, ys)


def moe_final(streams, nw, wr, wg, wu, wd, fw):
    n_exp = wg.shape[0]
    d = streams[0].shape[1]
    tms = [min(TOKEN_TILE, x.shape[0]) for x in streams]
    tile = TOKEN_TILE
    routed = [moe_route(x, nw, wr, tm=tm) for x, tm in zip(streams, tms)]
    counts = [r[7][0] for r in routed]
    total = sum(counts)
    padded = (total + tile - 1) // tile * tile
    ends = jnp.cumsum(padded)
    starts = ends - padded
    n_rows = sum(2 * x.shape[0] for x in streams) + n_exp * tile
    n_tiles = n_rows // tile
    tile_expert = jnp.minimum(
        jnp.sum(jnp.arange(n_tiles, dtype=jnp.int32)[:, None] * tile >= ends[None, :], axis=1), n_exp - 1
    ).astype(jnp.int32)
    n_active = (ends[n_exp - 1:] // tile).astype(jnp.int32)

    def slot_rows(e, rank, offset):
        table = starts + offset
        return (jnp.sum(jnp.where(e == jnp.arange(n_exp)[None, :], table[None, :], 0), axis=1, keepdims=True)
                + rank).astype(jnp.int32)

    xs = jnp.zeros((n_rows, d // 2), jnp.uint32)
    dests = []
    offset = jnp.zeros((n_exp,), jnp.int32)
    for (hp, e1, e2, w1, w2, r1, r2, cnt), tm in zip(routed, tms):
        d1, d2 = slot_rows(e1, r1, offset), slot_rows(e2, r2, offset)
        xs = moe_scatter(hp, d1, d2, xs, tm=tm)
        dests.append((d1, d2))
        offset = offset + cnt[0]
    ys = moe_experts(xs, tile_expert, n_active, wg, wu, wd, tm=tile, tf=EXPERT_CHUNK)
    return [moe_combine(x, ys, d1, d2, r[3], r[4], fw, tm=tm)
            for x, r, (d1, d2), tm in zip(streams, routed, dests, tms)]


def _swap_halves(t):
    half = t.shape[-1] // 2
    return jnp.concatenate([t[..., half:], t[..., :half]], axis=-1)


def _prep_w_in(w):
    d = w.shape[0]
    k_pe = w[:, Q_LORA + KV_LORA:Q_LORA + KV_LORA + QK_ROPE]
    pad = jnp.zeros((d, LANE - QK_ROPE), w.dtype)
    return jnp.concatenate([w[:, :Q_LORA + KV_LORA], w[:, Q_LORA + KV_LORA + QK_ROPE:],
                            k_pe, pad, _swap_halves(k_pe), pad], axis=1).astype(BF16)


def _prep_w_uq(w):
    w = w.reshape(Q_LORA, MLA_HEADS, QK_NOPE + QK_ROPE)
    nope = w[..., :QK_NOPE]
    rope = w[..., QK_NOPE:]
    pad = jnp.zeros((Q_LORA, MLA_HEADS, LANE - QK_ROPE), w.dtype)
    parts = [nope, jnp.concatenate([rope, pad], -1), jnp.concatenate([_swap_halves(rope), pad], -1)]
    return jnp.concatenate([t.reshape(Q_LORA, MLA_HEADS * LANE) for t in parts], axis=1).astype(BF16)


def _rope_tables(pos):
    half = QK_ROPE // 2
    inv_freq = jnp.exp(jnp.arange(half, dtype=F32) * (-math.log(ROPE_THETA) / half))
    ang = pos.astype(F32)[:, None] * inv_freq[None, :]
    cos, sin = jnp.cos(ang), jnp.sin(ang)
    pad = jnp.zeros((pos.shape[0], LANE - QK_ROPE), F32)
    return jnp.concatenate([cos, cos, pad], -1), jnp.concatenate([-sin, sin, pad], -1)


def _row(v):
    return v.reshape(1, -1)


def kernel(x_prompt, x_sample, cache_ckv, cache_kpe, state_hgrn, attn_norm_w, w_in, q_norm_w, kv_norm_w,
           w_uq, w_uk, w_uv, hg_lower_bounds, hg_norm_w, w_out, ffn_norm_w, w_gate, w_up, w_down,
           w_router, we_gate, we_up, we_down, final_norm_w):
    depth = w_in.shape[0]
    b_p, l_p, d = x_prompt.shape
    b_s, l_s, _ = x_sample.shape
    past = cache_ckv.shape[2]
    n_p, n_s = b_p * l_p, b_s * l_s
    assert depth == 2 and d == D_MODEL

    probs = jax.nn.softmax(hg_lower_bounds.astype(F32), axis=0)
    lower_bounds = jnp.cumsum(probs, axis=0) - probs[0:1]

    cos_p, sin_p = _rope_tables(jnp.arange(l_p))
    cos_s, sin_s = _rope_tables(past + jnp.arange(l_s))
    cos_s, sin_s = jnp.tile(cos_s, (b_s, 1)), jnp.tile(sin_s, (b_s, 1))
    zero_state = jnp.zeros((b_p, HG_HEADS, HG_DK, HG_DV), F32)

    xp = x_prompt.reshape(n_p, d)
    xs = x_sample.reshape(n_s, d)
    outs = {k: [] for k in ("ckv_p", "kpe_p", "st_p", "ckv_s", "kpe_s", "st_s")}

    for l in range(depth):
        w_in_l = _prep_w_in(w_in[l])
        w_uq_l = _prep_w_uq(w_uq[l])
        w_uk_l = jnp.transpose(w_uk[l], (1, 2, 0)).astype(BF16)
        w_uv_l = jnp.transpose(w_uv[l], (1, 0, 2)).astype(BF16)
        w_out_l = w_out[l].astype(BF16)
        lb_l = _row(lower_bounds[l])
        mixers = []
        for (x, n, batch, seq, cos, sin) in ((xp, n_p, b_p, l_p, cos_p, sin_p),
                                             (xs, n_s, b_s, l_s, cos_s, sin_s)):
            tm = min(TOKEN_TILE, n)
            p = norm_matmul(x, _row(attn_norm_w[l]), w_in_l, tm=tm, tn=1024)
            q, kb, ckv, kpe = mla_prep(p, _row(q_norm_w[l]), _row(kv_norm_w[l]), w_uq_l, w_uk_l,
                                       cos, sin, tm=tm)
            if x is xp:
                o_mla = attention(q, kb, w_uv_l, batch=batch, lq=seq, lk=seq, tq=128, kb=512, causal=True)
                o_hg, st = hgrn(p, lb_l, _row(hg_norm_w[l]), zero_state, batch=batch, seq=seq,
                                chunk=CHUNK, tb=min(HG_BLOCK, seq))
            else:
                past_k = jnp.concatenate(
                    [cache_ckv[l].astype(BF16), cache_kpe[l].astype(BF16),
                     jnp.zeros((batch, past, LANE - QK_ROPE), BF16)], axis=-1)
                keys = jnp.concatenate([past_k, kb.reshape(batch, seq, QK_PAD)], axis=1)
                lk = past + seq
                o_mla = attention(q, keys.reshape(batch * lk, QK_PAD), w_uv_l, batch=batch, lq=seq, lk=lk,
                                  tq=seq, kb=lk, causal=False)
                n_seq = math.gcd(batch, 8)
                o_hg, st = hgrn(p, lb_l, _row(hg_norm_w[l]), state_hgrn[l], batch=batch, seq=seq,
                                chunk=seq, tb=n_seq * seq, n_seq=n_seq)
            x1 = out_proj(x, o_mla, o_hg, w_out_l, tm=tm)
            mixers.append((x1, ckv.reshape(batch, seq, KV_LORA), kpe.reshape(batch, seq, QK_ROPE), st))
        (xp, ckv_p, kpe_p, st_p), (xs, ckv_s, kpe_s, st_s) = mixers
        for name, val in (("ckv_p", ckv_p), ("kpe_p", kpe_p), ("st_p", st_p),
                          ("ckv_s", ckv_s), ("kpe_s", kpe_s), ("st_s", st_s)):
            outs[name].append(val)

        i = l // 2
        if l % 2 == 0:
            wg, wu, wd = w_gate[i].astype(BF16), w_up[i].astype(BF16), w_down[i].astype(BF16)
            xp = ffn(xp, _row(ffn_norm_w[l]), wg, wu, wd, tm=min(TOKEN_TILE, n_p), tf=FFN_CHUNK)
            xs = ffn(xs, _row(ffn_norm_w[l]), wg, wu, wd, tm=min(TOKEN_TILE, n_s), tf=FFN_CHUNK)
        else:
            wg, wu, wd = we_gate[i].astype(BF16), we_up[i].astype(BF16), we_down[i].astype(BF16)
            xp, xs = moe_final([xp, xs], _row(ffn_norm_w[l]), w_router[i], wg, wu, wd, _row(final_norm_w))

    return (xp.reshape(b_p, l_p, d), xs.reshape(b_s, l_s, d),
            jnp.stack(outs["ckv_p"]), jnp.stack(outs["kpe_p"]), jnp.stack(outs["st_p"]),
            jnp.stack(outs["ckv_s"]), jnp.stack(outs["kpe_s"]), jnp.stack(outs["st_s"]))
```

```python
import functools
import math

import jax
import jax.numpy as jnp
from jax import lax
from jax.experimental import pallas as pl
from jax.experimental.pallas import tpu as pltpu

F32 = jnp.float32
BF16 = jnp.bfloat16

D_MODEL = 2048
CHUNK = 64
RMS_EPS = 1e-6
NEG_INF = -1e30
LB_FLOOR = 1e-30

MLA_HEADS = 8
V_HEAD = 128
QK_NOPE = 128
QK_ROPE = 64
Q_LORA = 512
KV_LORA = 256
ROPE_THETA = 10000.0
ATTN_SCALE = (QK_NOPE + QK_ROPE) ** -0.5
EXP2_SCALE = ATTN_SCALE * math.log2(math.e)
MLA_WIDTH = MLA_HEADS * V_HEAD

HG_HEADS = 8
HG_DK = 128
HG_DV = 128
HG_KEY = HG_HEADS * HG_DK
HG_WIDTH = HG_HEADS * HG_DV
HG_SUB = 16
HG_SAFE_EXPONENT = 80.0
HG_BLOCK = 2048

N_EXPERTS = 8

LANE = 128
QK_PAD = KV_LORA + LANE

COL_CQ = 0
COL_CKV = Q_LORA
COL_HQ = Q_LORA + KV_LORA
COL_HF = COL_HQ + HG_KEY
COL_HI = COL_HF + HG_KEY
COL_HG = COL_HI + HG_WIDTH
COL_KPE = COL_HG + HG_WIDTH
IN_EXT = COL_KPE + 2 * LANE

VMEM_LIMIT = 56 * 1024 * 1024
TOKEN_TILE = 512
EXPERT_CHUNK = 256
FFN_CHUNK = 512


def _params(semantics):
    return pltpu.CompilerParams(dimension_semantics=semantics, vmem_limit_bytes=VMEM_LIMIT)


def _rms(x, w):
    return x * lax.rsqrt(jnp.mean(x * x, axis=-1, keepdims=True) + RMS_EPS) * w


def _norm_matmul_kernel(x_ref, nw_ref, w_hbm, o_ref, w_vmem, sem, *, tn):
    @pl.when(pl.program_id(0) == 0)
    def _():
        cp = pltpu.make_async_copy(w_hbm, w_vmem, sem)
        cp.start()
        cp.wait()

    h = _rms(x_ref[...], nw_ref[...]).astype(BF16)
    for c in range(o_ref.shape[1] // tn):
        o_ref[:, c * tn:(c + 1) * tn] = jnp.dot(
            h, w_vmem[:, c * tn:(c + 1) * tn], preferred_element_type=F32).astype(o_ref.dtype)


def norm_matmul(x, nw, w, *, tm, tn):
    n, d = x.shape
    cols = w.shape[1]
    return pl.pallas_call(
        functools.partial(_norm_matmul_kernel, tn=tn),
        grid=(n // tm,),
        in_specs=[pl.BlockSpec((tm, d), lambda i: (i, 0)),
                  pl.BlockSpec((1, d), lambda i: (0, 0)),
                  pl.BlockSpec(memory_space=pl.ANY)],
        out_specs=pl.BlockSpec((tm, cols), lambda i: (i, 0)),
        out_shape=jax.ShapeDtypeStruct((n, cols), BF16),
        scratch_shapes=[pltpu.VMEM((d, cols), BF16), pltpu.SemaphoreType.DMA(())],
        compiler_params=_params(("arbitrary",)),
        name="in_proj",
    )(x, nw, w)


def _mla_prep_kernel(cq_ref, ckv_ref, kp_ref, qnw_ref, kvnw_ref, wuq_ref, wuk_ref, cos_ref, sin_ref,
                     q_ref, kb_ref, ckv_out_ref, kpe_out_ref):
    cos = cos_ref[...]
    sin = sin_ref[...]
    cqn = _rms(cq_ref[...].astype(F32), qnw_ref[...]).astype(BF16)
    q = jnp.dot(cqn, wuq_ref[...], preferred_element_type=F32)
    hw = MLA_HEADS * LANE
    for h in range(MLA_HEADS):
        sl = slice(h * LANE, (h + 1) * LANE)
        q_lat = jnp.dot(q[:, sl].astype(BF16), wuk_ref[h], preferred_element_type=F32)
        rope = q[:, hw + h * LANE:hw + (h + 1) * LANE] * cos + q[:, 2 * hw + h * LANE:2 * hw + (h + 1) * LANE] * sin
        q_ref[h, :, 0:KV_LORA] = q_lat.astype(BF16)
        q_ref[h, :, KV_LORA:QK_PAD] = rope.astype(BF16)
    ckv = _rms(ckv_ref[...].astype(F32), kvnw_ref[...])
    ckv_out_ref[...] = ckv
    kb_ref[:, 0:KV_LORA] = ckv.astype(BF16)
    kp = kp_ref[...].astype(F32)
    kpe = kp[:, 0:LANE] * cos + kp[:, LANE:2 * LANE] * sin
    kpe_out_ref[...] = kpe[:, 0:QK_ROPE]
    kb_ref[:, KV_LORA:QK_PAD] = kpe.astype(BF16)


def mla_prep(p, qnw, kvnw, wuq, wuk, cos, sin, *, tm):
    n = p.shape[0]
    n_pos = cos.shape[0] // tm
    return pl.pallas_call(
        _mla_prep_kernel,
        grid=(n // tm,),
        in_specs=[pl.BlockSpec((tm, Q_LORA), lambda i: (i, COL_CQ // Q_LORA)),
                  pl.BlockSpec((tm, KV_LORA), lambda i: (i, COL_CKV // KV_LORA)),
                  pl.BlockSpec((tm, 2 * LANE), lambda i: (i, COL_KPE // (2 * LANE))),
                  pl.BlockSpec((1, Q_LORA), lambda i: (0, 0)),
                  pl.BlockSpec((1, KV_LORA), lambda i: (0, 0)),
                  pl.BlockSpec(wuq.shape, lambda i: (0, 0)),
                  pl.BlockSpec(wuk.shape, lambda i: (0, 0, 0)),
                  pl.BlockSpec((tm, LANE), lambda i: (i % n_pos, 0)),
                  pl.BlockSpec((tm, LANE), lambda i: (i % n_pos, 0))],
        out_specs=[pl.BlockSpec((MLA_HEADS, tm, QK_PAD), lambda i: (0, i, 0)),
                   pl.BlockSpec((tm, QK_PAD), lambda i: (i, 0)),
                   pl.BlockSpec((tm, KV_LORA), lambda i: (i, 0)),
                   pl.BlockSpec((tm, QK_ROPE), lambda i: (i, 0))],
        out_shape=[jax.ShapeDtypeStruct((MLA_HEADS, n, QK_PAD), BF16),
                   jax.ShapeDtypeStruct((n, QK_PAD), BF16),
                   jax.ShapeDtypeStruct((n, KV_LORA), F32),
                   jax.ShapeDtypeStruct((n, QK_ROPE), F32)],
        compiler_params=_params(("arbitrary",)),
        name="mla_prep",
    )(p, p, p, qnw, kvnw, wuq, wuk, cos, sin)


def _lanes(x, n):
    if n == LANE:
        return x
    if n % LANE == 0:
        return jnp.concatenate([x] * (n // LANE), axis=1)
    return jnp.broadcast_to(x[:, 0:1], (x.shape[0], n))


def _attention_kernel(q_ref, k_ref, wuv_ref, o_ref, m_scr, l_scr, acc_scr, *, tq, kb, causal, lk, n_split):
    rows = MLA_HEADS * tq
    part_rows = rows // n_split
    q = q_ref[...].reshape(rows, QK_PAD)
    m_scr[...] = jnp.full((rows, LANE), NEG_INF, F32)
    l_scr[...] = jnp.zeros((rows, LANE), F32)
    acc_scr[...] = jnp.zeros((rows, KV_LORA), F32)

    def block(start, size, mask):
        k = k_ref[pl.ds(start, size), :]
        v = k[:, 0:KV_LORA]
        scores = [lax.dot_general(q[part * part_rows:(part + 1) * part_rows], k, (((1,), (1,)), ((), ())),
                                  preferred_element_type=F32) for part in range(n_split)]
        for part in range(n_split):
            r = slice(part * part_rows, (part + 1) * part_rows)
            s = scores[part]
            if mask is not None:
                s = jnp.where(mask, s, NEG_INF)
            m_old = m_scr[r, :]
            m_new = jnp.maximum(m_old, jnp.max(s, axis=-1, keepdims=True))
            alpha = jnp.exp2((m_old - m_new) * EXP2_SCALE)
            p = jnp.exp2((s - _lanes(m_new, size)) * EXP2_SCALE)
            l_scr[r, :] = alpha * l_scr[r, :] + jnp.sum(p, axis=-1, keepdims=True)
            acc_scr[r, :] = acc_scr[r, :] * _lanes(alpha, KV_LORA) + jnp.dot(
                p.astype(BF16), v, preferred_element_type=F32)
            m_scr[r, :] = m_new

    if causal:
        q_start = pl.program_id(1) * tq
        n_wide = q_start // (2 * kb)

        def wide_body(j, carry):
            block(pl.multiple_of(j * 2 * kb, 2 * kb), 2 * kb, None)
            return carry

        lax.fori_loop(0, n_wide, wide_body, 0)
        n_main = q_start // kb

        @pl.when(n_main > 2 * n_wide)
        def _():
            block(pl.multiple_of(n_wide * 2 * kb, 2 * kb), kb, None)

        n_before = (q_start - n_main * kb) // tq
        for r in range(kb // tq):
            width = (r + 1) * tq

            @pl.when(n_before == r)
            def _(r=r, width=width):
                tok = jnp.bitwise_and(lax.broadcasted_iota(jnp.int32, (part_rows, width), 0), tq - 1)
                col = lax.broadcasted_iota(jnp.int32, (part_rows, width), 1)
                block(pl.multiple_of(n_main * kb, kb), width, col < r * tq + (tok // CHUNK + 1) * CHUNK)
    else:
        for j in range(lk // kb):
            block(j * kb, kb, None)

    o = (acc_scr[...] / _lanes(l_scr[...], KV_LORA)).astype(BF16)
    for h in range(MLA_HEADS):
        o_ref[:, h * V_HEAD:(h + 1) * V_HEAD] = jnp.dot(
            o[h * tq:(h + 1) * tq], wuv_ref[h], preferred_element_type=F32).astype(o_ref.dtype)


def attention(q, k, wuv, *, batch, lq, lk, tq, kb, causal):
    nq = lq // tq
    rows = MLA_HEADS * tq
    kern = functools.partial(_attention_kernel, tq=tq, kb=kb, causal=causal, lk=lk,
                             n_split=4 if causal else 1)
    return pl.pallas_call(
        kern,
        grid=(batch, nq),
        in_specs=[pl.BlockSpec((MLA_HEADS, tq, QK_PAD), lambda b, i: (0, b * nq + i, 0)),
                  pl.BlockSpec((lk, QK_PAD), lambda b, i: (b, 0)),
                  pl.BlockSpec(wuv.shape, lambda b, i: (0, 0, 0))],
        out_specs=pl.BlockSpec((tq, MLA_WIDTH), lambda b, i: (b * nq + i, 0)),
        out_shape=jax.ShapeDtypeStruct((batch * lq, MLA_WIDTH), BF16),
        scratch_shapes=[pltpu.VMEM((rows, LANE), F32), pltpu.VMEM((rows, LANE), F32),
                        pltpu.VMEM((rows, KV_LORA), F32)],
        compiler_params=_params(("arbitrary", "arbitrary")),
        name="attention_causal" if causal else "attention_full",
    )(q, k, wuv)


def _sigmoid(x):
    return 1.0 / (1.0 + jnp.exp(-x))


def _hgrn_kernel(hq_ref, hf_ref, hi_ref, hg_ref, lb_ref, nw_ref, s0_ref, o_ref, sfin_ref,
                 st_scr, oacc_scr, g_scr, q_scr, k_scr, *, chunk, n_chunks, n_seq):
    t = pl.program_id(2)

    @pl.when(t == 0)
    def _():
        for sq in range(n_seq):
            st_scr[sq] = s0_ref[sq, 0].T

    lb = lb_ref[...]
    lb_floor = jnp.maximum(lb, LB_FLOOR)
    one_m_lb = 1.0 - lb
    tri = (lax.broadcasted_iota(jnp.int32, (chunk, chunk), 0)
           >= lax.broadcasted_iota(jnp.int32, (chunk, chunk), 1))
    tri_bf16 = jnp.where(tri, 1.0, 0.0).astype(BF16)
    ones = jnp.ones((HG_DK, HG_DV), BF16)
    row_id = lax.broadcasted_iota(jnp.int32, (chunk, HG_DK), 0)
    n_sub = chunk // HG_SUB

    hf = hf_ref[...].astype(F32)
    e = jnp.exp(-jnp.abs(hf))
    big = 1.0 / (1.0 + e)
    small = e * big
    pos = hf >= 0.0
    log_f = jnp.log(lb_floor + one_m_lb * jnp.where(pos, big, small))
    k_scr[...] = one_m_lb * jnp.where(pos, small, big)
    hq = hq_ref[...].astype(F32)
    q_scr[...] = hq * _sigmoid(hq)
    lf_a = log_f.astype(BF16)
    rest = log_f - lf_a.astype(F32)
    lf_b = rest.astype(BF16)
    lf_c = (rest - lf_b.astype(F32)).astype(BF16)
    g_min = None
    for c in range(n_seq * n_chunks):
        sl = slice(c * chunk, (c + 1) * chunk)
        parts = jnp.dot(tri_bf16, jnp.concatenate([lf_a[sl], lf_b[sl], lf_c[sl]], axis=1),
                        preferred_element_type=F32)
        g = parts[:, 0:HG_DK] + parts[:, HG_DK:2 * HG_DK] + parts[:, 2 * HG_DK:3 * HG_DK]
        g_scr[sl, :] = g
        g_end = g[chunk - 1:chunk]
        g_min = g_end if g_min is None else jnp.minimum(g_min, g_end)
    factorable = jnp.min(g_min) >= -HG_SAFE_EXPONENT

    def state_step(c, g, k, v16):
        st = st_scr.at[c // n_chunks]
        g_last = g[chunk - 1:chunk]
        kh = k * jnp.exp(g_last - g)
        st[...] = st[...] * jnp.exp(g_last) + lax.dot_general(
            v16, kh.astype(BF16), (((0,), (0,)), ((), ())), preferred_element_type=F32)

    def chunk_factored(c):
        sl = slice(c * chunk, (c + 1) * chunk)
        g, q, k = g_scr[sl, :], q_scr[sl, :], k_scr[sl, :]
        v16 = hi_ref[sl, :]
        qd = (q * jnp.exp(g)).astype(BF16)
        kd = (k * jnp.exp(-g)).astype(BF16)
        a = lax.dot_general(qd, kd, (((1,), (1,)), ((), ())), preferred_element_type=F32)
        a = jnp.where(tri, a, 0.0).astype(BF16)
        oacc_scr[sl, :] = (lax.dot_general(qd, st_scr[c // n_chunks].astype(BF16), (((1,), (1,)), ((), ())),
                                           preferred_element_type=F32)
                           + jnp.dot(a, v16, preferred_element_type=F32))
        state_step(c, g, k, v16)

    def chunk_exact(c):
        sl = slice(c * chunk, (c + 1) * chunk)
        oacc = oacc_scr.at[sl, :]
        g, q, k = g_scr[sl, :], q_scr[sl, :], k_scr[sl, :]
        v16 = hi_ref[sl, :]
        v = v16.astype(F32)

        oacc[...] = lax.dot_general((q * jnp.exp(g)).astype(BF16), st_scr[c // n_chunks].astype(BF16),
                                    (((1,), (1,)), ((), ())), preferred_element_type=F32)
        for i in range(1, n_sub):
            r = i * HG_SUB
            g_edge = g[r - 1:r]
            qt = q[r:r + HG_SUB] * jnp.exp(g[r:r + HG_SUB] - g_edge)
            kt = k[0:r] * jnp.exp(g_edge - g[0:r])
            a = lax.dot_general(qt.astype(BF16), kt.astype(BF16), (((1,), (1,)), ((), ())),
                                preferred_element_type=F32)
            oacc[r:r + HG_SUB, :] += jnp.dot(a.astype(BF16), v16[0:r], preferred_element_type=F32)
        for grp in range(chunk // 8):
            r0 = grp * 8
            r1 = (r0 // HG_SUB + 1) * HG_SUB
            n = r1 - r0
            parts = []
            for s in range(r0, r0 + 8):
                d = jnp.where(row_id[r0:r1] >= s, g[r0:r1] - g[s:s + 1], NEG_INF)
                parts.append(jnp.exp(d) * q[r0:r1] * k[s:s + 1])
            sums = jnp.dot(jnp.concatenate(parts, axis=0).astype(BF16), ones, preferred_element_type=F32)
            upd = sums[0:n] * v[r0:r0 + 1]
            for u in range(1, 8):
                upd = upd + sums[u * n:(u + 1) * n] * v[r0 + u:r0 + u + 1]
            oacc[r0:r1, :] += upd
        state_step(c, g, k, v16)

    @pl.when(factorable)
    def _():
        for c in range(n_seq * n_chunks):
            chunk_factored(c)

    @pl.when(jnp.logical_not(factorable))
    def _():
        for c in range(n_seq * n_chunks):
            chunk_exact(c)

    hg = hg_ref[...].astype(F32)
    o_ref[...] = (_rms(oacc_scr[...], nw_ref[...]) * (hg * _sigmoid(hg))).astype(o_ref.dtype)

    @pl.when(t == pl.num_programs(2) - 1)
    def _():
        for sq in range(n_seq):
            sfin_ref[sq, 0] = st_scr[sq].T


def hgrn(p, lb, nw, s0, *, batch, seq, chunk, tb, n_seq=1):
    nt = max(seq // tb, 1)
    assert tb == n_seq * seq or (n_seq == 1 and seq % tb == 0)
    kern = functools.partial(_hgrn_kernel, chunk=chunk, n_chunks=tb // (n_seq * chunk), n_seq=n_seq)

    def col(base):
        return lambda b, h, t: (b * nt + t, base // LANE + h)

    return pl.pallas_call(
        kern,
        grid=(batch // n_seq, HG_HEADS, nt),
        in_specs=[pl.BlockSpec((tb, LANE), col(COL_HQ)),
                  pl.BlockSpec((tb, LANE), col(COL_HF)),
                  pl.BlockSpec((tb, LANE), col(COL_HI)),
                  pl.BlockSpec((tb, LANE), col(COL_HG)),
                  pl.BlockSpec((1, HG_DK), lambda b, h, t: (0, h)),
                  pl.BlockSpec((1, HG_DV), lambda b, h, t: (0, 0)),
                  pl.BlockSpec((n_seq, 1, HG_DK, HG_DV), lambda b, h, t: (b, h, 0, 0))],
        out_specs=[pl.BlockSpec((tb, HG_DV), lambda b, h, t: (b * nt + t, h)),
                   pl.BlockSpec((n_seq, 1, HG_DK, HG_DV), lambda b, h, t: (b, h, 0, 0))],
        out_shape=[jax.ShapeDtypeStruct((batch * seq, HG_WIDTH), BF16),
                   jax.ShapeDtypeStruct((batch, HG_HEADS, HG_DK, HG_DV), F32)],
        scratch_shapes=[pltpu.VMEM((n_seq, HG_DV, HG_DK), F32), pltpu.VMEM((tb, HG_DV), F32),
                        pltpu.VMEM((tb, HG_DK), F32), pltpu.VMEM((tb, HG_DK), F32),
                        pltpu.VMEM((tb, HG_DK), F32)],
        compiler_params=_params(("arbitrary", "arbitrary", "arbitrary")),
        name="hgrn",
    )(p, p, p, p, lb, nw, s0)


def _out_proj_kernel(x_ref, a_ref, b_ref, w_ref, o_ref):
    o_ref[...] = (x_ref[...]
                  + jnp.dot(a_ref[...], w_ref[0:MLA_WIDTH, :], preferred_element_type=F32)
                  + jnp.dot(b_ref[...], w_ref[MLA_WIDTH:MLA_WIDTH + HG_WIDTH, :], preferred_element_type=F32))


def out_proj(x, a, b, w, *, tm):
    n, d = x.shape
    return pl.pallas_call(
        _out_proj_kernel,
        grid=(n // tm,),
        in_specs=[pl.BlockSpec((tm, d), lambda i: (i, 0)),
                  pl.BlockSpec((tm, MLA_WIDTH), lambda i: (i, 0)),
                  pl.BlockSpec((tm, HG_WIDTH), lambda i: (i, 0)),
                  pl.BlockSpec(w.shape, lambda i: (0, 0))],
        out_specs=pl.BlockSpec((tm, d), lambda i: (i, 0)),
        out_shape=jax.ShapeDtypeStruct((n, d), F32),
        compiler_params=_params(("arbitrary",)),
        name="out_proj",
    )(x, a, b, w)


def _silu(x):
    return x * _sigmoid(x)


def _swiglu_chunk(h, wg, wu, wd):
    a = _silu(jnp.dot(h, wg, preferred_element_type=F32)) * jnp.dot(h, wu, preferred_element_type=F32)
    return jnp.dot(a.astype(BF16), wd, preferred_element_type=F32)


def _ffn_kernel(x_ref, nw_ref, wg_hbm, wu_hbm, wd_hbm, o_ref, h_scr, acc_scr, wg_buf, wu_buf, wd_buf, sem, *, nf):
    i = pl.program_id(0)
    n_tiles = pl.num_programs(0)

    tf = wd_buf.shape[1]

    def chunk_copies(j, slot):
        cols = pl.ds(pl.multiple_of(j * tf, tf), tf)
        return (pltpu.make_async_copy(wg_hbm.at[:, cols], wg_buf.at[slot], sem.at[0, slot]),
                pltpu.make_async_copy(wu_hbm.at[:, cols], wu_buf.at[slot], sem.at[1, slot]),
                pltpu.make_async_copy(wd_hbm.at[cols, :], wd_buf.at[slot], sem.at[2, slot]))

    @pl.when(i == 0)
    def _():
        for cp in chunk_copies(0, 0):
            cp.start()

    h_scr[...] = _rms(x_ref[...], nw_ref[...]).astype(BF16)
    acc_scr[...] = jnp.zeros_like(acc_scr)

    def body(j, carry):
        slot = (i * nf + j) % 2

        @pl.when(j + 1 < nf)
        def _():
            for cp in chunk_copies(j + 1, 1 - slot):
                cp.start()

        @pl.when((j + 1 == nf) & (i + 1 < n_tiles))
        def _():
            for cp in chunk_copies(0, 1 - slot):
                cp.start()

        for cp in chunk_copies(j, slot):
            cp.wait()
        acc_scr[...] += _swiglu_chunk(h_scr[...], wg_buf[slot], wu_buf[slot], wd_buf[slot])
        return carry

    lax.fori_loop(0, nf, body, 0)
    o_ref[...] = x_ref[...] + acc_scr[...]


def ffn(x, nw, wg, wu, wd, *, tm, tf):
    n, d = x.shape
    nf = wg.shape[1] // tf
    any_spec = pl.BlockSpec(memory_space=pl.ANY)
    return pl.pallas_call(
        functools.partial(_ffn_kernel, nf=nf),
        grid=(n // tm,),
        in_specs=[pl.BlockSpec((tm, d), lambda i: (i, 0)),
                  pl.BlockSpec((1, d), lambda i: (0, 0)), any_spec, any_spec, any_spec],
        out_specs=pl.BlockSpec((tm, d), lambda i: (i, 0)),
        out_shape=jax.ShapeDtypeStruct((n, d), F32),
        scratch_shapes=[pltpu.VMEM((tm, d), BF16), pltpu.VMEM((tm, d), F32),
                        pltpu.VMEM((2, d, tf), BF16), pltpu.VMEM((2, d, tf), BF16),
                        pltpu.VMEM((2, tf, d), BF16), pltpu.SemaphoreType.DMA((3, 2))],
        compiler_params=_params(("arbitrary",)),
        name="ffn",
    )(x, nw, wg, wu, wd)


def _moe_route_kernel(x_ref, nw_ref, wr_ref, hp_ref, e1_ref, e2_ref, w1_ref, w2_ref, r1_ref, r2_ref,
                      cnt_ref, cnt_scr):
    @pl.when(pl.program_id(0) == 0)
    def _():
        cnt_scr[...] = jnp.zeros_like(cnt_scr)

    tm = x_ref.shape[0]
    half = x_ref.shape[1] // 2
    h = _rms(x_ref[...], nw_ref[...])
    hb = h.astype(BF16)
    h_lo = (h - hb.astype(F32)).astype(BF16)
    wr = wr_ref[...]
    w_hi = wr.astype(BF16)
    w_lo = (wr - w_hi.astype(F32)).astype(BF16)
    logits = jnp.dot(hb, w_hi, preferred_element_type=F32) + (
        jnp.dot(hb, w_lo, preferred_element_type=F32) + jnp.dot(h_lo, w_hi, preferred_element_type=F32))
    ids = lax.broadcasted_iota(jnp.int32, logits.shape, 1).astype(F32)
    m1 = jnp.max(logits, axis=-1, keepdims=True)
    i1 = jnp.min(jnp.where(logits == m1, ids, float(N_EXPERTS)), axis=-1, keepdims=True)
    rest = jnp.where(ids == i1, -jnp.inf, logits)
    m2 = jnp.max(rest, axis=-1, keepdims=True)
    i2 = jnp.min(jnp.where(rest == m2, ids, float(N_EXPERTS)), axis=-1, keepdims=True)
    t = jnp.exp(m2 - m1)
    e1_ref[...] = i1.astype(jnp.int32)
    e2_ref[...] = i2.astype(jnp.int32)
    w1_ref[...] = 1.0 / (1.0 + t)
    w2_ref[...] = t / (1.0 + t)
    oh1 = (ids == i1).astype(F32)
    oh2 = (ids == i2).astype(F32)
    both = oh1 + oh2
    before = (lax.broadcasted_iota(jnp.int32, (tm, tm), 0)
              > lax.broadcasted_iota(jnp.int32, (tm, tm), 1)).astype(BF16)
    prefix = jnp.dot(before, both.astype(BF16), preferred_element_type=F32) + cnt_scr[...]
    r1_ref[...] = jnp.sum(prefix * oh1, axis=-1, keepdims=True).astype(jnp.int32)
    r2_ref[...] = jnp.sum(prefix * oh2, axis=-1, keepdims=True).astype(jnp.int32)
    cnt_scr[...] += jnp.sum(both, axis=0, keepdims=True)
    cnt_ref[...] = cnt_scr[...].astype(jnp.int32)
    hb32 = hb.astype(F32)
    hi = pltpu.bitcast(hb32[:, 0:half], jnp.uint32)
    lo = pltpu.bitcast(hb32[:, half:2 * half], jnp.uint32)
    hp_ref[...] = hi | (lo >> 16)


def moe_route(x, nw, wr, *, tm):
    n, d = x.shape
    col = lambda dt: jax.ShapeDtypeStruct((n, 1), dt)
    col_spec = pl.BlockSpec((tm, 1), lambda i: (i, 0))
    return pl.pallas_call(
        _moe_route_kernel,
        grid=(n // tm,),
        in_specs=[pl.BlockSpec((tm, d), lambda i: (i, 0)),
                  pl.BlockSpec((1, d), lambda i: (0, 0)),
                  pl.BlockSpec(wr.shape, lambda i: (0, 0))],
        out_specs=[pl.BlockSpec((tm, d // 2), lambda i: (i, 0))] + [col_spec] * 6
                  + [pl.BlockSpec((1, N_EXPERTS), lambda i: (0, 0))],
        out_shape=[jax.ShapeDtypeStruct((n, d // 2), jnp.uint32), col(jnp.int32), col(jnp.int32),
                   col(F32), col(F32), col(jnp.int32), col(jnp.int32),
                   jax.ShapeDtypeStruct((1, N_EXPERTS), jnp.int32)],
        scratch_shapes=[pltpu.VMEM((1, N_EXPERTS), F32)],
        compiler_params=_params(("arbitrary",)),
        name="moe_route",
    )(x, nw, wr)


def _row_copy(src, src_row, dst, dst_row, sem):
    return pltpu.make_async_copy(src.at[pl.ds(src_row, 1), :], dst.at[pl.ds(dst_row, 1), :], sem)


def _moe_scatter_kernel(d1_ref, d2_ref, hp_ref, xs_in_hbm, xs_hbm, sem, *, tm):
    del xs_in_hbm

    def issue(r, carry):
        _row_copy(hp_ref, r, xs_hbm, d1_ref[0, 0, r], sem).start()
        _row_copy(hp_ref, r, xs_hbm, d2_ref[0, 0, r], sem).start()
        return carry

    lax.fori_loop(0, tm, issue, 0, unroll=8)

    def drain(r, carry):
        _row_copy(hp_ref, r, xs_hbm, d1_ref[0, 0, r], sem).wait()
        _row_copy(hp_ref, r, xs_hbm, d2_ref[0, 0, r], sem).wait()
        return carry

    lax.fori_loop(0, tm, drain, 0, unroll=8)


def moe_scatter(hp, d1, d2, xs, *, tm):
    n = hp.shape[0]
    idx_spec = pl.BlockSpec((1, 1, tm), lambda i: (i, 0, 0), memory_space=pltpu.SMEM)
    any_spec = pl.BlockSpec(memory_space=pl.ANY)
    return pl.pallas_call(
        functools.partial(_moe_scatter_kernel, tm=tm),
        grid=(n // tm,),
        in_specs=[idx_spec, idx_spec, pl.BlockSpec((tm, hp.shape[1]), lambda i: (i, 0)), any_spec],
        out_specs=any_spec,
        out_shape=jax.ShapeDtypeStruct(xs.shape, xs.dtype),
        scratch_shapes=[pltpu.SemaphoreType.DMA(())],
        input_output_aliases={3: 0},
        compiler_params=_params(("arbitrary",)),
        name="moe_scatter",
    )(d1.reshape(n // tm, 1, tm), d2.reshape(n // tm, 1, tm), hp, xs)


def _moe_expert_kernel(te_ref, na_ref, xs_ref, wg_hbm, wu_hbm, wd_hbm, ys_ref, h_scr, wg_buf, wu_buf, wd_buf, sem,
                       *, nf):
    i = pl.program_id(0)
    n_act = na_ref[0]
    active = i < n_act
    half = xs_ref.shape[1]

    tf = wd_buf.shape[1]

    def chunk_copies(tile, j, slot):
        e = te_ref[tile]
        cols = pl.ds(pl.multiple_of(j * tf, tf), tf)
        return (pltpu.make_async_copy(wg_hbm.at[e, :, cols], wg_buf.at[slot], sem.at[0, slot]),
                pltpu.make_async_copy(wu_hbm.at[e, :, cols], wu_buf.at[slot], sem.at[1, slot]),
                pltpu.make_async_copy(wd_hbm.at[e, cols, :], wd_buf.at[slot], sem.at[2, slot]))

    @pl.when(active)
    def _():
        @pl.when(i == 0)
        def _():
            for cp in chunk_copies(0, 0, 0):
                cp.start()

        packed = xs_ref[...]
        h_scr[:, 0:half] = pltpu.bitcast(packed & jnp.uint32(0xFFFF0000), F32).astype(BF16)
        h_scr[:, half:2 * half] = pltpu.bitcast(packed << 16, F32).astype(BF16)
        ys_ref[...] = jnp.zeros_like(ys_ref)

        def body(j, carry):
            slot = (i * nf + j) % 2

            @pl.when(j + 1 < nf)
            def _():
                for cp in chunk_copies(i, j + 1, 1 - slot):
                    cp.start()

            @pl.when((j + 1 == nf) & (i + 1 < n_act))
            def _():
                for cp in chunk_copies(i + 1, 0, 1 - slot):
                    cp.start()

            for cp in chunk_copies(i, j, slot):
                cp.wait()
            ys_ref[...] += _swiglu_chunk(h_scr[...], wg_buf[slot], wu_buf[slot], wd_buf[slot])
            return carry

        lax.fori_loop(0, nf, body, 0)

    @pl.when(jnp.logical_not(active))
    def _():
        ys_ref[...] = jnp.zeros_like(ys_ref)


def moe_experts(xs, tile_expert, n_active, wg, wu, wd, *, tm, tf):
    rows, half = xs.shape
    d = 2 * half
    nf = wg.shape[2] // tf
    any_spec = pl.BlockSpec(memory_space=pl.ANY)
    return pl.pallas_call(
        functools.partial(_moe_expert_kernel, nf=nf),
        grid_spec=pltpu.PrefetchScalarGridSpec(
            num_scalar_prefetch=2,
            grid=(rows // tm,),
            in_specs=[pl.BlockSpec((tm, half), lambda i, te, na: (i, 0)), any_spec, any_spec, any_spec],
            out_specs=pl.BlockSpec((tm, d), lambda i, te, na: (i, 0)),
            scratch_shapes=[pltpu.VMEM((tm, d), BF16),
                            pltpu.VMEM((2, d, tf), BF16), pltpu.VMEM((2, d, tf), BF16),
                            pltpu.VMEM((2, tf, d), BF16), pltpu.SemaphoreType.DMA((3, 2))]),
        out_shape=jax.ShapeDtypeStruct((rows, d), F32),
        compiler_params=_params(("arbitrary",)),
        name="moe_experts",
    )(tile_expert, n_active, xs, wg, wu, wd)


def _moe_combine_kernel(d1_ref, d2_ref, d1_next_ref, d2_next_ref, x_ref, w1_ref, w2_ref, fw_ref, ys_hbm, o_ref,
                        a_scr, b_scr, sem, *, tm):
    i = pl.program_id(0)
    slot = i % 2

    def gather(r1_ref, r2_ref, s, wait):
        def row(r, carry):
            for src_ref, dst in ((r1_ref, a_scr), (r2_ref, b_scr)):
                cp = _row_copy(ys_hbm, src_ref[0, 0, r], dst.at[s], r, sem.at[s])
                cp.wait() if wait else cp.start()
            return carry

        lax.fori_loop(0, tm, row, 0, unroll=8)

    @pl.when(i == 0)
    def _():
        gather(d1_ref, d2_ref, 0, wait=False)

    @pl.when(i + 1 < pl.num_programs(0))
    def _():
        gather(d1_next_ref, d2_next_ref, 1 - slot, wait=False)

    gather(d1_ref, d2_ref, slot, wait=True)
    y = x_ref[...] + (w1_ref[...] * a_scr[slot] + w2_ref[...] * b_scr[slot])
    o_ref[...] = _rms(y, fw_ref[...])


def moe_combine(x, ys, d1, d2, w1, w2, fw, *, tm):
    n, d = x.shape
    nt = n // tm
    idx_spec = pl.BlockSpec((1, 1, tm), lambda i: (i, 0, 0), memory_space=pltpu.SMEM)
    next_spec = pl.BlockSpec((1, 1, tm), lambda i: (jnp.minimum(i + 1, nt - 1), 0, 0), memory_space=pltpu.SMEM)
    col_spec = pl.BlockSpec((tm, 1), lambda i: (i, 0))
    d1, d2 = d1.reshape(nt, 1, tm), d2.reshape(nt, 1, tm)
    return pl.pallas_call(
        functools.partial(_moe_combine_kernel, tm=tm),
        grid=(nt,),
        in_specs=[idx_spec, idx_spec, next_spec, next_spec, pl.BlockSpec((tm, d), lambda i: (i, 0)),
                  col_spec, col_spec, pl.BlockSpec((1, d), lambda i: (0, 0)), pl.BlockSpec(memory_space=pl.ANY)],
        out_specs=pl.BlockSpec((tm, d), lambda i: (i, 0)),
        out_shape=jax.ShapeDtypeStruct((n, d), F32),
        scratch_shapes=[pltpu.VMEM((2, tm, d), F32), pltpu.VMEM((2, tm, d), F32),
                        pltpu.SemaphoreType.DMA((2,))],
        compiler_params=_params(("arbitrary",)),
        name="moe_combine",
    )(d1, d2, d1, d2, x, w1, w2, fw, ys)


def moe_final(streams, nw, wr, wg, wu, wd, fw):
    n_exp = wg.shape[0]
    d = streams[0].shape[1]
    tms = [min(TOKEN_TILE, x.shape[0]) for x in streams]
    tile = TOKEN_TILE
    routed = [moe_route(x, nw, wr, tm=tm) for x, tm in zip(streams, tms)]
    counts = [r[7][0] for r in routed]
    total = sum(counts)
    padded = (total + tile - 1) // tile * tile
    ends = jnp.cumsum(padded)
    starts = ends - padded
    n_rows = sum(2 * x.shape[0] for x in streams) + n_exp * tile
    n_tiles = n_rows // tile
    tile_expert = jnp.minimum(
        jnp.sum(jnp.arange(n_tiles, dtype=jnp.int32)[:, None] * tile >= ends[None, :], axis=1), n_exp - 1
    ).astype(jnp.int32)
    n_active = (ends[n_exp - 1:] // tile).astype(jnp.int32)

    def slot_rows(e, rank, offset):
        table = starts + offset
        return (jnp.sum(jnp.where(e == jnp.arange(n_exp)[None, :], table[None, :], 0), axis=1, keepdims=True)
                + rank).astype(jnp.int32)

    xs = jnp.zeros((n_rows, d // 2), jnp.uint32)
    dests = []
    offset = jnp.zeros((n_exp,), jnp.int32)
    for (hp, e1, e2, w1, w2, r1, r2, cnt), tm in zip(routed, tms):
        d1, d2 = slot_rows(e1, r1, offset), slot_rows(e2, r2, offset)
        xs = moe_scatter(hp, d1, d2, xs, tm=tm)
        dests.append((d1, d2))
        offset = offset + cnt[0]
    ys = moe_experts(xs, tile_expert, n_active, wg, wu, wd, tm=tile, tf=EXPERT_CHUNK)
    return [moe_combine(x, ys, d1, d2, r[3], r[4], fw, tm=tm)
            for x, r, (d1, d2), tm in zip(streams, routed, dests, tms)]


def _swap_halves(t):
    half = t.shape[-1] // 2
    return jnp.concatenate([t[..., half:], t[..., :half]], axis=-1)


def _prep_w_in(w):
    d = w.shape[0]
    k_pe = w[:, Q_LORA + KV_LORA:Q_LORA + KV_LORA + QK_ROPE]
    pad = jnp.zeros((d, LANE - QK_ROPE), w.dtype)
    return jnp.concatenate([w[:, :Q_LORA + KV_LORA], w[:, Q_LORA + KV_LORA + QK_ROPE:],
                            k_pe, pad, _swap_halves(k_pe), pad], axis=1).astype(BF16)


def _prep_w_uq(w):
    w = w.reshape(Q_LORA, MLA_HEADS, QK_NOPE + QK_ROPE)
    nope = w[..., :QK_NOPE]
    rope = w[..., QK_NOPE:]
    pad = jnp.zeros((Q_LORA, MLA_HEADS, LANE - QK_ROPE), w.dtype)
    parts = [nope, jnp.concatenate([rope, pad], -1), jnp.concatenate([_swap_halves(rope), pad], -1)]
    return jnp.concatenate([t.reshape(Q_LORA, MLA_HEADS * LANE) for t in parts], axis=1).astype(BF16)


def _rope_tables(pos):
    half = QK_ROPE // 2
    inv_freq = jnp.exp(jnp.arange(half, dtype=F32) * (-math.log(ROPE_THETA) / half))
    ang = pos.astype(F32)[:, None] * inv_freq[None, :]
    cos, sin = jnp.cos(ang), jnp.sin(ang)
    pad = jnp.zeros((pos.shape[0], LANE - QK_ROPE), F32)
    return jnp.concatenate([cos, cos, pad], -1), jnp.concatenate([-sin, sin, pad], -1)


def _row(v):
    return v.reshape(1, -1)


def kernel(x_prompt, x_sample, cache_ckv, cache_kpe, state_hgrn, attn_norm_w, w_in, q_norm_w, kv_norm_w,
           w_uq, w_uk, w_uv, hg_lower_bounds, hg_norm_w, w_out, ffn_norm_w, w_gate, w_up, w_down,
           w_router, we_gate, we_up, we_down, final_norm_w):
    depth = w_in.shape[0]
    b_p, l_p, d = x_prompt.shape
    b_s, l_s, _ = x_sample.shape
    past = cache_ckv.shape[2]
    n_p, n_s = b_p * l_p, b_s * l_s
    assert depth == 2 and d == D_MODEL

    probs = jax.nn.softmax(hg_lower_bounds.astype(F32), axis=0)
    lower_bounds = jnp.cumsum(probs, axis=0) - probs[0:1]

    cos_p, sin_p = _rope_tables(jnp.arange(l_p))
    cos_s, sin_s = _rope_tables(past + jnp.arange(l_s))
    cos_s, sin_s = jnp.tile(cos_s, (b_s, 1)), jnp.tile(sin_s, (b_s, 1))
    zero_state = jnp.zeros((b_p, HG_HEADS, HG_DK, HG_DV), F32)

    xp = x_prompt.reshape(n_p, d)
    xs = x_sample.reshape(n_s, d)
    outs = {k: [] for k in ("ckv_p", "kpe_p", "st_p", "ckv_s", "kpe_s", "st_s")}

    for l in range(depth):
        w_in_l = _prep_w_in(w_in[l])
        w_uq_l = _prep_w_uq(w_uq[l])
        w_uk_l = jnp.transpose(w_uk[l], (1, 2, 0)).astype(BF16)
        w_uv_l = jnp.transpose(w_uv[l], (1, 0, 2)).astype(BF16)
        w_out_l = w_out[l].astype(BF16)
        lb_l = _row(lower_bounds[l])
        mixers = []
        for (x, n, batch, seq, cos, sin) in ((xp, n_p, b_p, l_p, cos_p, sin_p),
                                             (xs, n_s, b_s, l_s, cos_s, sin_s)):
            tm = min(TOKEN_TILE, n)
            p = norm_matmul(x, _row(attn_norm_w[l]), w_in_l, tm=tm, tn=1024)
            q, kb, ckv, kpe = mla_prep(p, _row(q_norm_w[l]), _row(kv_norm_w[l]), w_uq_l, w_uk_l,
                                       cos, sin, tm=tm)
            if x is xp:
                o_mla = attention(q, kb, w_uv_l, batch=batch, lq=seq, lk=seq, tq=128, kb=512, causal=True)
                o_hg, st = hgrn(p, lb_l, _row(hg_norm_w[l]), zero_state, batch=batch, seq=seq,
                                chunk=CHUNK, tb=min(HG_BLOCK, seq))
            else:
                past_k = jnp.concatenate(
                    [cache_ckv[l].astype(BF16), cache_kpe[l].astype(BF16),
                     jnp.zeros((batch, past, LANE - QK_ROPE), BF16)], axis=-1)
                keys = jnp.concatenate([past_k, kb.reshape(batch, seq, QK_PAD)], axis=1)
                lk = past + seq
                o_mla = attention(q, keys.reshape(batch * lk, QK_PAD), w_uv_l, batch=batch, lq=seq, lk=lk,
                                  tq=seq, kb=lk, causal=False)
                n_seq = math.gcd(batch, 8)
                o_hg, st = hgrn(p, lb_l, _row(hg_norm_w[l]), state_hgrn[l], batch=batch, seq=seq,
                                chunk=seq, tb=n_seq * seq, n_seq=n_seq)
            x1 = out_proj(x, o_mla, o_hg, w_out_l, tm=tm)
            mixers.append((x1, ckv.reshape(batch, seq, KV_LORA), kpe.reshape(batch, seq, QK_ROPE), st))
        (xp, ckv_p, kpe_p, st_p), (xs, ckv_s, kpe_s, st_s) = mixers
        for name, val in (("ckv_p", ckv_p), ("kpe_p", kpe_p), ("st_p", st_p),
                          ("ckv_s", ckv_s), ("kpe_s", kpe_s), ("st_s", st_s)):
            outs[name].append(val)

        i = l // 2
        if l % 2 == 0:
            wg, wu, wd = w_gate[i].astype(BF16), w_up[i].astype(BF16), w_down[i].astype(BF16)
            xp = ffn(xp, _row(ffn_norm_w[l]), wg, wu, wd, tm=min(TOKEN_TILE, n_p), tf=FFN_CHUNK)
            xs = ffn(xs, _row(ffn_norm_w[l]), wg, wu, wd, tm=min(TOKEN_TILE, n_s), tf=FFN_CHUNK)
        else:
            wg, wu, wd = we_gate[i].astype(BF16), we_up[i].astype(BF16), we_down[i].astype(BF16)
            xp, xs = moe_final([xp, xs], _row(ffn_norm_w[l]), w_router[i], wg, wu, wd, _row(final_norm_w))

    return (xp.reshape(b_p, l_p, d), xs.reshape(b_s, l_s, d),
            jnp.stack(outs["ckv_p"]), jnp.stack(outs["kpe_p"]), jnp.stack(outs["st_p"]),
            jnp.stack(outs["ckv_s"]), jnp.stack(outs["kpe_s"]), jnp.stack(outs["st_s"]))
```

```python
import functools
import math

import jax
import jax.numpy as jnp
from jax import lax
from jax.experimental import pallas as pl
from jax.experimental.pallas import tpu as pltpu

F32 = jnp.float32
BF16 = jnp.bfloat16

D_MODEL = 2048
CHUNK = 64
RMS_EPS = 1e-6
NEG_INF = -1e30
LB_FLOOR = 1e-30

MLA_HEADS = 8
V_HEAD = 128
QK_NOPE = 128
QK_ROPE = 64
Q_LORA = 512
KV_LORA = 256
ROPE_THETA = 10000.0
ATTN_SCALE = (QK_NOPE + QK_ROPE) ** -0.5
EXP2_SCALE = ATTN_SCALE * math.log2(math.e)
MLA_WIDTH = MLA_HEADS * V_HEAD

HG_HEADS = 8
HG_DK = 128
HG_DV = 128
HG_KEY = HG_HEADS * HG_DK
HG_WIDTH = HG_HEADS * HG_DV
HG_SUB = 16
HG_SAFE_EXPONENT = 80.0
HG_BLOCK = 2048

N_EXPERTS = 8

LANE = 128
QK_PAD = KV_LORA + LANE

COL_CQ = 0
COL_CKV = Q_LORA
COL_HQ = Q_LORA + KV_LORA
COL_HF = COL_HQ + HG_KEY
COL_HI = COL_HF + HG_KEY
COL_HG = COL_HI + HG_WIDTH
COL_KPE = COL_HG + HG_WIDTH
IN_EXT = COL_KPE + 2 * LANE

VMEM_LIMIT = 56 * 1024 * 1024
TOKEN_TILE = 512
EXPERT_CHUNK = 256
FFN_CHUNK = 512
IN_PROJ_COLS = 1024
ATTN_Q_TILE = 256
ATTN_KEY_BLOCK = 512


def _params(semantics):
    return pltpu.CompilerParams(dimension_semantics=semantics, vmem_limit_bytes=VMEM_LIMIT)


def _rms(x, w):
    return x * lax.rsqrt(jnp.mean(x * x, axis=-1, keepdims=True) + RMS_EPS) * w


def _norm_matmul_kernel(x_ref, nw_ref, w_hbm, o_ref, w_vmem, sem, *, tn):
    @pl.when(pl.program_id(0) == 0)
    def _():
        cp = pltpu.make_async_copy(w_hbm, w_vmem, sem)
        cp.start()
        cp.wait()

    h = _rms(x_ref[...], nw_ref[...]).astype(BF16)
    for c in range(o_ref.shape[1] // tn):
        o_ref[:, c * tn:(c + 1) * tn] = jnp.dot(
            h, w_vmem[:, c * tn:(c + 1) * tn], preferred_element_type=F32).astype(o_ref.dtype)


def norm_matmul(x, nw, w, *, tm, tn):
    n, d = x.shape
    cols = w.shape[1]
    return pl.pallas_call(
        functools.partial(_norm_matmul_kernel, tn=tn),
        grid=(n // tm,),
        in_specs=[pl.BlockSpec((tm, d), lambda i: (i, 0)),
                  pl.BlockSpec((1, d), lambda i: (0, 0)),
                  pl.BlockSpec(memory_space=pl.ANY)],
        out_specs=pl.BlockSpec((tm, cols), lambda i: (i, 0)),
        out_shape=jax.ShapeDtypeStruct((n, cols), BF16),
        scratch_shapes=[pltpu.VMEM((d, cols), BF16), pltpu.SemaphoreType.DMA(())],
        compiler_params=_params(("arbitrary",)),
        name="in_proj",
    )(x, nw, w)


def _mla_prep_kernel(cq_ref, ckv_ref, kp_ref, qnw_ref, kvnw_ref, wuq_ref, wuk_ref, cos_ref, sin_ref,
                     q_ref, kb_ref, ckv_out_ref, kpe_out_ref):
    cos = cos_ref[...]
    sin = sin_ref[...]
    cqn = _rms(cq_ref[...].astype(F32), qnw_ref[...]).astype(BF16)
    q = jnp.dot(cqn, wuq_ref[...], preferred_element_type=F32)
    hw = MLA_HEADS * LANE
    for h in range(MLA_HEADS):
        sl = slice(h * LANE, (h + 1) * LANE)
        q_lat = jnp.dot(q[:, sl].astype(BF16), wuk_ref[h], preferred_element_type=F32)
        rope = q[:, hw + h * LANE:hw + (h + 1) * LANE] * cos + q[:, 2 * hw + h * LANE:2 * hw + (h + 1) * LANE] * sin
        q_ref[h, :, 0:KV_LORA] = q_lat.astype(BF16)
        q_ref[h, :, KV_LORA:QK_PAD] = rope.astype(BF16)
    ckv = _rms(ckv_ref[...].astype(F32), kvnw_ref[...])
    ckv_out_ref[...] = ckv
    kb_ref[:, 0:KV_LORA] = ckv.astype(BF16)
    kp = kp_ref[...].astype(F32)
    kpe = kp[:, 0:LANE] * cos + kp[:, LANE:2 * LANE] * sin
    kpe_out_ref[...] = kpe[:, 0:QK_ROPE]
    kb_ref[:, KV_LORA:QK_PAD] = kpe.astype(BF16)


def mla_prep(p, qnw, kvnw, wuq, wuk, cos, sin, *, tm):
    n = p.shape[0]
    n_pos = cos.shape[0] // tm
    return pl.pallas_call(
        _mla_prep_kernel,
        grid=(n // tm,),
        in_specs=[pl.BlockSpec((tm, Q_LORA), lambda i: (i, COL_CQ // Q_LORA)),
                  pl.BlockSpec((tm, KV_LORA), lambda i: (i, COL_CKV // KV_LORA)),
                  pl.BlockSpec((tm, 2 * LANE), lambda i: (i, COL_KPE // (2 * LANE))),
                  pl.BlockSpec((1, Q_LORA), lambda i: (0, 0)),
                  pl.BlockSpec((1, KV_LORA), lambda i: (0, 0)),
                  pl.BlockSpec(wuq.shape, lambda i: (0, 0)),
                  pl.BlockSpec(wuk.shape, lambda i: (0, 0, 0)),
                  pl.BlockSpec((tm, LANE), lambda i: (i % n_pos, 0)),
                  pl.BlockSpec((tm, LANE), lambda i: (i % n_pos, 0))],
        out_specs=[pl.BlockSpec((MLA_HEADS, tm, QK_PAD), lambda i: (0, i, 0)),
                   pl.BlockSpec((tm, QK_PAD), lambda i: (i, 0)),
                   pl.BlockSpec((tm, KV_LORA), lambda i: (i, 0)),
                   pl.BlockSpec((tm, QK_ROPE), lambda i: (i, 0))],
        out_shape=[jax.ShapeDtypeStruct((MLA_HEADS, n, QK_PAD), BF16),
                   jax.ShapeDtypeStruct((n, QK_PAD), BF16),
                   jax.ShapeDtypeStruct((n, KV_LORA), F32),
                   jax.ShapeDtypeStruct((n, QK_ROPE), F32)],
        compiler_params=_params(("arbitrary",)),
        name="mla_prep",
    )(p, p, p, qnw, kvnw, wuq, wuk, cos, sin)


def _lanes(x, n):
    if n == LANE:
        return x
    if n % LANE == 0:
        return jnp.concatenate([x] * (n // LANE), axis=1)
    return jnp.broadcast_to(x[:, 0:1], (x.shape[0], n))


def _attention_kernel(q_ref, k_ref, wuv_ref, o_ref, m_scr, l_scr, acc_scr, *, tq, kb, causal, lk, n_split):
    rows = MLA_HEADS * tq
    part_rows = rows // n_split
    q = q_ref[...].reshape(rows, QK_PAD)
    m_scr[...] = jnp.full((rows, LANE), NEG_INF, F32)
    l_scr[...] = jnp.zeros((rows, LANE), F32)
    acc_scr[...] = jnp.zeros((rows, KV_LORA), F32)

    def block(start, size, mask):
        k = k_ref[pl.ds(start, size), :]
        v = k[:, 0:KV_LORA]
        scores = [lax.dot_general(q[part * part_rows:(part + 1) * part_rows], k, (((1,), (1,)), ((), ())),
                                  preferred_element_type=F32) for part in range(n_split)]
        for part in range(n_split):
            r = slice(part * part_rows, (part + 1) * part_rows)
            s = scores[part]
            if mask is not None:
                s = jnp.where(mask, s, NEG_INF)
            m_old = m_scr[r, :]
            m_new = jnp.maximum(m_old, jnp.max(s, axis=-1, keepdims=True))
            alpha = jnp.exp2((m_old - m_new) * EXP2_SCALE)
            p = jnp.exp2((s - _lanes(m_new, size)) * EXP2_SCALE)
            l_scr[r, :] = alpha * l_scr[r, :] + jnp.sum(p, axis=-1, keepdims=True)
            acc_scr[r, :] = acc_scr[r, :] * _lanes(alpha, KV_LORA) + jnp.dot(
                p.astype(BF16), v, preferred_element_type=F32)
            m_scr[r, :] = m_new

    if causal:
        q_start = pl.program_id(1) * tq
        n_wide = q_start // (2 * kb)

        def wide_body(j, carry):
            block(pl.multiple_of(j * 2 * kb, 2 * kb), 2 * kb, None)
            return carry

        lax.fori_loop(0, n_wide, wide_body, 0)
        n_main = q_start // kb

        @pl.when(n_main > 2 * n_wide)
        def _():
            block(pl.multiple_of(n_wide * 2 * kb, 2 * kb), kb, None)

        n_before = (q_start - n_main * kb) // tq
        for r in range(kb // tq):
            width = (r + 1) * tq

            @pl.when(n_before == r)
            def _(r=r, width=width):
                tok = jnp.bitwise_and(lax.broadcasted_iota(jnp.int32, (part_rows, width), 0), tq - 1)
                col = lax.broadcasted_iota(jnp.int32, (part_rows, width), 1)
                block(pl.multiple_of(n_main * kb, kb), width, col < r * tq + (tok // CHUNK + 1) * CHUNK)
    else:
        for j in range(lk // kb):
            block(j * kb, kb, None)

    o = (acc_scr[...] / _lanes(l_scr[...], KV_LORA)).astype(BF16)
    for h in range(MLA_HEADS):
        o_ref[:, h * V_HEAD:(h + 1) * V_HEAD] = jnp.dot(
            o[h * tq:(h + 1) * tq], wuv_ref[h], preferred_element_type=F32).astype(o_ref.dtype)


def attention(q, k, wuv, *, batch, lq, lk, tq, kb, causal):
    nq = lq // tq
    rows = MLA_HEADS * tq
    kern = functools.partial(_attention_kernel, tq=tq, kb=kb, causal=causal, lk=lk,
                             n_split=4 if causal else 1)
    return pl.pallas_call(
        kern,
        grid=(batch, nq),
        in_specs=[pl.BlockSpec((MLA_HEADS, tq, QK_PAD), lambda b, i: (0, b * nq + i, 0)),
                  pl.BlockSpec((lk, QK_PAD), lambda b, i: (b, 0)),
                  pl.BlockSpec(wuv.shape, lambda b, i: (0, 0, 0))],
        out_specs=pl.BlockSpec((tq, MLA_WIDTH), lambda b, i: (b * nq + i, 0)),
        out_shape=jax.ShapeDtypeStruct((batch * lq, MLA_WIDTH), BF16),
        scratch_shapes=[pltpu.VMEM((rows, LANE), F32), pltpu.VMEM((rows, LANE), F32),
                        pltpu.VMEM((rows, KV_LORA), F32)],
        compiler_params=_params(("arbitrary", "arbitrary")),
        name="attention_causal" if causal else "attention_full",
    )(q, k, wuv)


def _sigmoid(x):
    return 1.0 / (1.0 + jnp.exp(-x))


def _hgrn_kernel(hq_ref, hf_ref, hi_ref, hg_ref, lb_ref, nw_ref, s0_ref, o_ref, sfin_ref,
                 st_scr, oacc_scr, g_scr, q_scr, k_scr, *, chunk, n_chunks, n_seq):
    t = pl.program_id(2)

    @pl.when(t == 0)
    def _():
        for sq in range(n_seq):
            st_scr[sq] = s0_ref[sq, 0].T

    lb = lb_ref[...]
    lb_floor = jnp.maximum(lb, LB_FLOOR)
    one_m_lb = 1.0 - lb
    tri = (lax.broadcasted_iota(jnp.int32, (chunk, chunk), 0)
           >= lax.broadcasted_iota(jnp.int32, (chunk, chunk), 1))
    tri_bf16 = jnp.where(tri, 1.0, 0.0).astype(BF16)
    ones = jnp.ones((HG_DK, HG_DV), BF16)
    row_id = lax.broadcasted_iota(jnp.int32, (chunk, HG_DK), 0)
    n_sub = chunk // HG_SUB

    hf = hf_ref[...].astype(F32)
    e = jnp.exp(-jnp.abs(hf))
    big = 1.0 / (1.0 + e)
    small = e * big
    pos = hf >= 0.0
    log_f = jnp.log(lb_floor + one_m_lb * jnp.where(pos, big, small))
    k_scr[...] = one_m_lb * jnp.where(pos, small, big)
    hq = hq_ref[...].astype(F32)
    q_scr[...] = hq * _sigmoid(hq)
    lf_a = log_f.astype(BF16)
    rest = log_f - lf_a.astype(F32)
    lf_b = rest.astype(BF16)
    lf_c = (rest - lf_b.astype(F32)).astype(BF16)
    g_min = None
    for c in range(n_seq * n_chunks):
        sl = slice(c * chunk, (c + 1) * chunk)
        parts = jnp.dot(tri_bf16, jnp.concatenate([lf_a[sl], lf_b[sl], lf_c[sl]], axis=1),
                        preferred_element_type=F32)
        g = parts[:, 0:HG_DK] + parts[:, HG_DK:2 * HG_DK] + parts[:, 2 * HG_DK:3 * HG_DK]
        g_scr[sl, :] = g
        g_end = g[chunk - 1:chunk]
        g_min = g_end if g_min is None else jnp.minimum(g_min, g_end)
    factorable = jnp.min(g_min) >= -HG_SAFE_EXPONENT

    def state_step(c, g, k, v16):
        st = st_scr.at[c // n_chunks]
        g_last = g[chunk - 1:chunk]
        kh = k * jnp.exp(g_last - g)
        st[...] = st[...] * jnp.exp(g_last) + lax.dot_general(
            v16, kh.astype(BF16), (((0,), (0,)), ((), ())), preferred_element_type=F32)

    def chunk_factored(c):
        sl = slice(c * chunk, (c + 1) * chunk)
        g, q, k = g_scr[sl, :], q_scr[sl, :], k_scr[sl, :]
        v16 = hi_ref[sl, :]
        qd = (q * jnp.exp(g)).astype(BF16)
        kd = (k * jnp.exp(-g)).astype(BF16)
        a = lax.dot_general(qd, kd, (((1,), (1,)), ((), ())), preferred_element_type=F32)
        a = jnp.where(tri, a, 0.0).astype(BF16)
        oacc_scr[sl, :] = (lax.dot_general(qd, st_scr[c // n_chunks].astype(BF16), (((1,), (1,)), ((), ())),
                                           preferred_element_type=F32)
                           + jnp.dot(a, v16, preferred_element_type=F32))
        state_step(c, g, k, v16)

    def chunk_exact(c):
        sl = slice(c * chunk, (c + 1) * chunk)
        oacc = oacc_scr.at[sl, :]
        g, q, k = g_scr[sl, :], q_scr[sl, :], k_scr[sl, :]
        v16 = hi_ref[sl, :]
        v = v16.astype(F32)

        oacc[...] = lax.dot_general((q * jnp.exp(g)).astype(BF16), st_scr[c // n_chunks].astype(BF16),
                                    (((1,), (1,)), ((), ())), preferred_element_type=F32)
        for i in range(1, n_sub):
            r = i * HG_SUB
            g_edge = g[r - 1:r]
            qt = q[r:r + HG_SUB] * jnp.exp(g[r:r + HG_SUB] - g_edge)
            kt = k[0:r] * jnp.exp(g_edge - g[0:r])
            a = lax.dot_general(qt.astype(BF16), kt.astype(BF16), (((1,), (1,)), ((), ())),
                                preferred_element_type=F32)
            oacc[r:r + HG_SUB, :] += jnp.dot(a.astype(BF16), v16[0:r], preferred_element_type=F32)
        for grp in range(chunk // 8):
            r0 = grp * 8
            r1 = (r0 // HG_SUB + 1) * HG_SUB
            n = r1 - r0
            parts = []
            for s in range(r0, r0 + 8):
                d = jnp.where(row_id[r0:r1] >= s, g[r0:r1] - g[s:s + 1], NEG_INF)
                parts.append(jnp.exp(d) * q[r0:r1] * k[s:s + 1])
            sums = jnp.dot(jnp.concatenate(parts, axis=0).astype(BF16), ones, preferred_element_type=F32)
            upd = sums[0:n] * v[r0:r0 + 1]
            for u in range(1, 8):
                upd = upd + sums[u * n:(u + 1) * n] * v[r0 + u:r0 + u + 1]
            oacc[r0:r1, :] += upd
        state_step(c, g, k, v16)

    @pl.when(factorable)
    def _():
        for c in range(n_seq * n_chunks):
            chunk_factored(c)

    @pl.when(jnp.logical_not(factorable))
    def _():
        for c in range(n_seq * n_chunks):
            chunk_exact(c)

    hg = hg_ref[...].astype(F32)
    o_ref[...] = (_rms(oacc_scr[...], nw_ref[...]) * (hg * _sigmoid(hg))).astype(o_ref.dtype)

    @pl.when(t == pl.num_programs(2) - 1)
    def _():
        for sq in range(n_seq):
            sfin_ref[sq, 0] = st_scr[sq].T


def hgrn(p, lb, nw, s0, *, batch, seq, chunk, tb, n_seq=1):
    nt = max(seq // tb, 1)
    assert tb == n_seq * seq or (n_seq == 1 and seq % tb == 0)
    kern = functools.partial(_hgrn_kernel, chunk=chunk, n_chunks=tb // (n_seq * chunk), n_seq=n_seq)

    def col(base):
        return lambda b, h, t: (b * nt + t, base // LANE + h)

    return pl.pallas_call(
        kern,
        grid=(batch // n_seq, HG_HEADS, nt),
        in_specs=[pl.BlockSpec((tb, LANE), col(COL_HQ)),
                  pl.BlockSpec((tb, LANE), col(COL_HF)),
                  pl.BlockSpec((tb, LANE), col(COL_HI)),
                  pl.BlockSpec((tb, LANE), col(COL_HG)),
                  pl.BlockSpec((1, HG_DK), lambda b, h, t: (0, h)),
                  pl.BlockSpec((1, HG_DV), lambda b, h, t: (0, 0)),
                  pl.BlockSpec((n_seq, 1, HG_DK, HG_DV), lambda b, h, t: (b, h, 0, 0))],
        out_specs=[pl.BlockSpec((tb, HG_DV), lambda b, h, t: (b * nt + t, h)),
                   pl.BlockSpec((n_seq, 1, HG_DK, HG_DV), lambda b, h, t: (b, h, 0, 0))],
        out_shape=[jax.ShapeDtypeStruct((batch * seq, HG_WIDTH), BF16),
                   jax.ShapeDtypeStruct((batch, HG_HEADS, HG_DK, HG_DV), F32)],
        scratch_shapes=[pltpu.VMEM((n_seq, HG_DV, HG_DK), F32), pltpu.VMEM((tb, HG_DV), F32),
                        pltpu.VMEM((tb, HG_DK), F32), pltpu.VMEM((tb, HG_DK), F32),
                        pltpu.VMEM((tb, HG_DK), F32)],
        compiler_params=_params(("arbitrary", "arbitrary", "arbitrary")),
        name="hgrn",
    )(p, p, p, p, lb, nw, s0)


def _out_proj_kernel(x_ref, a_ref, b_ref, w_ref, o_ref):
    o_ref[...] = (x_ref[...]
                  + jnp.dot(a_ref[...], w_ref[0:MLA_WIDTH, :], preferred_element_type=F32)
                  + jnp.dot(b_ref[...], w_ref[MLA_WIDTH:MLA_WIDTH + HG_WIDTH, :], preferred_element_type=F32))


def out_proj(x, a, b, w, *, tm):
    n, d = x.shape
    return pl.pallas_call(
        _out_proj_kernel,
        grid=(n // tm,),
        in_specs=[pl.BlockSpec((tm, d), lambda i: (i, 0)),
                  pl.BlockSpec((tm, MLA_WIDTH), lambda i: (i, 0)),
                  pl.BlockSpec((tm, HG_WIDTH), lambda i: (i, 0)),
                  pl.BlockSpec(w.shape, lambda i: (0, 0))],
        out_specs=pl.BlockSpec((tm, d), lambda i: (i, 0)),
        out_shape=jax.ShapeDtypeStruct((n, d), F32),
        compiler_params=_params(("arbitrary",)),
        name="out_proj",
    )(x, a, b, w)


def _silu(x):
    return x * _sigmoid(x)


def _swiglu_chunk(h, wg, wu, wd):
    a = _silu(jnp.dot(h, wg, preferred_element_type=F32)) * jnp.dot(h, wu, preferred_element_type=F32)
    return jnp.dot(a.astype(BF16), wd, preferred_element_type=F32)


def _ffn_kernel(x_ref, nw_ref, wg_hbm, wu_hbm, wd_hbm, o_ref, h_scr, acc_scr, wg_buf, wu_buf, wd_buf, sem, *, nf):
    i = pl.program_id(0)
    n_tiles = pl.num_programs(0)

    tf = wd_buf.shape[1]

    def chunk_copies(j, slot):
        cols = pl.ds(pl.multiple_of(j * tf, tf), tf)
        return (pltpu.make_async_copy(wg_hbm.at[:, cols], wg_buf.at[slot], sem.at[0, slot]),
                pltpu.make_async_copy(wu_hbm.at[:, cols], wu_buf.at[slot], sem.at[1, slot]),
                pltpu.make_async_copy(wd_hbm.at[cols, :], wd_buf.at[slot], sem.at[2, slot]))

    @pl.when(i == 0)
    def _():
        for cp in chunk_copies(0, 0):
            cp.start()

    h_scr[...] = _rms(x_ref[...], nw_ref[...]).astype(BF16)
    acc_scr[...] = jnp.zeros_like(acc_scr)

    def body(j, carry):
        slot = (i * nf + j) % 2

        @pl.when(j + 1 < nf)
        def _():
            for cp in chunk_copies(j + 1, 1 - slot):
                cp.start()

        @pl.when((j + 1 == nf) & (i + 1 < n_tiles))
        def _():
            for cp in chunk_copies(0, 1 - slot):
                cp.start()

        for cp in chunk_copies(j, slot):
            cp.wait()
        acc_scr[...] += _swiglu_chunk(h_scr[...], wg_buf[slot], wu_buf[slot], wd_buf[slot])
        return carry

    lax.fori_loop(0, nf, body, 0)
    o_ref[...] = x_ref[...] + acc_scr[...]


def ffn(x, nw, wg, wu, wd, *, tm, tf):
    n, d = x.shape
    nf = wg.shape[1] // tf
    any_spec = pl.BlockSpec(memory_space=pl.ANY)
    return pl.pallas_call(
        functools.partial(_ffn_kernel, nf=nf),
        grid=(n // tm,),
        in_specs=[pl.BlockSpec((tm, d), lambda i: (i, 0)),
                  pl.BlockSpec((1, d), lambda i: (0, 0)), any_spec, any_spec, any_spec],
        out_specs=pl.BlockSpec((tm, d), lambda i: (i, 0)),
        out_shape=jax.ShapeDtypeStruct((n, d), F32),
        scratch_shapes=[pltpu.VMEM((tm, d), BF16), pltpu.VMEM((tm, d), F32),
                        pltpu.VMEM((2, d, tf), BF16), pltpu.VMEM((2, d, tf), BF16),
                        pltpu.VMEM((2, tf, d), BF16), pltpu.SemaphoreType.DMA((3, 2))],
        compiler_params=_params(("arbitrary",)),
        name="ffn",
    )(x, nw, wg, wu, wd)


def _moe_route_kernel(x_ref, nw_ref, wr_ref, hp_ref, e1_ref, e2_ref, w1_ref, w2_ref, r1_ref, r2_ref,
                      cnt_ref, cnt_scr):
    @pl.when(pl.program_id(0) == 0)
    def _():
        cnt_scr[...] = jnp.zeros_like(cnt_scr)

    tm = x_ref.shape[0]
    half = x_ref.shape[1] // 2
    h = _rms(x_ref[...], nw_ref[...])
    hb = h.astype(BF16)
    h_lo = (h - hb.astype(F32)).astype(BF16)
    wr = wr_ref[...]
    w_hi = wr.astype(BF16)
    w_lo = (wr - w_hi.astype(F32)).astype(BF16)
    logits = jnp.dot(hb, w_hi, preferred_element_type=F32) + (
        jnp.dot(hb, w_lo, preferred_element_type=F32) + jnp.dot(h_lo, w_hi, preferred_element_type=F32))
    ids = lax.broadcasted_iota(jnp.int32, logits.shape, 1).astype(F32)
    m1 = jnp.max(logits, axis=-1, keepdims=True)
    i1 = jnp.min(jnp.where(logits == m1, ids, float(N_EXPERTS)), axis=-1, keepdims=True)
    rest = jnp.where(ids == i1, -jnp.inf, logits)
    m2 = jnp.max(rest, axis=-1, keepdims=True)
    i2 = jnp.min(jnp.where(rest == m2, ids, float(N_EXPERTS)), axis=-1, keepdims=True)
    t = jnp.exp(m2 - m1)
    e1_ref[...] = i1.astype(jnp.int32)
    e2_ref[...] = i2.astype(jnp.int32)
    w1_ref[...] = 1.0 / (1.0 + t)
    w2_ref[...] = t / (1.0 + t)
    oh1 = (ids == i1).astype(F32)
    oh2 = (ids == i2).astype(F32)
    both = oh1 + oh2
    before = (lax.broadcasted_iota(jnp.int32, (tm, tm), 0)
              > lax.broadcasted_iota(jnp.int32, (tm, tm), 1)).astype(BF16)
    prefix = jnp.dot(before, both.astype(BF16), preferred_element_type=F32) + cnt_scr[...]
    r1_ref[...] = jnp.sum(prefix * oh1, axis=-1, keepdims=True).astype(jnp.int32)
    r2_ref[...] = jnp.sum(prefix * oh2, axis=-1, keepdims=True).astype(jnp.int32)
    cnt_scr[...] += jnp.sum(both, axis=0, keepdims=True)
    cnt_ref[...] = cnt_scr[...].astype(jnp.int32)
    hb32 = hb.astype(F32)
    hi = pltpu.bitcast(hb32[:, 0:half], jnp.uint32)
    lo = pltpu.bitcast(hb32[:, half:2 * half], jnp.uint32)
    hp_ref[...] = hi | (lo >> 16)


def moe_route(x, nw, wr, *, tm):
    n, d = x.shape
    col = lambda dt: jax.ShapeDtypeStruct((n, 1), dt)
    col_spec = pl.BlockSpec((tm, 1), lambda i: (i, 0))
    return pl.pallas_call(
        _moe_route_kernel,
        grid=(n // tm,),
        in_specs=[pl.BlockSpec((tm, d), lambda i: (i, 0)),
                  pl.BlockSpec((1, d), lambda i: (0, 0)),
                  pl.BlockSpec(wr.shape, lambda i: (0, 0))],
        out_specs=[pl.BlockSpec((tm, d // 2), lambda i: (i, 0))] + [col_spec] * 6
                  + [pl.BlockSpec((1, N_EXPERTS), lambda i: (0, 0))],
        out_shape=[jax.ShapeDtypeStruct((n, d // 2), jnp.uint32), col(jnp.int32), col(jnp.int32),
                   col(F32), col(F32), col(jnp.int32), col(jnp.int32),
                   jax.ShapeDtypeStruct((1, N_EXPERTS), jnp.int32)],
        scratch_shapes=[pltpu.VMEM((1, N_EXPERTS), F32)],
        compiler_params=_params(("arbitrary",)),
        name="moe_route",
    )(x, nw, wr)


def _row_copy(src, src_row, dst, dst_row, sem):
    return pltpu.make_async_copy(src.at[pl.ds(src_row, 1), :], dst.at[pl.ds(dst_row, 1), :], sem)


def _moe_scatter_kernel(d1_ref, d2_ref, hp_ref, xs_in_hbm, xs_hbm, sem, *, tm):
    del xs_in_hbm

    def issue(r, carry):
        _row_copy(hp_ref, r, xs_hbm, d1_ref[0, 0, r], sem).start()
        _row_copy(hp_ref, r, xs_hbm, d2_ref[0, 0, r], sem).start()
        return carry

    lax.fori_loop(0, tm, issue, 0, unroll=8)

    def drain(r, carry):
        _row_copy(hp_ref, r, xs_hbm, d1_ref[0, 0, r], sem).wait()
        _row_copy(hp_ref, r, xs_hbm, d2_ref[0, 0, r], sem).wait()
        return carry

    lax.fori_loop(0, tm, drain, 0, unroll=8)


def moe_scatter(hp, d1, d2, xs, *, tm):
    n = hp.shape[0]
    idx_spec = pl.BlockSpec((1, 1, tm), lambda i: (i, 0, 0), memory_space=pltpu.SMEM)
    any_spec = pl.BlockSpec(memory_space=pl.ANY)
    return pl.pallas_call(
        functools.partial(_moe_scatter_kernel, tm=tm),
        grid=(n // tm,),
        in_specs=[idx_spec, idx_spec, pl.BlockSpec((tm, hp.shape[1]), lambda i: (i, 0)), any_spec],
        out_specs=any_spec,
        out_shape=jax.ShapeDtypeStruct(xs.shape, xs.dtype),
        scratch_shapes=[pltpu.SemaphoreType.DMA(())],
        input_output_aliases={3: 0},
        compiler_params=_params(("arbitrary",)),
        name="moe_scatter",
    )(d1.reshape(n // tm, 1, tm), d2.reshape(n // tm, 1, tm), hp, xs)


def _moe_expert_kernel(te_ref, na_ref, xs_ref, wg_hbm, wu_hbm, wd_hbm, ys_ref, h_scr, wg_buf, wu_buf, wd_buf, sem,
                       *, nf):
    i = pl.program_id(0)
    n_act = na_ref[0]
    active = i < n_act
    half = xs_ref.shape[1]

    tf = wd_buf.shape[1]

    def chunk_copies(tile, j, slot):
        e = te_ref[tile]
        cols = pl.ds(pl.multiple_of(j * tf, tf), tf)
        return (pltpu.make_async_copy(wg_hbm.at[e, :, cols], wg_buf.at[slot], sem.at[0, slot]),
                pltpu.make_async_copy(wu_hbm.at[e, :, cols], wu_buf.at[slot], sem.at[1, slot]),
                pltpu.make_async_copy(wd_hbm.at[e, cols, :], wd_buf.at[slot], sem.at[2, slot]))

    @pl.when(active)
    def _():
        @pl.when(i == 0)
        def _():
            for cp in chunk_copies(0, 0, 0):
                cp.start()

        packed = xs_ref[...]
        h_scr[:, 0:half] = pltpu.bitcast(packed & jnp.uint32(0xFFFF0000), F32).astype(BF16)
        h_scr[:, half:2 * half] = pltpu.bitcast(packed << 16, F32).astype(BF16)
        ys_ref[...] = jnp.zeros_like(ys_ref)

        def body(j, carry):
            slot = (i * nf + j) % 2

            @pl.when(j + 1 < nf)
            def _():
                for cp in chunk_copies(i, j + 1, 1 - slot):
                    cp.start()

            @pl.when((j + 1 == nf) & (i + 1 < n_act))
            def _():
                for cp in chunk_copies(i + 1, 0, 1 - slot):
                    cp.start()

            for cp in chunk_copies(i, j, slot):
                cp.wait()
            ys_ref[...] += _swiglu_chunk(h_scr[...], wg_buf[slot], wu_buf[slot], wd_buf[slot])
            return carry

        lax.fori_loop(0, nf, body, 0)

    @pl.when(jnp.logical_not(active))
    def _():
        ys_ref[...] = jnp.zeros_like(ys_ref)


def moe_experts(xs, tile_expert, n_active, wg, wu, wd, *, tm, tf):
    rows, half = xs.shape
    d = 2 * half
    nf = wg.shape[2] // tf
    any_spec = pl.BlockSpec(memory_space=pl.ANY)
    return pl.pallas_call(
        functools.partial(_moe_expert_kernel, nf=nf),
        grid_spec=pltpu.PrefetchScalarGridSpec(
            num_scalar_prefetch=2,
            grid=(rows // tm,),
            in_specs=[pl.BlockSpec((tm, half), lambda i, te, na: (i, 0)), any_spec, any_spec, any_spec],
            out_specs=pl.BlockSpec((tm, d), lambda i, te, na: (i, 0)),
            scratch_shapes=[pltpu.VMEM((tm, d), BF16),
                            pltpu.VMEM((2, d, tf), BF16), pltpu.VMEM((2, d, tf), BF16),
                            pltpu.VMEM((2, tf, d), BF16), pltpu.SemaphoreType.DMA((3, 2))]),
        out_shape=jax.ShapeDtypeStruct((rows, d), F32),
        compiler_params=_params(("arbitrary",)),
        name="moe_experts",
    )(tile_expert, n_active, xs, wg, wu, wd)


def _moe_combine_kernel(d1_ref, d2_ref, d1_next_ref, d2_next_ref, x_ref, w1_ref, w2_ref, fw_ref, ys_hbm, o_ref,
                        a_scr, b_scr, sem, *, tm):
    i = pl.program_id(0)
    slot = i % 2

    def gather(r1_ref, r2_ref, s, wait):
        def row(r, carry):
            for src_ref, dst in ((r1_ref, a_scr), (r2_ref, b_scr)):
                cp = _row_copy(ys_hbm, src_ref[0, 0, r], dst.at[s], r, sem.at[s])
                cp.wait() if wait else cp.start()
            return carry

        lax.fori_loop(0, tm, row, 0, unroll=8)

    @pl.when(i == 0)
    def _():
        gather(d1_ref, d2_ref, 0, wait=False)

    @pl.when(i + 1 < pl.num_programs(0))
    def _():
        gather(d1_next_ref, d2_next_ref, 1 - slot, wait=False)

    gather(d1_ref, d2_ref, slot, wait=True)
    y = x_ref[...] + (w1_ref[...] * a_scr[slot] + w2_ref[...] * b_scr[slot])
    o_ref[...] = _rms(y, fw_ref[...])


def moe_combine(x, ys, d1, d2, w1, w2, fw, *, tm):
    n, d = x.shape
    nt = n // tm
    idx_spec = pl.BlockSpec((1, 1, tm), lambda i: (i, 0, 0), memory_space=pltpu.SMEM)
    next_spec = pl.BlockSpec((1, 1, tm), lambda i: (jnp.minimum(i + 1, nt - 1), 0, 0), memory_space=pltpu.SMEM)
    col_spec = pl.BlockSpec((tm, 1), lambda i: (i, 0))
    d1, d2 = d1.reshape(nt, 1, tm), d2.reshape(nt, 1, tm)
    return pl.pallas_call(
        functools.partial(_moe_combine_kernel, tm=tm),
        grid=(nt,),
        in_specs=[idx_spec, idx_spec, next_spec, next_spec, pl.BlockSpec((tm, d), lambda i: (i, 0)),
                  col_spec, col_spec, pl.BlockSpec((1, d), lambda i: (0, 0)), pl.BlockSpec(memory_space=pl.ANY)],
        out_specs=pl.BlockSpec((tm, d), lambda i: (i, 0)),
        out_shape=jax.ShapeDtypeStruct((n, d), F32),
        scratch_shapes=[pltpu.VMEM((2, tm, d), F32), pltpu.VMEM((2, tm, d), F32),
                        pltpu.SemaphoreType.DMA((2,))],
        compiler_params=_params(("arbitrary",)),
        name="moe_combine",
    )(d1, d2, d1, d2, x, w1, w2, fw, ys)


def moe_final(streams, nw, wr, wg, wu, wd, fw):
    n_exp = wg.shape[0]
    d = streams[0].shape[1]
    tms = [min(TOKEN_TILE, x.shape[0]) for x in streams]
    tile = TOKEN_TILE
    routed = [moe_route(x, nw, wr, tm=tm) for x, tm in zip(streams, tms)]
    counts = [r[7][0] for r in routed]
    total = sum(counts)
    padded = (total + tile - 1) // tile * tile
    ends = jnp.cumsum(padded)
    starts = ends - padded
    n_rows = sum(2 * x.shape[0] for x in streams) + n_exp * tile
    n_tiles = n_rows // tile
    tile_expert = jnp.minimum(
        jnp.sum(jnp.arange(n_tiles, dtype=jnp.int32)[:, None] * tile >= ends[None, :], axis=1), n_exp - 1
    ).astype(jnp.int32)
    n_active = (ends[n_exp - 1:] // tile).astype(jnp.int32)

    def slot_rows(e, rank, offset):
        table = starts + offset
        return (jnp.sum(jnp.where(e == jnp.arange(n_exp)[None, :], table[None, :], 0), axis=1, keepdims=True)
                + rank).astype(jnp.int32)

    xs = jnp.zeros((n_rows, d // 2), jnp.uint32)
    dests = []
    offset = jnp.zeros((n_exp,), jnp.int32)
    for (hp, e1, e2, w1, w2, r1, r2, cnt), tm in zip(routed, tms):
        d1, d2 = slot_rows(e1, r1, offset), slot_rows(e2, r2, offset)
        xs = moe_scatter(hp, d1, d2, xs, tm=tm)
        dests.append((d1, d2))
        offset = offset + cnt[0]
    ys = moe_experts(xs, tile_expert, n_active, wg, wu, wd, tm=tile, tf=EXPERT_CHUNK)
    return [moe_combine(x, ys, d1, d2, r[3], r[4], fw, tm=tm)
            for x, r, (d1, d2), tm in zip(streams, routed, dests, tms)]


def _swap_halves(t):
    half = t.shape[-1] // 2
    return jnp.concatenate([t[..., half:], t[..., :half]], axis=-1)


def _prep_w_in(w):
    d = w.shape[0]
    k_pe = w[:, Q_LORA + KV_LORA:Q_LORA + KV_LORA + QK_ROPE]
    pad = jnp.zeros((d, LANE - QK_ROPE), w.dtype)
    return jnp.concatenate([w[:, :Q_LORA + KV_LORA], w[:, Q_LORA + KV_LORA + QK_ROPE:],
                            k_pe, pad, _swap_halves(k_pe), pad], axis=1).astype(BF16)


def _prep_w_uq(w):
    w = w.reshape(Q_LORA, MLA_HEADS, QK_NOPE + QK_ROPE)
    nope = w[..., :QK_NOPE]
    rope = w[..., QK_NOPE:]
    pad = jnp.zeros((Q_LORA, MLA_HEADS, LANE - QK_ROPE), w.dtype)
    parts = [nope, jnp.concatenate([rope, pad], -1), jnp.concatenate([_swap_halves(rope), pad], -1)]
    return jnp.concatenate([t.reshape(Q_LORA, MLA_HEADS * LANE) for t in parts], axis=1).astype(BF16)


def _rope_tables(pos):
    half = QK_ROPE // 2
    inv_freq = jnp.exp(jnp.arange(half, dtype=F32) * (-math.log(ROPE_THETA) / half))
    ang = pos.astype(F32)[:, None] * inv_freq[None, :]
    cos, sin = jnp.cos(ang), jnp.sin(ang)
    pad = jnp.zeros((pos.shape[0], LANE - QK_ROPE), F32)
    return jnp.concatenate([cos, cos, pad], -1), jnp.concatenate([-sin, sin, pad], -1)


def _row(v):
    return v.reshape(1, -1)


def kernel(x_prompt, x_sample, cache_ckv, cache_kpe, state_hgrn, attn_norm_w, w_in, q_norm_w, kv_norm_w,
           w_uq, w_uk, w_uv, hg_lower_bounds, hg_norm_w, w_out, ffn_norm_w, w_gate, w_up, w_down,
           w_router, we_gate, we_up, we_down, final_norm_w):
    depth = w_in.shape[0]
    b_p, l_p, d = x_prompt.shape
    b_s, l_s, _ = x_sample.shape
    past = cache_ckv.shape[2]
    n_p, n_s = b_p * l_p, b_s * l_s
    assert depth == 2 and d == D_MODEL

    probs = jax.nn.softmax(hg_lower_bounds.astype(F32), axis=0)
    lower_bounds = jnp.cumsum(probs, axis=0) - probs[0:1]

    cos_p, sin_p = _rope_tables(jnp.arange(l_p))
    cos_s, sin_s = _rope_tables(past + jnp.arange(l_s))
    cos_s, sin_s = jnp.tile(cos_s, (b_s, 1)), jnp.tile(sin_s, (b_s, 1))
    zero_state = jnp.zeros((b_p, HG_HEADS, HG_DK, HG_DV), F32)

    xp = x_prompt.reshape(n_p, d)
    xs = x_sample.reshape(n_s, d)
    outs = {k: [] for k in ("ckv_p", "kpe_p", "st_p", "ckv_s", "kpe_s", "st_s")}

    for l in range(depth):
        w_in_l = _prep_w_in(w_in[l])
        w_uq_l = _prep_w_uq(w_uq[l])
        w_uk_l = jnp.transpose(w_uk[l], (1, 2, 0)).astype(BF16)
        w_uv_l = jnp.transpose(w_uv[l], (1, 0, 2)).astype(BF16)
        w_out_l = w_out[l].astype(BF16)
        lb_l = _row(lower_bounds[l])
        mixers = []
        for (x, n, batch, seq, cos, sin) in ((xp, n_p, b_p, l_p, cos_p, sin_p),
                                             (xs, n_s, b_s, l_s, cos_s, sin_s)):
            tm = min(TOKEN_TILE, n)
            p = norm_matmul(x, _row(attn_norm_w[l]), w_in_l, tm=tm, tn=IN_PROJ_COLS)
            q, kb, ckv, kpe = mla_prep(p, _row(q_norm_w[l]), _row(kv_norm_w[l]), w_uq_l, w_uk_l,
                                       cos, sin, tm=tm)
            if x is xp:
                o_mla = attention(q, kb, w_uv_l, batch=batch, lq=seq, lk=seq,
                                  tq=min(ATTN_Q_TILE, seq), kb=ATTN_KEY_BLOCK, causal=True)
                o_hg, st = hgrn(p, lb_l, _row(hg_norm_w[l]), zero_state, batch=batch, seq=seq,
                                chunk=CHUNK, tb=min(HG_BLOCK, seq))
            else:
                past_k = jnp.concatenate(
                    [cache_ckv[l].astype(BF16), cache_kpe[l].astype(BF16),
                     jnp.zeros((batch, past, LANE - QK_ROPE), BF16)], axis=-1)
                keys = jnp.concatenate([past_k, kb.reshape(batch, seq, QK_PAD)], axis=1)
                lk = past + seq
                o_mla = attention(q, keys.reshape(batch * lk, QK_PAD), w_uv_l, batch=batch, lq=seq, lk=lk,
                                  tq=seq, kb=lk, causal=False)
                n_seq = math.gcd(batch, 8)
                o_hg, st = hgrn(p, lb_l, _row(hg_norm_w[l]), state_hgrn[l], batch=batch, seq=seq,
                                chunk=seq, tb=n_seq * seq, n_seq=n_seq)
            x1 = out_proj(x, o_mla, o_hg, w_out_l, tm=tm)
            mixers.append((x1, ckv.reshape(batch, seq, KV_LORA), kpe.reshape(batch, seq, QK_ROPE), st))
        (xp, ckv_p, kpe_p, st_p), (xs, ckv_s, kpe_s, st_s) = mixers
        for name, val in (("ckv_p", ckv_p), ("kpe_p", kpe_p), ("st_p", st_p),
                          ("ckv_s", ckv_s), ("kpe_s", kpe_s), ("st_s", st_s)):
            outs[name].append(val)

        i = l // 2
        if l % 2 == 0:
            wg, wu, wd = w_gate[i].astype(BF16), w_up[i].astype(BF16), w_down[i].astype(BF16)
            xp = ffn(xp, _row(ffn_norm_w[l]), wg, wu, wd, tm=min(TOKEN_TILE, n_p), tf=FFN_CHUNK)
            xs = ffn(xs, _row(ffn_norm_w[l]), wg, wu, wd, tm=min(TOKEN_TILE, n_s), tf=FFN_CHUNK)
        else:
            wg, wu, wd = we_gate[i].astype(BF16), we_up[i].astype(BF16), we_down[i].astype(BF16)
            xp, xs = moe_final([xp, xs], _row(ffn_norm_w[l]), w_router[i], wg, wu, wd, _row(final_norm_w))

    return (xp.reshape(b_p, l_p, d), xs.reshape(b_s, l_s, d),
            jnp.stack(outs["ckv_p"]), jnp.stack(outs["kpe_p"]), jnp.stack(outs["st_p"]),
            jnp.stack(outs["ckv_s"]), jnp.stack(outs["kpe_s"]), jnp.stack(outs["st_s"]))
```

```python
import functools
import math

import jax
import jax.numpy as jnp
from jax import lax
from jax.experimental import pallas as pl
from jax.experimental.pallas import tpu as pltpu

F32 = jnp.float32
BF16 = jnp.bfloat16

D_MODEL = 2048
CHUNK = 64
RMS_EPS = 1e-6
NEG_INF = -1e30
LB_FLOOR = 1e-30

MLA_HEADS = 8
V_HEAD = 128
QK_NOPE = 128
QK_ROPE = 64
Q_LORA = 512
KV_LORA = 256
ROPE_THETA = 10000.0
ATTN_SCALE = (QK_NOPE + QK_ROPE) ** -0.5
EXP2_SCALE = ATTN_SCALE * math.log2(math.e)
MLA_WIDTH = MLA_HEADS * V_HEAD

HG_HEADS = 8
HG_DK = 128
HG_DV = 128
HG_KEY = HG_HEADS * HG_DK
HG_WIDTH = HG_HEADS * HG_DV
HG_SUB = 16
HG_SAFE_EXPONENT = 80.0
HG_BLOCK = 4096

N_EXPERTS = 8

LANE = 128
QK_PAD = KV_LORA + LANE

COL_CQ = 0
COL_CKV = Q_LORA
COL_HQ = Q_LORA + KV_LORA
COL_HF = COL_HQ + HG_KEY
COL_HI = COL_HF + HG_KEY
COL_HG = COL_HI + HG_WIDTH
COL_KPE = COL_HG + HG_WIDTH
IN_EXT = COL_KPE + 2 * LANE

VMEM_LIMIT = 56 * 1024 * 1024
TOKEN_TILE = 512
EXPERT_CHUNK = 256
FFN_CHUNK = 512
IN_PROJ_COLS = 1024
ATTN_Q_TILE = 256
ATTN_KEY_BLOCK = 512


def _params(semantics):
    return pltpu.CompilerParams(dimension_semantics=semantics, vmem_limit_bytes=VMEM_LIMIT)


def _rms(x, w):
    return x * lax.rsqrt(jnp.mean(x * x, axis=-1, keepdims=True) + RMS_EPS) * w


def _norm_matmul_kernel(x_ref, nw_ref, w_hbm, o_ref, w_vmem, sem, *, tn):
    @pl.when(pl.program_id(0) == 0)
    def _():
        cp = pltpu.make_async_copy(w_hbm, w_vmem, sem)
        cp.start()
        cp.wait()

    h = _rms(x_ref[...], nw_ref[...]).astype(BF16)
    for c in range(o_ref.shape[1] // tn):
        o_ref[:, c * tn:(c + 1) * tn] = jnp.dot(
            h, w_vmem[:, c * tn:(c + 1) * tn], preferred_element_type=F32).astype(o_ref.dtype)


def norm_matmul(x, nw, w, *, tm, tn):
    n, d = x.shape
    cols = w.shape[1]
    return pl.pallas_call(
        functools.partial(_norm_matmul_kernel, tn=tn),
        grid=(n // tm,),
        in_specs=[pl.BlockSpec((tm, d), lambda i: (i, 0)),
                  pl.BlockSpec((1, d), lambda i: (0, 0)),
                  pl.BlockSpec(memory_space=pl.ANY)],
        out_specs=pl.BlockSpec((tm, cols), lambda i: (i, 0)),
        out_shape=jax.ShapeDtypeStruct((n, cols), BF16),
        scratch_shapes=[pltpu.VMEM((d, cols), BF16), pltpu.SemaphoreType.DMA(())],
        compiler_params=_params(("arbitrary",)),
        name="in_proj",
    )(x, nw, w)


def _mla_prep_kernel(cq_ref, ckv_ref, kp_ref, qnw_ref, kvnw_ref, wuq_ref, wuk_ref, cos_ref, sin_ref,
                     q_ref, kb_ref, ckv_out_ref, kpe_out_ref):
    cos = cos_ref[...]
    sin = sin_ref[...]
    cqn = _rms(cq_ref[...].astype(F32), qnw_ref[...]).astype(BF16)
    q = jnp.dot(cqn, wuq_ref[...], preferred_element_type=F32)
    hw = MLA_HEADS * LANE
    for h in range(MLA_HEADS):
        sl = slice(h * LANE, (h + 1) * LANE)
        q_lat = jnp.dot(q[:, sl].astype(BF16), wuk_ref[h], preferred_element_type=F32)
        rope = q[:, hw + h * LANE:hw + (h + 1) * LANE] * cos + q[:, 2 * hw + h * LANE:2 * hw + (h + 1) * LANE] * sin
        q_ref[h, :, 0:KV_LORA] = q_lat.astype(BF16)
        q_ref[h, :, KV_LORA:QK_PAD] = rope.astype(BF16)
    ckv = _rms(ckv_ref[...].astype(F32), kvnw_ref[...])
    ckv_out_ref[...] = ckv
    kb_ref[:, 0:KV_LORA] = ckv.astype(BF16)
    kp = kp_ref[...].astype(F32)
    kpe = kp[:, 0:LANE] * cos + kp[:, LANE:2 * LANE] * sin
    kpe_out_ref[...] = kpe[:, 0:QK_ROPE]
    kb_ref[:, KV_LORA:QK_PAD] = kpe.astype(BF16)


def mla_prep(p, qnw, kvnw, wuq, wuk, cos, sin, *, tm):
    n = p.shape[0]
    n_pos = cos.shape[0] // tm
    return pl.pallas_call(
        _mla_prep_kernel,
        grid=(n // tm,),
        in_specs=[pl.BlockSpec((tm, Q_LORA), lambda i: (i, COL_CQ // Q_LORA)),
                  pl.BlockSpec((tm, KV_LORA), lambda i: (i, COL_CKV // KV_LORA)),
                  pl.BlockSpec((tm, 2 * LANE), lambda i: (i, COL_KPE // (2 * LANE))),
                  pl.BlockSpec((1, Q_LORA), lambda i: (0, 0)),
                  pl.BlockSpec((1, KV_LORA), lambda i: (0, 0)),
                  pl.BlockSpec(wuq.shape, lambda i: (0, 0)),
                  pl.BlockSpec(wuk.shape, lambda i: (0, 0, 0)),
                  pl.BlockSpec((tm, LANE), lambda i: (i % n_pos, 0)),
                  pl.BlockSpec((tm, LANE), lambda i: (i % n_pos, 0))],
        out_specs=[pl.BlockSpec((MLA_HEADS, tm, QK_PAD), lambda i: (0, i, 0)),
                   pl.BlockSpec((tm, QK_PAD), lambda i: (i, 0)),
                   pl.BlockSpec((tm, KV_LORA), lambda i: (i, 0)),
                   pl.BlockSpec((tm, QK_ROPE), lambda i: (i, 0))],
        out_shape=[jax.ShapeDtypeStruct((MLA_HEADS, n, QK_PAD), BF16),
                   jax.ShapeDtypeStruct((n, QK_PAD), BF16),
                   jax.ShapeDtypeStruct((n, KV_LORA), F32),
                   jax.ShapeDtypeStruct((n, QK_ROPE), F32)],
        compiler_params=_params(("arbitrary",)),
        name="mla_prep",
    )(p, p, p, qnw, kvnw, wuq, wuk, cos, sin)


def _lanes(x, n):
    if n == LANE:
        return x
    if n % LANE == 0:
        return jnp.concatenate([x] * (n // LANE), axis=1)
    return jnp.broadcast_to(x[:, 0:1], (x.shape[0], n))


def _attention_kernel(q_ref, k_ref, wuv_ref, o_ref, m_scr, l_scr, acc_scr, *, tq, kb, causal, lk, n_split):
    rows = MLA_HEADS * tq
    part_rows = rows // n_split
    q = q_ref[...].reshape(rows, QK_PAD)
    m_scr[...] = jnp.full((rows, LANE), NEG_INF, F32)
    l_scr[...] = jnp.zeros((rows, LANE), F32)
    acc_scr[...] = jnp.zeros((rows, KV_LORA), F32)

    def block(start, size, mask):
        k = k_ref[pl.ds(start, size), :]
        v = k[:, 0:KV_LORA]
        scores = [lax.dot_general(q[part * part_rows:(part + 1) * part_rows], k, (((1,), (1,)), ((), ())),
                                  preferred_element_type=F32) for part in range(n_split)]
        for part in range(n_split):
            r = slice(part * part_rows, (part + 1) * part_rows)
            s = scores[part]
            if mask is not None:
                s = jnp.where(mask, s, NEG_INF)
            m_old = m_scr[r, :]
            m_new = jnp.maximum(m_old, jnp.max(s, axis=-1, keepdims=True))
            alpha = jnp.exp2((m_old - m_new) * EXP2_SCALE)
            p = jnp.exp2((s - _lanes(m_new, size)) * EXP2_SCALE)
            l_scr[r, :] = alpha * l_scr[r, :] + jnp.sum(p, axis=-1, keepdims=True)
            acc_scr[r, :] = acc_scr[r, :] * _lanes(alpha, KV_LORA) + jnp.dot(
                p.astype(BF16), v, preferred_element_type=F32)
            m_scr[r, :] = m_new

    if causal:
        q_start = pl.program_id(1) * tq
        n_wide = q_start // (2 * kb)

        def wide_body(j, carry):
            block(pl.multiple_of(j * 2 * kb, 2 * kb), 2 * kb, None)
            return carry

        lax.fori_loop(0, n_wide, wide_body, 0)
        n_main = q_start // kb

        @pl.when(n_main > 2 * n_wide)
        def _():
            block(pl.multiple_of(n_wide * 2 * kb, 2 * kb), kb, None)

        n_before = (q_start - n_main * kb) // tq
        for r in range(kb // tq):
            width = (r + 1) * tq

            @pl.when(n_before == r)
            def _(r=r, width=width):
                tok = jnp.bitwise_and(lax.broadcasted_iota(jnp.int32, (part_rows, width), 0), tq - 1)
                col = lax.broadcasted_iota(jnp.int32, (part_rows, width), 1)
                block(pl.multiple_of(n_main * kb, kb), width, col < r * tq + (tok // CHUNK + 1) * CHUNK)
    else:
        for j in range(lk // kb):
            block(j * kb, kb, None)

    o = (acc_scr[...] / _lanes(l_scr[...], KV_LORA)).astype(BF16)
    for h in range(MLA_HEADS):
        o_ref[:, h * V_HEAD:(h + 1) * V_HEAD] = jnp.dot(
            o[h * tq:(h + 1) * tq], wuv_ref[h], preferred_element_type=F32).astype(o_ref.dtype)


def attention(q, k, wuv, *, batch, lq, lk, tq, kb, causal):
    nq = lq // tq
    rows = MLA_HEADS * tq
    kern = functools.partial(_attention_kernel, tq=tq, kb=kb, causal=causal, lk=lk,
                             n_split=4 if causal else 1)
    return pl.pallas_call(
        kern,
        grid=(batch, nq),
        in_specs=[pl.BlockSpec((MLA_HEADS, tq, QK_PAD), lambda b, i: (0, b * nq + i, 0)),
                  pl.BlockSpec((lk, QK_PAD), lambda b, i: (b, 0)),
                  pl.BlockSpec(wuv.shape, lambda b, i: (0, 0, 0))],
        out_specs=pl.BlockSpec((tq, MLA_WIDTH), lambda b, i: (b * nq + i, 0)),
        out_shape=jax.ShapeDtypeStruct((batch * lq, MLA_WIDTH), BF16),
        scratch_shapes=[pltpu.VMEM((rows, LANE), F32), pltpu.VMEM((rows, LANE), F32),
                        pltpu.VMEM((rows, KV_LORA), F32)],
        compiler_params=_params(("arbitrary", "arbitrary")),
        name="attention_causal" if causal else "attention_full",
    )(q, k, wuv)


def _sigmoid(x):
    return 1.0 / (1.0 + jnp.exp(-x))


def _hgrn_kernel(hq_ref, hf_ref, hi_ref, hg_ref, lb_ref, nw_ref, s0_ref, o_ref, sfin_ref,
                 st_scr, oacc_scr, g_scr, q_scr, k_scr, *, chunk, n_chunks, n_seq):
    t = pl.program_id(2)

    @pl.when(t == 0)
    def _():
        for sq in range(n_seq):
            st_scr[sq] = s0_ref[sq, 0].T

    lb = lb_ref[...]
    lb_floor = jnp.maximum(lb, LB_FLOOR)
    one_m_lb = 1.0 - lb
    tri = (lax.broadcasted_iota(jnp.int32, (chunk, chunk), 0)
           >= lax.broadcasted_iota(jnp.int32, (chunk, chunk), 1))
    tri_bf16 = jnp.where(tri, 1.0, 0.0).astype(BF16)
    ones = jnp.ones((HG_DK, HG_DV), BF16)
    row_id = lax.broadcasted_iota(jnp.int32, (chunk, HG_DK), 0)
    n_sub = chunk // HG_SUB

    hf = hf_ref[...].astype(F32)
    e = jnp.exp(-jnp.abs(hf))
    big = 1.0 / (1.0 + e)
    small = e * big
    pos = hf >= 0.0
    log_f = jnp.log(lb_floor + one_m_lb * jnp.where(pos, big, small))
    k_scr[...] = one_m_lb * jnp.where(pos, small, big)
    hq = hq_ref[...].astype(F32)
    q_scr[...] = hq * _sigmoid(hq)
    lf_a = log_f.astype(BF16)
    rest = log_f - lf_a.astype(F32)
    lf_b = rest.astype(BF16)
    lf_c = (rest - lf_b.astype(F32)).astype(BF16)
    g_min = None
    for c in range(n_seq * n_chunks):
        sl = slice(c * chunk, (c + 1) * chunk)
        parts = jnp.dot(tri_bf16, jnp.concatenate([lf_a[sl], lf_b[sl], lf_c[sl]], axis=1),
                        preferred_element_type=F32)
        g = parts[:, 0:HG_DK] + parts[:, HG_DK:2 * HG_DK] + parts[:, 2 * HG_DK:3 * HG_DK]
        g_scr[sl, :] = g
        g_end = g[chunk - 1:chunk]
        g_min = g_end if g_min is None else jnp.minimum(g_min, g_end)
    factorable = jnp.min(g_min) >= -HG_SAFE_EXPONENT

    def state_step(c, g, k, v16):
        st = st_scr.at[c // n_chunks]
        g_last = g[chunk - 1:chunk]
        kh = k * jnp.exp(g_last - g)
        st[...] = st[...] * jnp.exp(g_last) + lax.dot_general(
            v16, kh.astype(BF16), (((0,), (0,)), ((), ())), preferred_element_type=F32)

    def chunk_factored(c):
        sl = slice(c * chunk, (c + 1) * chunk)
        g, q, k = g_scr[sl, :], q_scr[sl, :], k_scr[sl, :]
        v16 = hi_ref[sl, :]
        qd = (q * jnp.exp(g)).astype(BF16)
        kd = (k * jnp.exp(-g)).astype(BF16)
        a = lax.dot_general(qd, kd, (((1,), (1,)), ((), ())), preferred_element_type=F32)
        a = jnp.where(tri, a, 0.0).astype(BF16)
        oacc_scr[sl, :] = (lax.dot_general(qd, st_scr[c // n_chunks].astype(BF16), (((1,), (1,)), ((), ())),
                                           preferred_element_type=F32)
                           + jnp.dot(a, v16, preferred_element_type=F32))
        state_step(c, g, k, v16)

    def chunk_exact(c):
        sl = slice(c * chunk, (c + 1) * chunk)
        oacc = oacc_scr.at[sl, :]
        g, q, k = g_scr[sl, :], q_scr[sl, :], k_scr[sl, :]
        v16 = hi_ref[sl, :]
        v = v16.astype(F32)

        oacc[...] = lax.dot_general((q * jnp.exp(g)).astype(BF16), st_scr[c // n_chunks].astype(BF16),
                                    (((1,), (1,)), ((), ())), preferred_element_type=F32)
        for i in range(1, n_sub):
            r = i * HG_SUB
            g_edge = g[r - 1:r]
            qt = q[r:r + HG_SUB] * jnp.exp(g[r:r + HG_SUB] - g_edge)
            kt = k[0:r] * jnp.exp(g_edge - g[0:r])
            a = lax.dot_general(qt.astype(BF16), kt.astype(BF16), (((1,), (1,)), ((), ())),
                                preferred_element_type=F32)
            oacc[r:r + HG_SUB, :] += jnp.dot(a.astype(BF16), v16[0:r], preferred_element_type=F32)
        for grp in range(chunk // 8):
            r0 = grp * 8
            r1 = (r0 // HG_SUB + 1) * HG_SUB
            n = r1 - r0
            parts = []
            for s in range(r0, r0 + 8):
                d = jnp.where(row_id[r0:r1] >= s, g[r0:r1] - g[s:s + 1], NEG_INF)
                parts.append(jnp.exp(d) * q[r0:r1] * k[s:s + 1])
            sums = jnp.dot(jnp.concatenate(parts, axis=0).astype(BF16), ones, preferred_element_type=F32)
            upd = sums[0:n] * v[r0:r0 + 1]
            for u in range(1, 8):
                upd = upd + sums[u * n:(u + 1) * n] * v[r0 + u:r0 + u + 1]
            oacc[r0:r1, :] += upd
        state_step(c, g, k, v16)

    @pl.when(factorable)
    def _():
        for c in range(n_seq * n_chunks):
            chunk_factored(c)

    @pl.when(jnp.logical_not(factorable))
    def _():
        for c in range(n_seq * n_chunks):
            chunk_exact(c)

    hg = hg_ref[...].astype(F32)
    o_ref[...] = (_rms(oacc_scr[...], nw_ref[...]) * (hg * _sigmoid(hg))).astype(o_ref.dtype)

    @pl.when(t == pl.num_programs(2) - 1)
    def _():
        for sq in range(n_seq):
            sfin_ref[sq, 0] = st_scr[sq].T


def hgrn(p, lb, nw, s0, *, batch, seq, chunk, tb, n_seq=1):
    nt = max(seq // tb, 1)
    assert tb == n_seq * seq or (n_seq == 1 and seq % tb == 0)
    kern = functools.partial(_hgrn_kernel, chunk=chunk, n_chunks=tb // (n_seq * chunk), n_seq=n_seq)

    def col(base):
        return lambda b, h, t: (b * nt + t, base // LANE + h)

    return pl.pallas_call(
        kern,
        grid=(batch // n_seq, HG_HEADS, nt),
        in_specs=[pl.BlockSpec((tb, LANE), col(COL_HQ)),
                  pl.BlockSpec((tb, LANE), col(COL_HF)),
                  pl.BlockSpec((tb, LANE), col(COL_HI)),
                  pl.BlockSpec((tb, LANE), col(COL_HG)),
                  pl.BlockSpec((1, HG_DK), lambda b, h, t: (0, h)),
                  pl.BlockSpec((1, HG_DV), lambda b, h, t: (0, 0)),
                  pl.BlockSpec((n_seq, 1, HG_DK, HG_DV), lambda b, h, t: (b, h, 0, 0))],
        out_specs=[pl.BlockSpec((tb, HG_DV), lambda b, h, t: (b * nt + t, h)),
                   pl.BlockSpec((n_seq, 1, HG_DK, HG_DV), lambda b, h, t: (b, h, 0, 0))],
        out_shape=[jax.ShapeDtypeStruct((batch * seq, HG_WIDTH), BF16),
                   jax.ShapeDtypeStruct((batch, HG_HEADS, HG_DK, HG_DV), F32)],
        scratch_shapes=[pltpu.VMEM((n_seq, HG_DV, HG_DK), F32), pltpu.VMEM((tb, HG_DV), F32),
                        pltpu.VMEM((tb, HG_DK), F32), pltpu.VMEM((tb, HG_DK), F32),
                        pltpu.VMEM((tb, HG_DK), F32)],
        compiler_params=_params(("arbitrary", "arbitrary", "arbitrary")),
        name="hgrn",
    )(p, p, p, p, lb, nw, s0)


def _out_proj_kernel(x_ref, a_ref, b_ref, w_ref, o_ref):
    o_ref[...] = (x_ref[...]
                  + jnp.dot(a_ref[...], w_ref[0:MLA_WIDTH, :], preferred_element_type=F32)
                  + jnp.dot(b_ref[...], w_ref[MLA_WIDTH:MLA_WIDTH + HG_WIDTH, :], preferred_element_type=F32))


def out_proj(x, a, b, w, *, tm):
    n, d = x.shape
    return pl.pallas_call(
        _out_proj_kernel,
        grid=(n // tm,),
        in_specs=[pl.BlockSpec((tm, d), lambda i: (i, 0)),
                  pl.BlockSpec((tm, MLA_WIDTH), lambda i: (i, 0)),
                  pl.BlockSpec((tm, HG_WIDTH), lambda i: (i, 0)),
                  pl.BlockSpec(w.shape, lambda i: (0, 0))],
        out_specs=pl.BlockSpec((tm, d), lambda i: (i, 0)),
        out_shape=jax.ShapeDtypeStruct((n, d), F32),
        compiler_params=_params(("arbitrary",)),
        name="out_proj",
    )(x, a, b, w)


def _silu(x):
    return x * _sigmoid(x)


def _swiglu_chunk(h, wg, wu, wd):
    a = _silu(jnp.dot(h, wg, preferred_element_type=F32)) * jnp.dot(h, wu, preferred_element_type=F32)
    return jnp.dot(a.astype(BF16), wd, preferred_element_type=F32)


def _ffn_kernel(x_ref, nw_ref, wg_hbm, wu_hbm, wd_hbm, o_ref, h_scr, acc_scr, wg_buf, wu_buf, wd_buf, sem, *, nf):
    i = pl.program_id(0)
    n_tiles = pl.num_programs(0)

    tf = wd_buf.shape[1]

    def chunk_copies(j, slot):
        cols = pl.ds(pl.multiple_of(j * tf, tf), tf)
        return (pltpu.make_async_copy(wg_hbm.at[:, cols], wg_buf.at[slot], sem.at[0, slot]),
                pltpu.make_async_copy(wu_hbm.at[:, cols], wu_buf.at[slot], sem.at[1, slot]),
                pltpu.make_async_copy(wd_hbm.at[cols, :], wd_buf.at[slot], sem.at[2, slot]))

    @pl.when(i == 0)
    def _():
        for cp in chunk_copies(0, 0):
            cp.start()

    h_scr[...] = _rms(x_ref[...], nw_ref[...]).astype(BF16)
    acc_scr[...] = jnp.zeros_like(acc_scr)

    def body(j, carry):
        slot = (i * nf + j) % 2

        @pl.when(j + 1 < nf)
        def _():
            for cp in chunk_copies(j + 1, 1 - slot):
                cp.start()

        @pl.when((j + 1 == nf) & (i + 1 < n_tiles))
        def _():
            for cp in chunk_copies(0, 1 - slot):
                cp.start()

        for cp in chunk_copies(j, slot):
            cp.wait()
        acc_scr[...] += _swiglu_chunk(h_scr[...], wg_buf[slot], wu_buf[slot], wd_buf[slot])
        return carry

    lax.fori_loop(0, nf, body, 0)
    o_ref[...] = x_ref[...] + acc_scr[...]


def ffn(x, nw, wg, wu, wd, *, tm, tf):
    n, d = x.shape
    nf = wg.shape[1] // tf
    any_spec = pl.BlockSpec(memory_space=pl.ANY)
    return pl.pallas_call(
        functools.partial(_ffn_kernel, nf=nf),
        grid=(n // tm,),
        in_specs=[pl.BlockSpec((tm, d), lambda i: (i, 0)),
                  pl.BlockSpec((1, d), lambda i: (0, 0)), any_spec, any_spec, any_spec],
        out_specs=pl.BlockSpec((tm, d), lambda i: (i, 0)),
        out_shape=jax.ShapeDtypeStruct((n, d), F32),
        scratch_shapes=[pltpu.VMEM((tm, d), BF16), pltpu.VMEM((tm, d), F32),
                        pltpu.VMEM((2, d, tf), BF16), pltpu.VMEM((2, d, tf), BF16),
                        pltpu.VMEM((2, tf, d), BF16), pltpu.SemaphoreType.DMA((3, 2))],
        compiler_params=_params(("arbitrary",)),
        name="ffn",
    )(x, nw, wg, wu, wd)


def _moe_route_kernel(x_ref, nw_ref, wr_ref, hp_ref, e1_ref, e2_ref, w1_ref, w2_ref, r1_ref, r2_ref,
                      cnt_ref, cnt_scr):
    @pl.when(pl.program_id(0) == 0)
    def _():
        cnt_scr[...] = jnp.zeros_like(cnt_scr)

    tm = x_ref.shape[0]
    half = x_ref.shape[1] // 2
    h = _rms(x_ref[...], nw_ref[...])
    hb = h.astype(BF16)
    h_lo = (h - hb.astype(F32)).astype(BF16)
    wr = wr_ref[...]
    w_hi = wr.astype(BF16)
    w_lo = (wr - w_hi.astype(F32)).astype(BF16)
    logits = jnp.dot(hb, w_hi, preferred_element_type=F32) + (
        jnp.dot(hb, w_lo, preferred_element_type=F32) + jnp.dot(h_lo, w_hi, preferred_element_type=F32))
    ids = lax.broadcasted_iota(jnp.int32, logits.shape, 1).astype(F32)
    m1 = jnp.max(logits, axis=-1, keepdims=True)
    i1 = jnp.min(jnp.where(logits == m1, ids, float(N_EXPERTS)), axis=-1, keepdims=True)
    rest = jnp.where(ids == i1, -jnp.inf, logits)
    m2 = jnp.max(rest, axis=-1, keepdims=True)
    i2 = jnp.min(jnp.where(rest == m2, ids, float(N_EXPERTS)), axis=-1, keepdims=True)
    t = jnp.exp(m2 - m1)
    e1_ref[...] = i1.astype(jnp.int32)
    e2_ref[...] = i2.astype(jnp.int32)
    w1_ref[...] = 1.0 / (1.0 + t)
    w2_ref[...] = t / (1.0 + t)
    oh1 = (ids == i1).astype(F32)
    oh2 = (ids == i2).astype(F32)
    both = oh1 + oh2
    before = (lax.broadcasted_iota(jnp.int32, (tm, tm), 0)
              > lax.broadcasted_iota(jnp.int32, (tm, tm), 1)).astype(BF16)
    prefix = jnp.dot(before, both.astype(BF16), preferred_element_type=F32) + cnt_scr[...]
    r1_ref[...] = jnp.sum(prefix * oh1, axis=-1, keepdims=True).astype(jnp.int32)
    r2_ref[...] = jnp.sum(prefix * oh2, axis=-1, keepdims=True).astype(jnp.int32)
    cnt_scr[...] += jnp.sum(both, axis=0, keepdims=True)
    cnt_ref[...] = cnt_scr[...].astype(jnp.int32)
    hb32 = hb.astype(F32)
    hi = pltpu.bitcast(hb32[:, 0:half], jnp.uint32)
    lo = pltpu.bitcast(hb32[:, half:2 * half], jnp.uint32)
    hp_ref[...] = hi | (lo >> 16)


def moe_route(x, nw, wr, *, tm):
    n, d = x.shape
    col = lambda dt: jax.ShapeDtypeStruct((n, 1), dt)
    col_spec = pl.BlockSpec((tm, 1), lambda i: (i, 0))
    return pl.pallas_call(
        _moe_route_kernel,
        grid=(n // tm,),
        in_specs=[pl.BlockSpec((tm, d), lambda i: (i, 0)),
                  pl.BlockSpec((1, d), lambda i: (0, 0)),
                  pl.BlockSpec(wr.shape, lambda i: (0, 0))],
        out_specs=[pl.BlockSpec((tm, d // 2), lambda i: (i, 0))] + [col_spec] * 6
                  + [pl.BlockSpec((1, N_EXPERTS), lambda i: (0, 0))],
        out_shape=[jax.ShapeDtypeStruct((n, d // 2), jnp.uint32), col(jnp.int32), col(jnp.int32),
                   col(F32), col(F32), col(jnp.int32), col(jnp.int32),
                   jax.ShapeDtypeStruct((1, N_EXPERTS), jnp.int32)],
        scratch_shapes=[pltpu.VMEM((1, N_EXPERTS), F32)],
        compiler_params=_params(("arbitrary",)),
        name="moe_route",
    )(x, nw, wr)


def _row_copy(src, src_row, dst, dst_row, sem):
    return pltpu.make_async_copy(src.at[pl.ds(src_row, 1), :], dst.at[pl.ds(dst_row, 1), :], sem)


def _moe_scatter_kernel(d1_ref, d2_ref, hp_ref, xs_in_hbm, xs_hbm, sem, *, tm):
    del xs_in_hbm

    def issue(r, carry):
        _row_copy(hp_ref, r, xs_hbm, d1_ref[0, 0, r], sem).start()
        _row_copy(hp_ref, r, xs_hbm, d2_ref[0, 0, r], sem).start()
        return carry

    lax.fori_loop(0, tm, issue, 0, unroll=8)

    def drain(r, carry):
        _row_copy(hp_ref, r, xs_hbm, d1_ref[0, 0, r], sem).wait()
        _row_copy(hp_ref, r, xs_hbm, d2_ref[0, 0, r], sem).wait()
        return carry

    lax.fori_loop(0, tm, drain, 0, unroll=8)


def moe_scatter(hp, d1, d2, xs, *, tm):
    n = hp.shape[0]
    idx_spec = pl.BlockSpec((1, 1, tm), lambda i: (i, 0, 0), memory_space=pltpu.SMEM)
    any_spec = pl.BlockSpec(memory_space=pl.ANY)
    return pl.pallas_call(
        functools.partial(_moe_scatter_kernel, tm=tm),
        grid=(n // tm,),
        in_specs=[idx_spec, idx_spec, pl.BlockSpec((tm, hp.shape[1]), lambda i: (i, 0)), any_spec],
        out_specs=any_spec,
        out_shape=jax.ShapeDtypeStruct(xs.shape, xs.dtype),
        scratch_shapes=[pltpu.SemaphoreType.DMA(())],
        input_output_aliases={3: 0},
        compiler_params=_params(("arbitrary",)),
        name="moe_scatter",
    )(d1.reshape(n // tm, 1, tm), d2.reshape(n // tm, 1, tm), hp, xs)


def _moe_expert_kernel(te_ref, na_ref, xs_ref, wg_hbm, wu_hbm, wd_hbm, ys_ref, h_scr, wg_buf, wu_buf, wd_buf, sem,
                       *, nf):
    i = pl.program_id(0)
    n_act = na_ref[0]
    active = i < n_act
    half = xs_ref.shape[1]

    tf = wd_buf.shape[1]

    def chunk_copies(tile, j, slot):
        e = te_ref[tile]
        cols = pl.ds(pl.multiple_of(j * tf, tf), tf)
        return (pltpu.make_async_copy(wg_hbm.at[e, :, cols], wg_buf.at[slot], sem.at[0, slot]),
                pltpu.make_async_copy(wu_hbm.at[e, :, cols], wu_buf.at[slot], sem.at[1, slot]),
                pltpu.make_async_copy(wd_hbm.at[e, cols, :], wd_buf.at[slot], sem.at[2, slot]))

    @pl.when(active)
    def _():
        @pl.when(i == 0)
        def _():
            for cp in chunk_copies(0, 0, 0):
                cp.start()

        packed = xs_ref[...]
        h_scr[:, 0:half] = pltpu.bitcast(packed & jnp.uint32(0xFFFF0000), F32).astype(BF16)
        h_scr[:, half:2 * half] = pltpu.bitcast(packed << 16, F32).astype(BF16)
        ys_ref[...] = jnp.zeros_like(ys_ref)

        def body(j, carry):
            slot = (i * nf + j) % 2

            @pl.when(j + 1 < nf)
            def _():
                for cp in chunk_copies(i, j + 1, 1 - slot):
                    cp.start()

            @pl.when((j + 1 == nf) & (i + 1 < n_act))
            def _():
                for cp in chunk_copies(i + 1, 0, 1 - slot):
                    cp.start()

            for cp in chunk_copies(i, j, slot):
                cp.wait()
            ys_ref[...] += _swiglu_chunk(h_scr[...], wg_buf[slot], wu_buf[slot], wd_buf[slot])
            return carry

        lax.fori_loop(0, nf, body, 0)

    @pl.when(jnp.logical_not(active))
    def _():
        ys_ref[...] = jnp.zeros_like(ys_ref)


def moe_experts(xs, tile_expert, n_active, wg, wu, wd, *, tm, tf):
    rows, half = xs.shape
    d = 2 * half
    nf = wg.shape[2] // tf
    any_spec = pl.BlockSpec(memory_space=pl.ANY)
    return pl.pallas_call(
        functools.partial(_moe_expert_kernel, nf=nf),
        grid_spec=pltpu.PrefetchScalarGridSpec(
            num_scalar_prefetch=2,
            grid=(rows // tm,),
            in_specs=[pl.BlockSpec((tm, half), lambda i, te, na: (i, 0)), any_spec, any_spec, any_spec],
            out_specs=pl.BlockSpec((tm, d), lambda i, te, na: (i, 0)),
            scratch_shapes=[pltpu.VMEM((tm, d), BF16),
                            pltpu.VMEM((2, d, tf), BF16), pltpu.VMEM((2, d, tf), BF16),
                            pltpu.VMEM((2, tf, d), BF16), pltpu.SemaphoreType.DMA((3, 2))]),
        out_shape=jax.ShapeDtypeStruct((rows, d), F32),
        compiler_params=_params(("arbitrary",)),
        name="moe_experts",
    )(tile_expert, n_active, xs, wg, wu, wd)


def _moe_combine_kernel(d1_ref, d2_ref, d1_next_ref, d2_next_ref, x_ref, w1_ref, w2_ref, fw_ref, ys_hbm, o_ref,
                        a_scr, b_scr, sem, *, tm):
    i = pl.program_id(0)
    slot = i % 2

    def gather(r1_ref, r2_ref, s, wait):
        def row(r, carry):
            for src_ref, dst in ((r1_ref, a_scr), (r2_ref, b_scr)):
                cp = _row_copy(ys_hbm, src_ref[0, 0, r], dst.at[s], r, sem.at[s])
                cp.wait() if wait else cp.start()
            return carry

        lax.fori_loop(0, tm, row, 0, unroll=8)

    @pl.when(i == 0)
    def _():
        gather(d1_ref, d2_ref, 0, wait=False)

    @pl.when(i + 1 < pl.num_programs(0))
    def _():
        gather(d1_next_ref, d2_next_ref, 1 - slot, wait=False)

    gather(d1_ref, d2_ref, slot, wait=True)
    y = x_ref[...] + (w1_ref[...] * a_scr[slot] + w2_ref[...] * b_scr[slot])
    o_ref[...] = _rms(y, fw_ref[...])


def moe_combine(x, ys, d1, d2, w1, w2, fw, *, tm):
    n, d = x.shape
    nt = n // tm
    idx_spec = pl.BlockSpec((1, 1, tm), lambda i: (i, 0, 0), memory_space=pltpu.SMEM)
    next_spec = pl.BlockSpec((1, 1, tm), lambda i: (jnp.minimum(i + 1, nt - 1), 0, 0), memory_space=pltpu.SMEM)
    col_spec = pl.BlockSpec((tm, 1), lambda i: (i, 0))
    d1, d2 = d1.reshape(nt, 1, tm), d2.reshape(nt, 1, tm)
    return pl.pallas_call(
        functools.partial(_moe_combine_kernel, tm=tm),
        grid=(nt,),
        in_specs=[idx_spec, idx_spec, next_spec, next_spec, pl.BlockSpec((tm, d), lambda i: (i, 0)),
                  col_spec, col_spec, pl.BlockSpec((1, d), lambda i: (0, 0)), pl.BlockSpec(memory_space=pl.ANY)],
        out_specs=pl.BlockSpec((tm, d), lambda i: (i, 0)),
        out_shape=jax.ShapeDtypeStruct((n, d), F32),
        scratch_shapes=[pltpu.VMEM((2, tm, d), F32), pltpu.VMEM((2, tm, d), F32),
                        pltpu.SemaphoreType.DMA((2,))],
        compiler_params=_params(("arbitrary",)),
        name="moe_combine",
    )(d1, d2, d1, d2, x, w1, w2, fw, ys)


def moe_final(streams, nw, wr, wg, wu, wd, fw):
    n_exp = wg.shape[0]
    d = streams[0].shape[1]
    tms = [min(TOKEN_TILE, x.shape[0]) for x in streams]
    tile = TOKEN_TILE
    routed = [moe_route(x, nw, wr, tm=tm) for x, tm in zip(streams, tms)]
    counts = [r[7][0] for r in routed]
    total = sum(counts)
    padded = (total + tile - 1) // tile * tile
    ends = jnp.cumsum(padded)
    starts = ends - padded
    n_rows = sum(2 * x.shape[0] for x in streams) + n_exp * tile
    n_tiles = n_rows // tile
    tile_expert = jnp.minimum(
        jnp.sum(jnp.arange(n_tiles, dtype=jnp.int32)[:, None] * tile >= ends[None, :], axis=1), n_exp - 1
    ).astype(jnp.int32)
    n_active = (ends[n_exp - 1:] // tile).astype(jnp.int32)

    def slot_rows(e, rank, offset):
        table = starts + offset
        return (jnp.sum(jnp.where(e == jnp.arange(n_exp)[None, :], table[None, :], 0), axis=1, keepdims=True)
                + rank).astype(jnp.int32)

    xs = jnp.zeros((n_rows, d // 2), jnp.uint32)
    dests = []
    offset = jnp.zeros((n_exp,), jnp.int32)
    for (hp, e1, e2, w1, w2, r1, r2, cnt), tm in zip(routed, tms):
        d1, d2 = slot_rows(e1, r1, offset), slot_rows(e2, r2, offset)
        xs = moe_scatter(hp, d1, d2, xs, tm=tm)
        dests.append((d1, d2))
        offset = offset + cnt[0]
    ys = moe_experts(xs, tile_expert, n_active, wg, wu, wd, tm=tile, tf=EXPERT_CHUNK)
    return [moe_combine(x, ys, d1, d2, r[3], r[4], fw, tm=tm)
            for x, r, (d1, d2), tm in zip(streams, routed, dests, tms)]


def _swap_halves(t):
    half = t.shape[-1] // 2
    return jnp.concatenate([t[..., half:], t[..., :half]], axis=-1)


def _prep_w_in(w):
    d = w.shape[0]
    k_pe = w[:, Q_LORA + KV_LORA:Q_LORA + KV_LORA + QK_ROPE]
    pad = jnp.zeros((d, LANE - QK_ROPE), w.dtype)
    return jnp.concatenate([w[:, :Q_LORA + KV_LORA], w[:, Q_LORA + KV_LORA + QK_ROPE:],
                            k_pe, pad, _swap_halves(k_pe), pad], axis=1).astype(BF16)


def _prep_w_uq(w):
    w = w.reshape(Q_LORA, MLA_HEADS, QK_NOPE + QK_ROPE)
    nope = w[..., :QK_NOPE]
    rope = w[..., QK_NOPE:]
    pad = jnp.zeros((Q_LORA, MLA_HEADS, LANE - QK_ROPE), w.dtype)
    parts = [nope, jnp.concatenate([rope, pad], -1), jnp.concatenate([_swap_halves(rope), pad], -1)]
    return jnp.concatenate([t.reshape(Q_LORA, MLA_HEADS * LANE) for t in parts], axis=1).astype(BF16)


def _rope_tables(pos):
    half = QK_ROPE // 2
    inv_freq = jnp.exp(jnp.arange(half, dtype=F32) * (-math.log(ROPE_THETA) / half))
    ang = pos.astype(F32)[:, None] * inv_freq[None, :]
    cos, sin = jnp.cos(ang), jnp.sin(ang)
    pad = jnp.zeros((pos.shape[0], LANE - QK_ROPE), F32)
    return jnp.concatenate([cos, cos, pad], -1), jnp.concatenate([-sin, sin, pad], -1)


def _row(v):
    return v.reshape(1, -1)


def kernel(x_prompt, x_sample, cache_ckv, cache_kpe, state_hgrn, attn_norm_w, w_in, q_norm_w, kv_norm_w,
           w_uq, w_uk, w_uv, hg_lower_bounds, hg_norm_w, w_out, ffn_norm_w, w_gate, w_up, w_down,
           w_router, we_gate, we_up, we_down, final_norm_w):
    depth = w_in.shape[0]
    b_p, l_p, d = x_prompt.shape
    b_s, l_s, _ = x_sample.shape
    past = cache_ckv.shape[2]
    n_p, n_s = b_p * l_p, b_s * l_s
    assert depth == 2 and d == D_MODEL

    probs = jax.nn.softmax(hg_lower_bounds.astype(F32), axis=0)
    lower_bounds = jnp.cumsum(probs, axis=0) - probs[0:1]

    cos_p, sin_p = _rope_tables(jnp.arange(l_p))
    cos_s, sin_s = _rope_tables(past + jnp.arange(l_s))
    cos_s, sin_s = jnp.tile(cos_s, (b_s, 1)), jnp.tile(sin_s, (b_s, 1))
    zero_state = jnp.zeros((b_p, HG_HEADS, HG_DK, HG_DV), F32)

    xp = x_prompt.reshape(n_p, d)
    xs = x_sample.reshape(n_s, d)
    outs = {k: [] for k in ("ckv_p", "kpe_p", "st_p", "ckv_s", "kpe_s", "st_s")}

    for l in range(depth):
        w_in_l = _prep_w_in(w_in[l])
        w_uq_l = _prep_w_uq(w_uq[l])
        w_uk_l = jnp.transpose(w_uk[l], (1, 2, 0)).astype(BF16)
        w_uv_l = jnp.transpose(w_uv[l], (1, 0, 2)).astype(BF16)
        w_out_l = w_out[l].astype(BF16)
        lb_l = _row(lower_bounds[l])
        mixers = []
        for (x, n, batch, seq, cos, sin) in ((xp, n_p, b_p, l_p, cos_p, sin_p),
                                             (xs, n_s, b_s, l_s, cos_s, sin_s)):
            tm = min(TOKEN_TILE, n)
            p = norm_matmul(x, _row(attn_norm_w[l]), w_in_l, tm=tm, tn=IN_PROJ_COLS)
            q, kb, ckv, kpe = mla_prep(p, _row(q_norm_w[l]), _row(kv_norm_w[l]), w_uq_l, w_uk_l,
                                       cos, sin, tm=tm)
            if x is xp:
                o_mla = attention(q, kb, w_uv_l, batch=batch, lq=seq, lk=seq,
                                  tq=min(ATTN_Q_TILE, seq), kb=ATTN_KEY_BLOCK, causal=True)
                o_hg, st = hgrn(p, lb_l, _row(hg_norm_w[l]), zero_state, batch=batch, seq=seq,
                                chunk=CHUNK, tb=min(HG_BLOCK, seq))
            else:
                past_k = jnp.concatenate(
                    [cache_ckv[l].astype(BF16), cache_kpe[l].astype(BF16),
                     jnp.zeros((batch, past, LANE - QK_ROPE), BF16)], axis=-1)
                keys = jnp.concatenate([past_k, kb.reshape(batch, seq, QK_PAD)], axis=1)
                lk = past + seq
                o_mla = attention(q, keys.reshape(batch * lk, QK_PAD), w_uv_l, batch=batch, lq=seq, lk=lk,
                                  tq=seq, kb=lk, causal=False)
                n_seq = math.gcd(batch, 8)
                o_hg, st = hgrn(p, lb_l, _row(hg_norm_w[l]), state_hgrn[l], batch=batch, seq=seq,
                                chunk=seq, tb=n_seq * seq, n_seq=n_seq)
            x1 = out_proj(x, o_mla, o_hg, w_out_l, tm=tm)
            mixers.append((x1, ckv.reshape(batch, seq, KV_LORA), kpe.reshape(batch, seq, QK_ROPE), st))
        (xp, ckv_p, kpe_p, st_p), (xs, ckv_s, kpe_s, st_s) = mixers
        for name, val in (("ckv_p", ckv_p), ("kpe_p", kpe_p), ("st_p", st_p),
                          ("ckv_s", ckv_s), ("kpe_s", kpe_s), ("st_s", st_s)):
            outs[name].append(val)

        i = l // 2
        if l % 2 == 0:
            wg, wu, wd = w_gate[i].astype(BF16), w_up[i].astype(BF16), w_down[i].astype(BF16)
            xp = ffn(xp, _row(ffn_norm_w[l]), wg, wu, wd, tm=min(TOKEN_TILE, n_p), tf=FFN_CHUNK)
            xs = ffn(xs, _row(ffn_norm_w[l]), wg, wu, wd, tm=min(TOKEN_TILE, n_s), tf=FFN_CHUNK)
        else:
            wg, wu, wd = we_gate[i].astype(BF16), we_up[i].astype(BF16), we_down[i].astype(BF16)
            xp, xs = moe_final([xp, xs], _row(ffn_norm_w[l]), w_router[i], wg, wu, wd, _row(final_norm_w))

    return (xp.reshape(b_p, l_p, d), xs.reshape(b_s, l_s, d),
            jnp.stack(outs["ckv_p"]), jnp.stack(outs["kpe_p"]), jnp.stack(outs["st_p"]),
            jnp.stack(outs["ckv_s"]), jnp.stack(outs["kpe_s"]), jnp.stack(outs["st_s"]))
```

```python
import functools
import math

import jax
import jax.numpy as jnp
from jax import lax
from jax.experimental import pallas as pl
from jax.experimental.pallas import tpu as pltpu

F32 = jnp.float32
BF16 = jnp.bfloat16

D_MODEL = 2048
CHUNK = 64
RMS_EPS = 1e-6
NEG_INF = -1e30
LB_FLOOR = 1e-30

MLA_HEADS = 8
V_HEAD = 128
QK_NOPE = 128
QK_ROPE = 64
Q_LORA = 512
KV_LORA = 256
ROPE_THETA = 10000.0
ATTN_SCALE = (QK_NOPE + QK_ROPE) ** -0.5
EXP2_SCALE = ATTN_SCALE * math.log2(math.e)
MLA_WIDTH = MLA_HEADS * V_HEAD

HG_HEADS = 8
HG_DK = 128
HG_DV = 128
HG_KEY = HG_HEADS * HG_DK
HG_WIDTH = HG_HEADS * HG_DV
HG_SUB = 16
HG_SAFE_EXPONENT = 80.0
HG_BLOCK = 4096

N_EXPERTS = 8

LANE = 128
QK_PAD = KV_LORA + LANE

COL_CQ = 0
COL_CKV = Q_LORA
COL_HQ = Q_LORA + KV_LORA
COL_HF = COL_HQ + HG_KEY
COL_HI = COL_HF + HG_KEY
COL_HG = COL_HI + HG_WIDTH
COL_KPE = COL_HG + HG_WIDTH
IN_EXT = COL_KPE + 2 * LANE

VMEM_LIMIT = 56 * 1024 * 1024
TOKEN_TILE = 512
EXPERT_ROW_TILE = 1024
EXPERT_CHUNK = 256
FFN_CHUNK = 512
IN_PROJ_COLS = 1024
ATTN_Q_TILE = 256
ATTN_KEY_BLOCK = 512


def _params(semantics):
    return pltpu.CompilerParams(dimension_semantics=semantics, vmem_limit_bytes=VMEM_LIMIT)


def _rms(x, w):
    return x * lax.rsqrt(jnp.mean(x * x, axis=-1, keepdims=True) + RMS_EPS) * w


def _norm_matmul_kernel(x_ref, nw_ref, w_hbm, o_ref, w_vmem, sem, *, tn):
    @pl.when(pl.program_id(0) == 0)
    def _():
        cp = pltpu.make_async_copy(w_hbm, w_vmem, sem)
        cp.start()
        cp.wait()

    h = _rms(x_ref[...], nw_ref[...]).astype(BF16)
    for c in range(o_ref.shape[1] // tn):
        o_ref[:, c * tn:(c + 1) * tn] = jnp.dot(
            h, w_vmem[:, c * tn:(c + 1) * tn], preferred_element_type=F32).astype(o_ref.dtype)


def norm_matmul(x, nw, w, *, tm, tn):
    n, d = x.shape
    cols = w.shape[1]
    return pl.pallas_call(
        functools.partial(_norm_matmul_kernel, tn=tn),
        grid=(n // tm,),
        in_specs=[pl.BlockSpec((tm, d), lambda i: (i, 0)),
                  pl.BlockSpec((1, d), lambda i: (0, 0)),
                  pl.BlockSpec(memory_space=pl.ANY)],
        out_specs=pl.BlockSpec((tm, cols), lambda i: (i, 0)),
        out_shape=jax.ShapeDtypeStruct((n, cols), BF16),
        scratch_shapes=[pltpu.VMEM((d, cols), BF16), pltpu.SemaphoreType.DMA(())],
        compiler_params=_params(("arbitrary",)),
        name="in_proj",
    )(x, nw, w)


def _mla_prep_kernel(cq_ref, ckv_ref, kp_ref, qnw_ref, kvnw_ref, wuq_ref, wuk_ref, cos_ref, sin_ref,
                     q_ref, kb_ref, ckv_out_ref, kpe_out_ref):
    cos = cos_ref[...]
    sin = sin_ref[...]
    cqn = _rms(cq_ref[...].astype(F32), qnw_ref[...]).astype(BF16)
    q = jnp.dot(cqn, wuq_ref[...], preferred_element_type=F32)
    hw = MLA_HEADS * LANE
    for h in range(MLA_HEADS):
        sl = slice(h * LANE, (h + 1) * LANE)
        q_lat = jnp.dot(q[:, sl].astype(BF16), wuk_ref[h], preferred_element_type=F32)
        rope = q[:, hw + h * LANE:hw + (h + 1) * LANE] * cos + q[:, 2 * hw + h * LANE:2 * hw + (h + 1) * LANE] * sin
        q_ref[h, :, 0:KV_LORA] = q_lat.astype(BF16)
        q_ref[h, :, KV_LORA:QK_PAD] = rope.astype(BF16)
    ckv = _rms(ckv_ref[...].astype(F32), kvnw_ref[...])
    ckv_out_ref[...] = ckv
    kb_ref[:, 0:KV_LORA] = ckv.astype(BF16)
    kp = kp_ref[...].astype(F32)
    kpe = kp[:, 0:LANE] * cos + kp[:, LANE:2 * LANE] * sin
    kpe_out_ref[...] = kpe[:, 0:QK_ROPE]
    kb_ref[:, KV_LORA:QK_PAD] = kpe.astype(BF16)


def mla_prep(p, qnw, kvnw, wuq, wuk, cos, sin, *, tm):
    n = p.shape[0]
    n_pos = cos.shape[0] // tm
    return pl.pallas_call(
        _mla_prep_kernel,
        grid=(n // tm,),
        in_specs=[pl.BlockSpec((tm, Q_LORA), lambda i: (i, COL_CQ // Q_LORA)),
                  pl.BlockSpec((tm, KV_LORA), lambda i: (i, COL_CKV // KV_LORA)),
                  pl.BlockSpec((tm, 2 * LANE), lambda i: (i, COL_KPE // (2 * LANE))),
                  pl.BlockSpec((1, Q_LORA), lambda i: (0, 0)),
                  pl.BlockSpec((1, KV_LORA), lambda i: (0, 0)),
                  pl.BlockSpec(wuq.shape, lambda i: (0, 0)),
                  pl.BlockSpec(wuk.shape, lambda i: (0, 0, 0)),
                  pl.BlockSpec((tm, LANE), lambda i: (i % n_pos, 0)),
                  pl.BlockSpec((tm, LANE), lambda i: (i % n_pos, 0))],
        out_specs=[pl.BlockSpec((MLA_HEADS, tm, QK_PAD), lambda i: (0, i, 0)),
                   pl.BlockSpec((tm, QK_PAD), lambda i: (i, 0)),
                   pl.BlockSpec((tm, KV_LORA), lambda i: (i, 0)),
                   pl.BlockSpec((tm, QK_ROPE), lambda i: (i, 0))],
        out_shape=[jax.ShapeDtypeStruct((MLA_HEADS, n, QK_PAD), BF16),
                   jax.ShapeDtypeStruct((n, QK_PAD), BF16),
                   jax.ShapeDtypeStruct((n, KV_LORA), F32),
                   jax.ShapeDtypeStruct((n, QK_ROPE), F32)],
        compiler_params=_params(("arbitrary",)),
        name="mla_prep",
    )(p, p, p, qnw, kvnw, wuq, wuk, cos, sin)


def _lanes(x, n):
    if n == LANE:
        return x
    if n % LANE == 0:
        return jnp.concatenate([x] * (n // LANE), axis=1)
    return jnp.broadcast_to(x[:, 0:1], (x.shape[0], n))


def _attention_kernel(q_ref, k_ref, wuv_ref, o_ref, m_scr, l_scr, acc_scr, *, tq, kb, causal, lk, n_split):
    rows = MLA_HEADS * tq
    part_rows = rows // n_split
    q = q_ref[...].reshape(rows, QK_PAD)
    m_scr[...] = jnp.full((rows, LANE), NEG_INF, F32)
    l_scr[...] = jnp.zeros((rows, LANE), F32)
    acc_scr[...] = jnp.zeros((rows, KV_LORA), F32)

    def block(start, size, mask):
        k = k_ref[pl.ds(start, size), :]
        v = k[:, 0:KV_LORA]
        scores = [lax.dot_general(q[part * part_rows:(part + 1) * part_rows], k, (((1,), (1,)), ((), ())),
                                  preferred_element_type=F32) for part in range(n_split)]
        for part in range(n_split):
            r = slice(part * part_rows, (part + 1) * part_rows)
            s = scores[part]
            if mask is not None:
                s = jnp.where(mask, s, NEG_INF)
            m_old = m_scr[r, :]
            m_new = jnp.maximum(m_old, jnp.max(s, axis=-1, keepdims=True))
            alpha = jnp.exp2((m_old - m_new) * EXP2_SCALE)
            p = jnp.exp2((s - _lanes(m_new, size)) * EXP2_SCALE)
            l_scr[r, :] = alpha * l_scr[r, :] + jnp.sum(p, axis=-1, keepdims=True)
            acc_scr[r, :] = acc_scr[r, :] * _lanes(alpha, KV_LORA) + jnp.dot(
                p.astype(BF16), v, preferred_element_type=F32)
            m_scr[r, :] = m_new

    if causal:
        q_start = pl.program_id(1) * tq
        n_wide = q_start // (2 * kb)

        def wide_body(j, carry):
            block(pl.multiple_of(j * 2 * kb, 2 * kb), 2 * kb, None)
            return carry

        lax.fori_loop(0, n_wide, wide_body, 0)
        n_main = q_start // kb

        @pl.when(n_main > 2 * n_wide)
        def _():
            block(pl.multiple_of(n_wide * 2 * kb, 2 * kb), kb, None)

        n_before = (q_start - n_main * kb) // tq
        for r in range(kb // tq):
            width = (r + 1) * tq

            @pl.when(n_before == r)
            def _(r=r, width=width):
                tok = jnp.bitwise_and(lax.broadcasted_iota(jnp.int32, (part_rows, width), 0), tq - 1)
                col = lax.broadcasted_iota(jnp.int32, (part_rows, width), 1)
                block(pl.multiple_of(n_main * kb, kb), width, col < r * tq + (tok // CHUNK + 1) * CHUNK)
    else:
        for j in range(lk // kb):
            block(j * kb, kb, None)

    o = (acc_scr[...] / _lanes(l_scr[...], KV_LORA)).astype(BF16)
    for h in range(MLA_HEADS):
        o_ref[:, h * V_HEAD:(h + 1) * V_HEAD] = jnp.dot(
            o[h * tq:(h + 1) * tq], wuv_ref[h], preferred_element_type=F32).astype(o_ref.dtype)


def attention(q, k, wuv, *, batch, lq, lk, tq, kb, causal):
    nq = lq // tq
    rows = MLA_HEADS * tq
    kern = functools.partial(_attention_kernel, tq=tq, kb=kb, causal=causal, lk=lk,
                             n_split=4 if causal else 1)
    return pl.pallas_call(
        kern,
        grid=(batch, nq),
        in_specs=[pl.BlockSpec((MLA_HEADS, tq, QK_PAD), lambda b, i: (0, b * nq + i, 0)),
                  pl.BlockSpec((lk, QK_PAD), lambda b, i: (b, 0)),
                  pl.BlockSpec(wuv.shape, lambda b, i: (0, 0, 0))],
        out_specs=pl.BlockSpec((tq, MLA_WIDTH), lambda b, i: (b * nq + i, 0)),
        out_shape=jax.ShapeDtypeStruct((batch * lq, MLA_WIDTH), BF16),
        scratch_shapes=[pltpu.VMEM((rows, LANE), F32), pltpu.VMEM((rows, LANE), F32),
                        pltpu.VMEM((rows, KV_LORA), F32)],
        compiler_params=_params(("arbitrary", "arbitrary")),
        name="attention_causal" if causal else "attention_full",
    )(q, k, wuv)


def _sigmoid(x):
    return 1.0 / (1.0 + jnp.exp(-x))


def _hgrn_kernel(hq_ref, hf_ref, hi_ref, hg_ref, lb_ref, nw_ref, s0_ref, o_ref, sfin_ref,
                 st_scr, oacc_scr, g_scr, q_scr, k_scr, *, chunk, n_chunks, n_seq):
    t = pl.program_id(2)

    @pl.when(t == 0)
    def _():
        for sq in range(n_seq):
            st_scr[sq] = s0_ref[sq, 0].T

    lb = lb_ref[...]
    lb_floor = jnp.maximum(lb, LB_FLOOR)
    one_m_lb = 1.0 - lb
    tri = (lax.broadcasted_iota(jnp.int32, (chunk, chunk), 0)
           >= lax.broadcasted_iota(jnp.int32, (chunk, chunk), 1))
    tri_bf16 = jnp.where(tri, 1.0, 0.0).astype(BF16)
    ones = jnp.ones((HG_DK, HG_DV), BF16)
    row_id = lax.broadcasted_iota(jnp.int32, (chunk, HG_DK), 0)
    n_sub = chunk // HG_SUB

    hf = hf_ref[...].astype(F32)
    e = jnp.exp(-jnp.abs(hf))
    big = 1.0 / (1.0 + e)
    small = e * big
    pos = hf >= 0.0
    log_f = jnp.log(lb_floor + one_m_lb * jnp.where(pos, big, small))
    k_scr[...] = one_m_lb * jnp.where(pos, small, big)
    hq = hq_ref[...].astype(F32)
    q_scr[...] = hq * _sigmoid(hq)
    lf_a = log_f.astype(BF16)
    rest = log_f - lf_a.astype(F32)
    lf_b = rest.astype(BF16)
    lf_c = (rest - lf_b.astype(F32)).astype(BF16)
    g_min = None
    for c in range(n_seq * n_chunks):
        sl = slice(c * chunk, (c + 1) * chunk)
        parts = jnp.dot(tri_bf16, jnp.concatenate([lf_a[sl], lf_b[sl], lf_c[sl]], axis=1),
                        preferred_element_type=F32)
        g = parts[:, 0:HG_DK] + parts[:, HG_DK:2 * HG_DK] + parts[:, 2 * HG_DK:3 * HG_DK]
        g_scr[sl, :] = g
        g_end = g[chunk - 1:chunk]
        g_min = g_end if g_min is None else jnp.minimum(g_min, g_end)
    factorable = jnp.min(g_min) >= -HG_SAFE_EXPONENT

    def state_step(c, g, k, v16):
        st = st_scr.at[c // n_chunks]
        g_last = g[chunk - 1:chunk]
        kh = k * jnp.exp(g_last - g)
        st[...] = st[...] * jnp.exp(g_last) + lax.dot_general(
            v16, kh.astype(BF16), (((0,), (0,)), ((), ())), preferred_element_type=F32)

    def chunk_factored(c):
        sl = slice(c * chunk, (c + 1) * chunk)
        g, q, k = g_scr[sl, :], q_scr[sl, :], k_scr[sl, :]
        v16 = hi_ref[sl, :]
        qd = (q * jnp.exp(g)).astype(BF16)
        kd = (k * jnp.exp(-g)).astype(BF16)
        a = lax.dot_general(qd, kd, (((1,), (1,)), ((), ())), preferred_element_type=F32)
        a = jnp.where(tri, a, 0.0).astype(BF16)
        oacc_scr[sl, :] = (lax.dot_general(qd, st_scr[c // n_chunks].astype(BF16), (((1,), (1,)), ((), ())),
                                           preferred_element_type=F32)
                           + jnp.dot(a, v16, preferred_element_type=F32))
        state_step(c, g, k, v16)

    def chunk_exact(c):
        sl = slice(c * chunk, (c + 1) * chunk)
        oacc = oacc_scr.at[sl, :]
        g, q, k = g_scr[sl, :], q_scr[sl, :], k_scr[sl, :]
        v16 = hi_ref[sl, :]
        v = v16.astype(F32)

        oacc[...] = lax.dot_general((q * jnp.exp(g)).astype(BF16), st_scr[c // n_chunks].astype(BF16),
                                    (((1,), (1,)), ((), ())), preferred_element_type=F32)
        for i in range(1, n_sub):
            r = i * HG_SUB
            g_edge = g[r - 1:r]
            qt = q[r:r + HG_SUB] * jnp.exp(g[r:r + HG_SUB] - g_edge)
            kt = k[0:r] * jnp.exp(g_edge - g[0:r])
            a = lax.dot_general(qt.astype(BF16), kt.astype(BF16), (((1,), (1,)), ((), ())),
                                preferred_element_type=F32)
            oacc[r:r + HG_SUB, :] += jnp.dot(a.astype(BF16), v16[0:r], preferred_element_type=F32)
        for grp in range(chunk // 8):
            r0 = grp * 8
            r1 = (r0 // HG_SUB + 1) * HG_SUB
            n = r1 - r0
            parts = []
            for s in range(r0, r0 + 8):
                d = jnp.where(row_id[r0:r1] >= s, g[r0:r1] - g[s:s + 1], NEG_INF)
                parts.append(jnp.exp(d) * q[r0:r1] * k[s:s + 1])
            sums = jnp.dot(jnp.concatenate(parts, axis=0).astype(BF16), ones, preferred_element_type=F32)
            upd = sums[0:n] * v[r0:r0 + 1]
            for u in range(1, 8):
                upd = upd + sums[u * n:(u + 1) * n] * v[r0 + u:r0 + u + 1]
            oacc[r0:r1, :] += upd
        state_step(c, g, k, v16)

    @pl.when(factorable)
    def _():
        for c in range(n_seq * n_chunks):
            chunk_factored(c)

    @pl.when(jnp.logical_not(factorable))
    def _():
        for c in range(n_seq * n_chunks):
            chunk_exact(c)

    hg = hg_ref[...].astype(F32)
    o_ref[...] = (_rms(oacc_scr[...], nw_ref[...]) * (hg * _sigmoid(hg))).astype(o_ref.dtype)

    @pl.when(t == pl.num_programs(2) - 1)
    def _():
        for sq in range(n_seq):
            sfin_ref[sq, 0] = st_scr[sq].T


def hgrn(p, lb, nw, s0, *, batch, seq, chunk, tb, n_seq=1):
    nt = max(seq // tb, 1)
    assert tb == n_seq * seq or (n_seq == 1 and seq % tb == 0)
    kern = functools.partial(_hgrn_kernel, chunk=chunk, n_chunks=tb // (n_seq * chunk), n_seq=n_seq)

    def col(base):
        return lambda b, h, t: (b * nt + t, base // LANE + h)

    return pl.pallas_call(
        kern,
        grid=(batch // n_seq, HG_HEADS, nt),
        in_specs=[pl.BlockSpec((tb, LANE), col(COL_HQ)),
                  pl.BlockSpec((tb, LANE), col(COL_HF)),
                  pl.BlockSpec((tb, LANE), col(COL_HI)),
                  pl.BlockSpec((tb, LANE), col(COL_HG)),
                  pl.BlockSpec((1, HG_DK), lambda b, h, t: (0, h)),
                  pl.BlockSpec((1, HG_DV), lambda b, h, t: (0, 0)),
                  pl.BlockSpec((n_seq, 1, HG_DK, HG_DV), lambda b, h, t: (b, h, 0, 0))],
        out_specs=[pl.BlockSpec((tb, HG_DV), lambda b, h, t: (b * nt + t, h)),
                   pl.BlockSpec((n_seq, 1, HG_DK, HG_DV), lambda b, h, t: (b, h, 0, 0))],
        out_shape=[jax.ShapeDtypeStruct((batch * seq, HG_WIDTH), BF16),
                   jax.ShapeDtypeStruct((batch, HG_HEADS, HG_DK, HG_DV), F32)],
        scratch_shapes=[pltpu.VMEM((n_seq, HG_DV, HG_DK), F32), pltpu.VMEM((tb, HG_DV), F32),
                        pltpu.VMEM((tb, HG_DK), F32), pltpu.VMEM((tb, HG_DK), F32),
                        pltpu.VMEM((tb, HG_DK), F32)],
        compiler_params=_params(("arbitrary", "arbitrary", "arbitrary")),
        name="hgrn",
    )(p, p, p, p, lb, nw, s0)


def _out_proj_kernel(x_ref, a_ref, b_ref, w_ref, o_ref):
    o_ref[...] = (x_ref[...]
                  + jnp.dot(a_ref[...], w_ref[0:MLA_WIDTH, :], preferred_element_type=F32)
                  + jnp.dot(b_ref[...], w_ref[MLA_WIDTH:MLA_WIDTH + HG_WIDTH, :], preferred_element_type=F32))


def out_proj(x, a, b, w, *, tm):
    n, d = x.shape
    return pl.pallas_call(
        _out_proj_kernel,
        grid=(n // tm,),
        in_specs=[pl.BlockSpec((tm, d), lambda i: (i, 0)),
                  pl.BlockSpec((tm, MLA_WIDTH), lambda i: (i, 0)),
                  pl.BlockSpec((tm, HG_WIDTH), lambda i: (i, 0)),
                  pl.BlockSpec(w.shape, lambda i: (0, 0))],
        out_specs=pl.BlockSpec((tm, d), lambda i: (i, 0)),
        out_shape=jax.ShapeDtypeStruct((n, d), F32),
        compiler_params=_params(("arbitrary",)),
        name="out_proj",
    )(x, a, b, w)


def _silu(x):
    return x * _sigmoid(x)


def _swiglu_chunk(h, wg, wu, wd):
    a = _silu(jnp.dot(h, wg, preferred_element_type=F32)) * jnp.dot(h, wu, preferred_element_type=F32)
    return jnp.dot(a.astype(BF16), wd, preferred_element_type=F32)


def _ffn_kernel(x_ref, nw_ref, wg_hbm, wu_hbm, wd_hbm, o_ref, h_scr, acc_scr, wg_buf, wu_buf, wd_buf, sem, *, nf):
    i = pl.program_id(0)
    n_tiles = pl.num_programs(0)

    tf = wd_buf.shape[1]

    def chunk_copies(j, slot):
        cols = pl.ds(pl.multiple_of(j * tf, tf), tf)
        return (pltpu.make_async_copy(wg_hbm.at[:, cols], wg_buf.at[slot], sem.at[0, slot]),
                pltpu.make_async_copy(wu_hbm.at[:, cols], wu_buf.at[slot], sem.at[1, slot]),
                pltpu.make_async_copy(wd_hbm.at[cols, :], wd_buf.at[slot], sem.at[2, slot]))

    @pl.when(i == 0)
    def _():
        for cp in chunk_copies(0, 0):
            cp.start()

    h_scr[...] = _rms(x_ref[...], nw_ref[...]).astype(BF16)
    acc_scr[...] = jnp.zeros_like(acc_scr)

    def body(j, carry):
        slot = (i * nf + j) % 2

        @pl.when(j + 1 < nf)
        def _():
            for cp in chunk_copies(j + 1, 1 - slot):
                cp.start()

        @pl.when((j + 1 == nf) & (i + 1 < n_tiles))
        def _():
            for cp in chunk_copies(0, 1 - slot):
                cp.start()

        for cp in chunk_copies(j, slot):
            cp.wait()
        acc_scr[...] += _swiglu_chunk(h_scr[...], wg_buf[slot], wu_buf[slot], wd_buf[slot])
        return carry

    lax.fori_loop(0, nf, body, 0)
    o_ref[...] = x_ref[...] + acc_scr[...]


def ffn(x, nw, wg, wu, wd, *, tm, tf):
    n, d = x.shape
    nf = wg.shape[1] // tf
    any_spec = pl.BlockSpec(memory_space=pl.ANY)
    return pl.pallas_call(
        functools.partial(_ffn_kernel, nf=nf),
        grid=(n // tm,),
        in_specs=[pl.BlockSpec((tm, d), lambda i: (i, 0)),
                  pl.BlockSpec((1, d), lambda i: (0, 0)), any_spec, any_spec, any_spec],
        out_specs=pl.BlockSpec((tm, d), lambda i: (i, 0)),
        out_shape=jax.ShapeDtypeStruct((n, d), F32),
        scratch_shapes=[pltpu.VMEM((tm, d), BF16), pltpu.VMEM((tm, d), F32),
                        pltpu.VMEM((2, d, tf), BF16), pltpu.VMEM((2, d, tf), BF16),
                        pltpu.VMEM((2, tf, d), BF16), pltpu.SemaphoreType.DMA((3, 2))],
        compiler_params=_params(("arbitrary",)),
        name="ffn",
    )(x, nw, wg, wu, wd)


def _moe_route_kernel(x_ref, nw_ref, wr_ref, hp_ref, e1_ref, e2_ref, w1_ref, w2_ref, r1_ref, r2_ref,
                      cnt_ref, cnt_scr):
    @pl.when(pl.program_id(0) == 0)
    def _():
        cnt_scr[...] = jnp.zeros_like(cnt_scr)

    tm = x_ref.shape[0]
    half = x_ref.shape[1] // 2
    h = _rms(x_ref[...], nw_ref[...])
    hb = h.astype(BF16)
    h_lo = (h - hb.astype(F32)).astype(BF16)
    wr = wr_ref[...]
    w_hi = wr.astype(BF16)
    w_lo = (wr - w_hi.astype(F32)).astype(BF16)
    logits = jnp.dot(hb, w_hi, preferred_element_type=F32) + (
        jnp.dot(hb, w_lo, preferred_element_type=F32) + jnp.dot(h_lo, w_hi, preferred_element_type=F32))
    ids = lax.broadcasted_iota(jnp.int32, logits.shape, 1).astype(F32)
    m1 = jnp.max(logits, axis=-1, keepdims=True)
    i1 = jnp.min(jnp.where(logits == m1, ids, float(N_EXPERTS)), axis=-1, keepdims=True)
    rest = jnp.where(ids == i1, -jnp.inf, logits)
    m2 = jnp.max(rest, axis=-1, keepdims=True)
    i2 = jnp.min(jnp.where(rest == m2, ids, float(N_EXPERTS)), axis=-1, keepdims=True)
    t = jnp.exp(m2 - m1)
    e1_ref[...] = i1.astype(jnp.int32)
    e2_ref[...] = i2.astype(jnp.int32)
    w1_ref[...] = 1.0 / (1.0 + t)
    w2_ref[...] = t / (1.0 + t)
    oh1 = (ids == i1).astype(F32)
    oh2 = (ids == i2).astype(F32)
    both = oh1 + oh2
    before = (lax.broadcasted_iota(jnp.int32, (tm, tm), 0)
              > lax.broadcasted_iota(jnp.int32, (tm, tm), 1)).astype(BF16)
    prefix = jnp.dot(before, both.astype(BF16), preferred_element_type=F32) + cnt_scr[...]
    r1_ref[...] = jnp.sum(prefix * oh1, axis=-1, keepdims=True).astype(jnp.int32)
    r2_ref[...] = jnp.sum(prefix * oh2, axis=-1, keepdims=True).astype(jnp.int32)
    cnt_scr[...] += jnp.sum(both, axis=0, keepdims=True)
    cnt_ref[...] = cnt_scr[...].astype(jnp.int32)
    hb32 = hb.astype(F32)
    hi = pltpu.bitcast(hb32[:, 0:half], jnp.uint32)
    lo = pltpu.bitcast(hb32[:, half:2 * half], jnp.uint32)
    hp_ref[...] = hi | (lo >> 16)


def moe_route(x, nw, wr, *, tm):
    n, d = x.shape
    col = lambda dt: jax.ShapeDtypeStruct((n, 1), dt)
    col_spec = pl.BlockSpec((tm, 1), lambda i: (i, 0))
    return pl.pallas_call(
        _moe_route_kernel,
        grid=(n // tm,),
        in_specs=[pl.BlockSpec((tm, d), lambda i: (i, 0)),
                  pl.BlockSpec((1, d), lambda i: (0, 0)),
                  pl.BlockSpec(wr.shape, lambda i: (0, 0))],
        out_specs=[pl.BlockSpec((tm, d // 2), lambda i: (i, 0))] + [col_spec] * 6
                  + [pl.BlockSpec((1, N_EXPERTS), lambda i: (0, 0))],
        out_shape=[jax.ShapeDtypeStruct((n, d // 2), jnp.uint32), col(jnp.int32), col(jnp.int32),
                   col(F32), col(F32), col(jnp.int32), col(jnp.int32),
                   jax.ShapeDtypeStruct((1, N_EXPERTS), jnp.int32)],
        scratch_shapes=[pltpu.VMEM((1, N_EXPERTS), F32)],
        compiler_params=_params(("arbitrary",)),
        name="moe_route",
    )(x, nw, wr)


def _row_copy(src, src_row, dst, dst_row, sem):
    return pltpu.make_async_copy(src.at[pl.ds(src_row, 1), :], dst.at[pl.ds(dst_row, 1), :], sem)


def _moe_scatter_kernel(d1_ref, d2_ref, hp_ref, xs_in_hbm, xs_hbm, sem, *, tm):
    del xs_in_hbm

    def issue(r, carry):
        _row_copy(hp_ref, r, xs_hbm, d1_ref[0, 0, r], sem).start()
        _row_copy(hp_ref, r, xs_hbm, d2_ref[0, 0, r], sem).start()
        return carry

    lax.fori_loop(0, tm, issue, 0, unroll=8)

    def drain(r, carry):
        _row_copy(hp_ref, r, xs_hbm, d1_ref[0, 0, r], sem).wait()
        _row_copy(hp_ref, r, xs_hbm, d2_ref[0, 0, r], sem).wait()
        return carry

    lax.fori_loop(0, tm, drain, 0, unroll=8)


def moe_scatter(hp, d1, d2, xs, *, tm):
    n = hp.shape[0]
    idx_spec = pl.BlockSpec((1, 1, tm), lambda i: (i, 0, 0), memory_space=pltpu.SMEM)
    any_spec = pl.BlockSpec(memory_space=pl.ANY)
    return pl.pallas_call(
        functools.partial(_moe_scatter_kernel, tm=tm),
        grid=(n // tm,),
        in_specs=[idx_spec, idx_spec, pl.BlockSpec((tm, hp.shape[1]), lambda i: (i, 0)), any_spec],
        out_specs=any_spec,
        out_shape=jax.ShapeDtypeStruct(xs.shape, xs.dtype),
        scratch_shapes=[pltpu.SemaphoreType.DMA(())],
        input_output_aliases={3: 0},
        compiler_params=_params(("arbitrary",)),
        name="moe_scatter",
    )(d1.reshape(n // tm, 1, tm), d2.reshape(n // tm, 1, tm), hp, xs)


def _moe_expert_kernel(te_ref, na_ref, xs_ref, wg_hbm, wu_hbm, wd_hbm, ys_ref, h_scr, wg_buf, wu_buf, wd_buf, sem,
                       *, nf):
    i = pl.program_id(0)
    n_act = na_ref[0]
    active = i < n_act
    half = xs_ref.shape[1]

    tf = wd_buf.shape[1]

    def chunk_copies(tile, j, slot):
        e = te_ref[tile]
        cols = pl.ds(pl.multiple_of(j * tf, tf), tf)
        return (pltpu.make_async_copy(wg_hbm.at[e, :, cols], wg_buf.at[slot], sem.at[0, slot]),
                pltpu.make_async_copy(wu_hbm.at[e, :, cols], wu_buf.at[slot], sem.at[1, slot]),
                pltpu.make_async_copy(wd_hbm.at[e, cols, :], wd_buf.at[slot], sem.at[2, slot]))

    @pl.when(active)
    def _():
        @pl.when(i == 0)
        def _():
            for cp in chunk_copies(0, 0, 0):
                cp.start()

        packed = xs_ref[...]
        h_scr[:, 0:half] = pltpu.bitcast(packed & jnp.uint32(0xFFFF0000), F32).astype(BF16)
        h_scr[:, half:2 * half] = pltpu.bitcast(packed << 16, F32).astype(BF16)
        ys_ref[...] = jnp.zeros_like(ys_ref)

        def body(j, carry):
            slot = (i * nf + j) % 2

            @pl.when(j + 1 < nf)
            def _():
                for cp in chunk_copies(i, j + 1, 1 - slot):
                    cp.start()

            @pl.when((j + 1 == nf) & (i + 1 < n_act))
            def _():
                for cp in chunk_copies(i + 1, 0, 1 - slot):
                    cp.start()

            for cp in chunk_copies(i, j, slot):
                cp.wait()
            ys_ref[...] += _swiglu_chunk(h_scr[...], wg_buf[slot], wu_buf[slot], wd_buf[slot])
            return carry

        lax.fori_loop(0, nf, body, 0)

    @pl.when(jnp.logical_not(active))
    def _():
        ys_ref[...] = jnp.zeros_like(ys_ref)


def moe_experts(xs, tile_expert, n_active, wg, wu, wd, *, tm, tf):
    rows, half = xs.shape
    d = 2 * half
    nf = wg.shape[2] // tf
    any_spec = pl.BlockSpec(memory_space=pl.ANY)
    return pl.pallas_call(
        functools.partial(_moe_expert_kernel, nf=nf),
        grid_spec=pltpu.PrefetchScalarGridSpec(
            num_scalar_prefetch=2,
            grid=(rows // tm,),
            in_specs=[pl.BlockSpec((tm, half), lambda i, te, na: (i, 0)), any_spec, any_spec, any_spec],
            out_specs=pl.BlockSpec((tm, d), lambda i, te, na: (i, 0)),
            scratch_shapes=[pltpu.VMEM((tm, d), BF16),
                            pltpu.VMEM((2, d, tf), BF16), pltpu.VMEM((2, d, tf), BF16),
                            pltpu.VMEM((2, tf, d), BF16), pltpu.SemaphoreType.DMA((3, 2))]),
        out_shape=jax.ShapeDtypeStruct((rows, d), F32),
        compiler_params=_params(("arbitrary",)),
        name="moe_experts",
    )(tile_expert, n_active, xs, wg, wu, wd)


def _moe_combine_kernel(d1_ref, d2_ref, d1_next_ref, d2_next_ref, x_ref, w1_ref, w2_ref, fw_ref, ys_hbm, o_ref,
                        a_scr, b_scr, sem, *, tm):
    i = pl.program_id(0)
    slot = i % 2

    def gather(r1_ref, r2_ref, s, wait):
        def row(r, carry):
            for src_ref, dst in ((r1_ref, a_scr), (r2_ref, b_scr)):
                cp = _row_copy(ys_hbm, src_ref[0, 0, r], dst.at[s], r, sem.at[s])
                cp.wait() if wait else cp.start()
            return carry

        lax.fori_loop(0, tm, row, 0, unroll=8)

    @pl.when(i == 0)
    def _():
        gather(d1_ref, d2_ref, 0, wait=False)

    @pl.when(i + 1 < pl.num_programs(0))
    def _():
        gather(d1_next_ref, d2_next_ref, 1 - slot, wait=False)

    gather(d1_ref, d2_ref, slot, wait=True)
    y = x_ref[...] + (w1_ref[...] * a_scr[slot] + w2_ref[...] * b_scr[slot])
    o_ref[...] = _rms(y, fw_ref[...])


def moe_combine(x, ys, d1, d2, w1, w2, fw, *, tm):
    n, d = x.shape
    nt = n // tm
    idx_spec = pl.BlockSpec((1, 1, tm), lambda i: (i, 0, 0), memory_space=pltpu.SMEM)
    next_spec = pl.BlockSpec((1, 1, tm), lambda i: (jnp.minimum(i + 1, nt - 1), 0, 0), memory_space=pltpu.SMEM)
    col_spec = pl.BlockSpec((tm, 1), lambda i: (i, 0))
    d1, d2 = d1.reshape(nt, 1, tm), d2.reshape(nt, 1, tm)
    return pl.pallas_call(
        functools.partial(_moe_combine_kernel, tm=tm),
        grid=(nt,),
        in_specs=[idx_spec, idx_spec, next_spec, next_spec, pl.BlockSpec((tm, d), lambda i: (i, 0)),
                  col_spec, col_spec, pl.BlockSpec((1, d), lambda i: (0, 0)), pl.BlockSpec(memory_space=pl.ANY)],
        out_specs=pl.BlockSpec((tm, d), lambda i: (i, 0)),
        out_shape=jax.ShapeDtypeStruct((n, d), F32),
        scratch_shapes=[pltpu.VMEM((2, tm, d), F32), pltpu.VMEM((2, tm, d), F32),
                        pltpu.SemaphoreType.DMA((2,))],
        compiler_params=_params(("arbitrary",)),
        name="moe_combine",
    )(d1, d2, d1, d2, x, w1, w2, fw, ys)


def moe_final(streams, nw, wr, wg, wu, wd, fw):
    n_exp = wg.shape[0]
    d = streams[0].shape[1]
    tms = [min(TOKEN_TILE, x.shape[0]) for x in streams]
    tile = EXPERT_ROW_TILE
    routed = [moe_route(x, nw, wr, tm=tm) for x, tm in zip(streams, tms)]
    counts = [r[7][0] for r in routed]
    total = sum(counts)
    padded = (total + tile - 1) // tile * tile
    ends = jnp.cumsum(padded)
    starts = ends - padded
    n_rows = sum(2 * x.shape[0] for x in streams) + n_exp * tile
    n_tiles = n_rows // tile
    tile_expert = jnp.minimum(
        jnp.sum(jnp.arange(n_tiles, dtype=jnp.int32)[:, None] * tile >= ends[None, :], axis=1), n_exp - 1
    ).astype(jnp.int32)
    n_active = (ends[n_exp - 1:] // tile).astype(jnp.int32)

    def slot_rows(e, rank, offset):
        table = starts + offset
        return (jnp.sum(jnp.where(e == jnp.arange(n_exp)[None, :], table[None, :], 0), axis=1, keepdims=True)
                + rank).astype(jnp.int32)

    xs = jnp.zeros((n_rows, d // 2), jnp.uint32)
    dests = []
    offset = jnp.zeros((n_exp,), jnp.int32)
    for (hp, e1, e2, w1, w2, r1, r2, cnt), tm in zip(routed, tms):
        d1, d2 = slot_rows(e1, r1, offset), slot_rows(e2, r2, offset)
        xs = moe_scatter(hp, d1, d2, xs, tm=tm)
        dests.append((d1, d2))
        offset = offset + cnt[0]
    ys = moe_experts(xs, tile_expert, n_active, wg, wu, wd, tm=tile, tf=EXPERT_CHUNK)
    return [moe_combine(x, ys, d1, d2, r[3], r[4], fw, tm=tm)
            for x, r, (d1, d2), tm in zip(streams, routed, dests, tms)]


def _swap_halves(t):
    half = t.shape[-1] // 2
    return jnp.concatenate([t[..., half:], t[..., :half]], axis=-1)


def _prep_w_in(w):
    d = w.shape[0]
    k_pe = w[:, Q_LORA + KV_LORA:Q_LORA + KV_LORA + QK_ROPE]
    pad = jnp.zeros((d, LANE - QK_ROPE), w.dtype)
    return jnp.concatenate([w[:, :Q_LORA + KV_LORA], w[:, Q_LORA + KV_LORA + QK_ROPE:],
                            k_pe, pad, _swap_halves(k_pe), pad], axis=1).astype(BF16)


def _prep_w_uq(w):
    w = w.reshape(Q_LORA, MLA_HEADS, QK_NOPE + QK_ROPE)
    nope = w[..., :QK_NOPE]
    rope = w[..., QK_NOPE:]
    pad = jnp.zeros((Q_LORA, MLA_HEADS, LANE - QK_ROPE), w.dtype)
    parts = [nope, jnp.concatenate([rope, pad], -1), jnp.concatenate([_swap_halves(rope), pad], -1)]
    return jnp.concatenate([t.reshape(Q_LORA, MLA_HEADS * LANE) for t in parts], axis=1).astype(BF16)


def _rope_tables(pos):
    half = QK_ROPE // 2
    inv_freq = jnp.exp(jnp.arange(half, dtype=F32) * (-math.log(ROPE_THETA) / half))
    ang = pos.astype(F32)[:, None] * inv_freq[None, :]
    cos, sin = jnp.cos(ang), jnp.sin(ang)
    pad = jnp.zeros((pos.shape[0], LANE - QK_ROPE), F32)
    return jnp.concatenate([cos, cos, pad], -1), jnp.concatenate([-sin, sin, pad], -1)


def _row(v):
    return v.reshape(1, -1)


def kernel(x_prompt, x_sample, cache_ckv, cache_kpe, state_hgrn, attn_norm_w, w_in, q_norm_w, kv_norm_w,
           w_uq, w_uk, w_uv, hg_lower_bounds, hg_norm_w, w_out, ffn_norm_w, w_gate, w_up, w_down,
           w_router, we_gate, we_up, we_down, final_norm_w):
    depth = w_in.shape[0]
    b_p, l_p, d = x_prompt.shape
    b_s, l_s, _ = x_sample.shape
    past = cache_ckv.shape[2]
    n_p, n_s = b_p * l_p, b_s * l_s
    assert depth == 2 and d == D_MODEL

    probs = jax.nn.softmax(hg_lower_bounds.astype(F32), axis=0)
    lower_bounds = jnp.cumsum(probs, axis=0) - probs[0:1]

    cos_p, sin_p = _rope_tables(jnp.arange(l_p))
    cos_s, sin_s = _rope_tables(past + jnp.arange(l_s))
    cos_s, sin_s = jnp.tile(cos_s, (b_s, 1)), jnp.tile(sin_s, (b_s, 1))
    zero_state = jnp.zeros((b_p, HG_HEADS, HG_DK, HG_DV), F32)

    xp = x_prompt.reshape(n_p, d)
    xs = x_sample.reshape(n_s, d)
    outs = {k: [] for k in ("ckv_p", "kpe_p", "st_p", "ckv_s", "kpe_s", "st_s")}

    for l in range(depth):
        w_in_l = _prep_w_in(w_in[l])
        w_uq_l = _prep_w_uq(w_uq[l])
        w_uk_l = jnp.transpose(w_uk[l], (1, 2, 0)).astype(BF16)
        w_uv_l = jnp.transpose(w_uv[l], (1, 0, 2)).astype(BF16)
        w_out_l = w_out[l].astype(BF16)
        lb_l = _row(lower_bounds[l])
        mixers = []
        for (x, n, batch, seq, cos, sin) in ((xp, n_p, b_p, l_p, cos_p, sin_p),
                                             (xs, n_s, b_s, l_s, cos_s, sin_s)):
            tm = min(TOKEN_TILE, n)
            p = norm_matmul(x, _row(attn_norm_w[l]), w_in_l, tm=tm, tn=IN_PROJ_COLS)
            q, kb, ckv, kpe = mla_prep(p, _row(q_norm_w[l]), _row(kv_norm_w[l]), w_uq_l, w_uk_l,
                                       cos, sin, tm=tm)
            if x is xp:
                o_mla = attention(q, kb, w_uv_l, batch=batch, lq=seq, lk=seq,
                                  tq=min(ATTN_Q_TILE, seq), kb=ATTN_KEY_BLOCK, causal=True)
                o_hg, st = hgrn(p, lb_l, _row(hg_norm_w[l]), zero_state, batch=batch, seq=seq,
                                chunk=CHUNK, tb=min(HG_BLOCK, seq))
            else:
                past_k = jnp.concatenate(
                    [cache_ckv[l].astype(BF16), cache_kpe[l].astype(BF16),
                     jnp.zeros((batch, past, LANE - QK_ROPE), BF16)], axis=-1)
                keys = jnp.concatenate([past_k, kb.reshape(batch, seq, QK_PAD)], axis=1)
                lk = past + seq
                o_mla = attention(q, keys.reshape(batch * lk, QK_PAD), w_uv_l, batch=batch, lq=seq, lk=lk,
                                  tq=seq, kb=lk, causal=False)
                n_seq = math.gcd(batch, 8)
                o_hg, st = hgrn(p, lb_l, _row(hg_norm_w[l]), state_hgrn[l], batch=batch, seq=seq,
                                chunk=seq, tb=n_seq * seq, n_seq=n_seq)
            x1 = out_proj(x, o_mla, o_hg, w_out_l, tm=tm)
            mixers.append((x1, ckv.reshape(batch, seq, KV_LORA), kpe.reshape(batch, seq, QK_ROPE), st))
        (xp, ckv_p, kpe_p, st_p), (xs, ckv_s, kpe_s, st_s) = mixers
        for name, val in (("ckv_p", ckv_p), ("kpe_p", kpe_p), ("st_p", st_p),
                          ("ckv_s", ckv_s), ("kpe_s", kpe_s), ("st_s", st_s)):
            outs[name].append(val)

        i = l // 2
        if l % 2 == 0:
            wg, wu, wd = w_gate[i].astype(BF16), w_up[i].astype(BF16), w_down[i].astype(BF16)
            xp = ffn(xp, _row(ffn_norm_w[l]), wg, wu, wd, tm=min(TOKEN_TILE, n_p), tf=FFN_CHUNK)
            xs = ffn(xs, _row(ffn_norm_w[l]), wg, wu, wd, tm=min(TOKEN_TILE, n_s), tf=FFN_CHUNK)
        else:
            wg, wu, wd = we_gate[i].astype(BF16), we_up[i].astype(BF16), we_down[i].astype(BF16)
            xp, xs = moe_final([xp, xs], _row(ffn_norm_w[l]), w_router[i], wg, wu, wd, _row(final_norm_w))

    return (xp.reshape(b_p, l_p, d), xs.reshape(b_s, l_s, d),
            jnp.stack(outs["ckv_p"]), jnp.stack(outs["kpe_p"]), jnp.stack(outs["st_p"]),
            jnp.stack(outs["ckv_s"]), jnp.stack(outs["kpe_s"]), jnp.stack(outs["st_s"]))
```

```python
import functools
import math

import jax
import jax.numpy as jnp
from jax import lax
from jax.experimental import pallas as pl
from jax.experimental.pallas import tpu as pltpu

F32 = jnp.float32
BF16 = jnp.bfloat16

D_MODEL = 2048
CHUNK = 64
RMS_EPS = 1e-6
NEG_INF = -1e30
LB_FLOOR = 1e-30

MLA_HEADS = 8
V_HEAD = 128
QK_NOPE = 128
QK_ROPE = 64
Q_LORA = 512
KV_LORA = 256
ROPE_THETA = 10000.0
ATTN_SCALE = (QK_NOPE + QK_ROPE) ** -0.5
EXP2_SCALE = ATTN_SCALE * math.log2(math.e)
MLA_WIDTH = MLA_HEADS * V_HEAD

HG_HEADS = 8
HG_DK = 128
HG_DV = 128
HG_KEY = HG_HEADS * HG_DK
HG_WIDTH = HG_HEADS * HG_DV
HG_SUB = 16
HG_SAFE_EXPONENT = 80.0
HG_BLOCK = 4096

N_EXPERTS = 8

LANE = 128
QK_PAD = KV_LORA + LANE

COL_CQ = 0
COL_CKV = Q_LORA
COL_HQ = Q_LORA + KV_LORA
COL_HF = COL_HQ + HG_KEY
COL_HI = COL_HF + HG_KEY
COL_HG = COL_HI + HG_WIDTH
COL_KPE = COL_HG + HG_WIDTH
IN_EXT = COL_KPE + 2 * LANE

VMEM_LIMIT = 56 * 1024 * 1024
TOKEN_TILE = 512
EXPERT_ROW_TILE = 1024
EXPERT_CHUNK = 256
FFN_ROW_TILE = 1024
FFN_CHUNK = 256
IN_PROJ_COLS = 1024
ATTN_Q_TILE = 256
ATTN_KEY_BLOCK = 512


def _params(semantics):
    return pltpu.CompilerParams(dimension_semantics=semantics, vmem_limit_bytes=VMEM_LIMIT)


def _rms(x, w):
    return x * lax.rsqrt(jnp.mean(x * x, axis=-1, keepdims=True) + RMS_EPS) * w


def _norm_matmul_kernel(x_ref, nw_ref, w_hbm, o_ref, w_vmem, sem, *, tn):
    @pl.when(pl.program_id(0) == 0)
    def _():
        cp = pltpu.make_async_copy(w_hbm, w_vmem, sem)
        cp.start()
        cp.wait()

    h = _rms(x_ref[...], nw_ref[...]).astype(BF16)
    for c in range(o_ref.shape[1] // tn):
        o_ref[:, c * tn:(c + 1) * tn] = jnp.dot(
            h, w_vmem[:, c * tn:(c + 1) * tn], preferred_element_type=F32).astype(o_ref.dtype)


def norm_matmul(x, nw, w, *, tm, tn):
    n, d = x.shape
    cols = w.shape[1]
    return pl.pallas_call(
        functools.partial(_norm_matmul_kernel, tn=tn),
        grid=(n // tm,),
        in_specs=[pl.BlockSpec((tm, d), lambda i: (i, 0)),
                  pl.BlockSpec((1, d), lambda i: (0, 0)),
                  pl.BlockSpec(memory_space=pl.ANY)],
        out_specs=pl.BlockSpec((tm, cols), lambda i: (i, 0)),
        out_shape=jax.ShapeDtypeStruct((n, cols), BF16),
        scratch_shapes=[pltpu.VMEM((d, cols), BF16), pltpu.SemaphoreType.DMA(())],
        compiler_params=_params(("arbitrary",)),
        name="in_proj",
    )(x, nw, w)


def _mla_prep_kernel(cq_ref, ckv_ref, kp_ref, qnw_ref, kvnw_ref, wuq_ref, wuk_ref, cos_ref, sin_ref,
                     q_ref, kb_ref, ckv_out_ref, kpe_out_ref):
    cos = cos_ref[...]
    sin = sin_ref[...]
    cqn = _rms(cq_ref[...].astype(F32), qnw_ref[...]).astype(BF16)
    q = jnp.dot(cqn, wuq_ref[...], preferred_element_type=F32)
    hw = MLA_HEADS * LANE
    for h in range(MLA_HEADS):
        sl = slice(h * LANE, (h + 1) * LANE)
        q_lat = jnp.dot(q[:, sl].astype(BF16), wuk_ref[h], preferred_element_type=F32)
        rope = q[:, hw + h * LANE:hw + (h + 1) * LANE] * cos + q[:, 2 * hw + h * LANE:2 * hw + (h + 1) * LANE] * sin
        q_ref[h, :, 0:KV_LORA] = q_lat.astype(BF16)
        q_ref[h, :, KV_LORA:QK_PAD] = rope.astype(BF16)
    ckv = _rms(ckv_ref[...].astype(F32), kvnw_ref[...])
    ckv_out_ref[...] = ckv
    kb_ref[:, 0:KV_LORA] = ckv.astype(BF16)
    kp = kp_ref[...].astype(F32)
    kpe = kp[:, 0:LANE] * cos + kp[:, LANE:2 * LANE] * sin
    kpe_out_ref[...] = kpe[:, 0:QK_ROPE]
    kb_ref[:, KV_LORA:QK_PAD] = kpe.astype(BF16)


def mla_prep(p, qnw, kvnw, wuq, wuk, cos, sin, *, tm):
    n = p.shape[0]
    n_pos = cos.shape[0] // tm
    return pl.pallas_call(
        _mla_prep_kernel,
        grid=(n // tm,),
        in_specs=[pl.BlockSpec((tm, Q_LORA), lambda i: (i, COL_CQ // Q_LORA)),
                  pl.BlockSpec((tm, KV_LORA), lambda i: (i, COL_CKV // KV_LORA)),
                  pl.BlockSpec((tm, 2 * LANE), lambda i: (i, COL_KPE // (2 * LANE))),
                  pl.BlockSpec((1, Q_LORA), lambda i: (0, 0)),
                  pl.BlockSpec((1, KV_LORA), lambda i: (0, 0)),
                  pl.BlockSpec(wuq.shape, lambda i: (0, 0)),
                  pl.BlockSpec(wuk.shape, lambda i: (0, 0, 0)),
                  pl.BlockSpec((tm, LANE), lambda i: (i % n_pos, 0)),
                  pl.BlockSpec((tm, LANE), lambda i: (i % n_pos, 0))],
        out_specs=[pl.BlockSpec((MLA_HEADS, tm, QK_PAD), lambda i: (0, i, 0)),
                   pl.BlockSpec((tm, QK_PAD), lambda i: (i, 0)),
                   pl.BlockSpec((tm, KV_LORA), lambda i: (i, 0)),
                   pl.BlockSpec((tm, QK_ROPE), lambda i: (i, 0))],
        out_shape=[jax.ShapeDtypeStruct((MLA_HEADS, n, QK_PAD), BF16),
                   jax.ShapeDtypeStruct((n, QK_PAD), BF16),
                   jax.ShapeDtypeStruct((n, KV_LORA), F32),
                   jax.ShapeDtypeStruct((n, QK_ROPE), F32)],
        compiler_params=_params(("arbitrary",)),
        name="mla_prep",
    )(p, p, p, qnw, kvnw, wuq, wuk, cos, sin)


def _lanes(x, n):
    if n == LANE:
        return x
    if n % LANE == 0:
        return jnp.concatenate([x] * (n // LANE), axis=1)
    return jnp.broadcast_to(x[:, 0:1], (x.shape[0], n))


def _attention_kernel(q_ref, k_ref, wuv_ref, o_ref, m_scr, l_scr, acc_scr, *, tq, kb, causal, lk, n_split):
    rows = MLA_HEADS * tq
    part_rows = rows // n_split
    q = q_ref[...].reshape(rows, QK_PAD)
    m_scr[...] = jnp.full((rows, LANE), NEG_INF, F32)
    l_scr[...] = jnp.zeros((rows, LANE), F32)
    acc_scr[...] = jnp.zeros((rows, KV_LORA), F32)

    def block(start, size, mask):
        k = k_ref[pl.ds(start, size), :]
        v = k[:, 0:KV_LORA]
        scores = [lax.dot_general(q[part * part_rows:(part + 1) * part_rows], k, (((1,), (1,)), ((), ())),
                                  preferred_element_type=F32) for part in range(n_split)]
        for part in range(n_split):
            r = slice(part * part_rows, (part + 1) * part_rows)
            s = scores[part]
            if mask is not None:
                s = jnp.where(mask, s, NEG_INF)
            m_old = m_scr[r, :]
            m_new = jnp.maximum(m_old, jnp.max(s, axis=-1, keepdims=True))
            alpha = jnp.exp2((m_old - m_new) * EXP2_SCALE)
            p = jnp.exp2((s - _lanes(m_new, size)) * EXP2_SCALE)
            l_scr[r, :] = alpha * l_scr[r, :] + jnp.sum(p, axis=-1, keepdims=True)
            acc_scr[r, :] = acc_scr[r, :] * _lanes(alpha, KV_LORA) + jnp.dot(
                p.astype(BF16), v, preferred_element_type=F32)
            m_scr[r, :] = m_new

    if causal:
        q_start = pl.program_id(1) * tq
        n_wide = q_start // (2 * kb)

        def wide_body(j, carry):
            block(pl.multiple_of(j * 2 * kb, 2 * kb), 2 * kb, None)
            return carry

        lax.fori_loop(0, n_wide, wide_body, 0)
        n_main = q_start // kb

        @pl.when(n_main > 2 * n_wide)
        def _():
            block(pl.multiple_of(n_wide * 2 * kb, 2 * kb), kb, None)

        n_before = (q_start - n_main * kb) // tq
        for r in range(kb // tq):
            width = (r + 1) * tq

            @pl.when(n_before == r)
            def _(r=r, width=width):
                tok = jnp.bitwise_and(lax.broadcasted_iota(jnp.int32, (part_rows, width), 0), tq - 1)
                col = lax.broadcasted_iota(jnp.int32, (part_rows, width), 1)
                block(pl.multiple_of(n_main * kb, kb), width, col < r * tq + (tok // CHUNK + 1) * CHUNK)
    else:
        for j in range(lk // kb):
            block(j * kb, kb, None)

    o = (acc_scr[...] / _lanes(l_scr[...], KV_LORA)).astype(BF16)
    for h in range(MLA_HEADS):
        o_ref[:, h * V_HEAD:(h + 1) * V_HEAD] = jnp.dot(
            o[h * tq:(h + 1) * tq], wuv_ref[h], preferred_element_type=F32).astype(o_ref.dtype)


def attention(q, k, wuv, *, batch, lq, lk, tq, kb, causal):
    nq = lq // tq
    rows = MLA_HEADS * tq
    kern = functools.partial(_attention_kernel, tq=tq, kb=kb, causal=causal, lk=lk,
                             n_split=4 if causal else 1)
    return pl.pallas_call(
        kern,
        grid=(batch, nq),
        in_specs=[pl.BlockSpec((MLA_HEADS, tq, QK_PAD), lambda b, i: (0, b * nq + i, 0)),
                  pl.BlockSpec((lk, QK_PAD), lambda b, i: (b, 0)),
                  pl.BlockSpec(wuv.shape, lambda b, i: (0, 0, 0))],
        out_specs=pl.BlockSpec((tq, MLA_WIDTH), lambda b, i: (b * nq + i, 0)),
        out_shape=jax.ShapeDtypeStruct((batch * lq, MLA_WIDTH), BF16),
        scratch_shapes=[pltpu.VMEM((rows, LANE), F32), pltpu.VMEM((rows, LANE), F32),
                        pltpu.VMEM((rows, KV_LORA), F32)],
        compiler_params=_params(("arbitrary", "arbitrary")),
        name="attention_causal" if causal else "attention_full",
    )(q, k, wuv)


def _sigmoid(x):
    return 1.0 / (1.0 + jnp.exp(-x))


def _hgrn_kernel(hq_ref, hf_ref, hi_ref, hg_ref, lb_ref, nw_ref, s0_ref, o_ref, sfin_ref,
                 st_scr, oacc_scr, g_scr, q_scr, k_scr, *, chunk, n_chunks, n_seq):
    t = pl.program_id(2)

    @pl.when(t == 0)
    def _():
        for sq in range(n_seq):
            st_scr[sq] = s0_ref[sq, 0].T

    lb = lb_ref[...]
    lb_floor = jnp.maximum(lb, LB_FLOOR)
    one_m_lb = 1.0 - lb
    tri = (lax.broadcasted_iota(jnp.int32, (chunk, chunk), 0)
           >= lax.broadcasted_iota(jnp.int32, (chunk, chunk), 1))
    tri_bf16 = jnp.where(tri, 1.0, 0.0).astype(BF16)
    ones = jnp.ones((HG_DK, HG_DV), BF16)
    row_id = lax.broadcasted_iota(jnp.int32, (chunk, HG_DK), 0)
    n_sub = chunk // HG_SUB

    hf = hf_ref[...].astype(F32)
    e = jnp.exp(-jnp.abs(hf))
    big = 1.0 / (1.0 + e)
    small = e * big
    pos = hf >= 0.0
    log_f = jnp.log(lb_floor + one_m_lb * jnp.where(pos, big, small))
    k_scr[...] = one_m_lb * jnp.where(pos, small, big)
    hq = hq_ref[...].astype(F32)
    q_scr[...] = hq * _sigmoid(hq)
    lf_a = log_f.astype(BF16)
    rest = log_f - lf_a.astype(F32)
    lf_b = rest.astype(BF16)
    lf_c = (rest - lf_b.astype(F32)).astype(BF16)
    g_min = None
    for c in range(n_seq * n_chunks):
        sl = slice(c * chunk, (c + 1) * chunk)
        parts = jnp.dot(tri_bf16, jnp.concatenate([lf_a[sl], lf_b[sl], lf_c[sl]], axis=1),
                        preferred_element_type=F32)
        g = parts[:, 0:HG_DK] + parts[:, HG_DK:2 * HG_DK] + parts[:, 2 * HG_DK:3 * HG_DK]
        g_scr[sl, :] = g
        g_end = g[chunk - 1:chunk]
        g_min = g_end if g_min is None else jnp.minimum(g_min, g_end)
    factorable = jnp.min(g_min) >= -HG_SAFE_EXPONENT

    def state_step(c, g, k, v16):
        st = st_scr.at[c // n_chunks]
        g_last = g[chunk - 1:chunk]
        kh = k * jnp.exp(g_last - g)
        st[...] = st[...] * jnp.exp(g_last) + lax.dot_general(
            v16, kh.astype(BF16), (((0,), (0,)), ((), ())), preferred_element_type=F32)

    def chunk_factored(c):
        sl = slice(c * chunk, (c + 1) * chunk)
        g, q, k = g_scr[sl, :], q_scr[sl, :], k_scr[sl, :]
        v16 = hi_ref[sl, :]
        qd = (q * jnp.exp(g)).astype(BF16)
        kd = (k * jnp.exp(-g)).astype(BF16)
        a = lax.dot_general(qd, kd, (((1,), (1,)), ((), ())), preferred_element_type=F32)
        a = jnp.where(tri, a, 0.0).astype(BF16)
        oacc_scr[sl, :] = (lax.dot_general(qd, st_scr[c // n_chunks].astype(BF16), (((1,), (1,)), ((), ())),
                                           preferred_element_type=F32)
                           + jnp.dot(a, v16, preferred_element_type=F32))
        state_step(c, g, k, v16)

    def chunk_exact(c):
        sl = slice(c * chunk, (c + 1) * chunk)
        oacc = oacc_scr.at[sl, :]
        g, q, k = g_scr[sl, :], q_scr[sl, :], k_scr[sl, :]
        v16 = hi_ref[sl, :]
        v = v16.astype(F32)

        oacc[...] = lax.dot_general((q * jnp.exp(g)).astype(BF16), st_scr[c // n_chunks].astype(BF16),
                                    (((1,), (1,)), ((), ())), preferred_element_type=F32)
        for i in range(1, n_sub):
            r = i * HG_SUB
            g_edge = g[r - 1:r]
            qt = q[r:r + HG_SUB] * jnp.exp(g[r:r + HG_SUB] - g_edge)
            kt = k[0:r] * jnp.exp(g_edge - g[0:r])
            a = lax.dot_general(qt.astype(BF16), kt.astype(BF16), (((1,), (1,)), ((), ())),
                                preferred_element_type=F32)
            oacc[r:r + HG_SUB, :] += jnp.dot(a.astype(BF16), v16[0:r], preferred_element_type=F32)
        for grp in range(chunk // 8):
            r0 = grp * 8
            r1 = (r0 // HG_SUB + 1) * HG_SUB
            n = r1 - r0
            parts = []
            for s in range(r0, r0 + 8):
                d = jnp.where(row_id[r0:r1] >= s, g[r0:r1] - g[s:s + 1], NEG_INF)
                parts.append(jnp.exp(d) * q[r0:r1] * k[s:s + 1])
            sums = jnp.dot(jnp.concatenate(parts, axis=0).astype(BF16), ones, preferred_element_type=F32)
            upd = sums[0:n] * v[r0:r0 + 1]
            for u in range(1, 8):
                upd = upd + sums[u * n:(u + 1) * n] * v[r0 + u:r0 + u + 1]
            oacc[r0:r1, :] += upd
        state_step(c, g, k, v16)

    @pl.when(factorable)
    def _():
        for c in range(n_seq * n_chunks):
            chunk_factored(c)

    @pl.when(jnp.logical_not(factorable))
    def _():
        for c in range(n_seq * n_chunks):
            chunk_exact(c)

    hg = hg_ref[...].astype(F32)
    o_ref[...] = (_rms(oacc_scr[...], nw_ref[...]) * (hg * _sigmoid(hg))).astype(o_ref.dtype)

    @pl.when(t == pl.num_programs(2) - 1)
    def _():
        for sq in range(n_seq):
            sfin_ref[sq, 0] = st_scr[sq].T


def hgrn(p, lb, nw, s0, *, batch, seq, chunk, tb, n_seq=1):
    nt = max(seq // tb, 1)
    assert tb == n_seq * seq or (n_seq == 1 and seq % tb == 0)
    kern = functools.partial(_hgrn_kernel, chunk=chunk, n_chunks=tb // (n_seq * chunk), n_seq=n_seq)

    def col(base):
        return lambda b, h, t: (b * nt + t, base // LANE + h)

    return pl.pallas_call(
        kern,
        grid=(batch // n_seq, HG_HEADS, nt),
        in_specs=[pl.BlockSpec((tb, LANE), col(COL_HQ)),
                  pl.BlockSpec((tb, LANE), col(COL_HF)),
                  pl.BlockSpec((tb, LANE), col(COL_HI)),
                  pl.BlockSpec((tb, LANE), col(COL_HG)),
                  pl.BlockSpec((1, HG_DK), lambda b, h, t: (0, h)),
                  pl.BlockSpec((1, HG_DV), lambda b, h, t: (0, 0)),
                  pl.BlockSpec((n_seq, 1, HG_DK, HG_DV), lambda b, h, t: (b, h, 0, 0))],
        out_specs=[pl.BlockSpec((tb, HG_DV), lambda b, h, t: (b * nt + t, h)),
                   pl.BlockSpec((n_seq, 1, HG_DK, HG_DV), lambda b, h, t: (b, h, 0, 0))],
        out_shape=[jax.ShapeDtypeStruct((batch * seq, HG_WIDTH), BF16),
                   jax.ShapeDtypeStruct((batch, HG_HEADS, HG_DK, HG_DV), F32)],
        scratch_shapes=[pltpu.VMEM((n_seq, HG_DV, HG_DK), F32), pltpu.VMEM((tb, HG_DV), F32),
                        pltpu.VMEM((tb, HG_DK), F32), pltpu.VMEM((tb, HG_DK), F32),
                        pltpu.VMEM((tb, HG_DK), F32)],
        compiler_params=_params(("arbitrary", "arbitrary", "arbitrary")),
        name="hgrn",
    )(p, p, p, p, lb, nw, s0)


def _out_proj_kernel(x_ref, a_ref, b_ref, w_ref, o_ref):
    o_ref[...] = (x_ref[...]
                  + jnp.dot(a_ref[...], w_ref[0:MLA_WIDTH, :], preferred_element_type=F32)
                  + jnp.dot(b_ref[...], w_ref[MLA_WIDTH:MLA_WIDTH + HG_WIDTH, :], preferred_element_type=F32))


def out_proj(x, a, b, w, *, tm):
    n, d = x.shape
    return pl.pallas_call(
        _out_proj_kernel,
        grid=(n // tm,),
        in_specs=[pl.BlockSpec((tm, d), lambda i: (i, 0)),
                  pl.BlockSpec((tm, MLA_WIDTH), lambda i: (i, 0)),
                  pl.BlockSpec((tm, HG_WIDTH), lambda i: (i, 0)),
                  pl.BlockSpec(w.shape, lambda i: (0, 0))],
        out_specs=pl.BlockSpec((tm, d), lambda i: (i, 0)),
        out_shape=jax.ShapeDtypeStruct((n, d), F32),
        compiler_params=_params(("arbitrary",)),
        name="out_proj",
    )(x, a, b, w)


def _silu(x):
    return x * _sigmoid(x)


def _swiglu_chunk(h, wg, wu, wd):
    a = _silu(jnp.dot(h, wg, preferred_element_type=F32)) * jnp.dot(h, wu, preferred_element_type=F32)
    return jnp.dot(a.astype(BF16), wd, preferred_element_type=F32)


def _ffn_kernel(x_ref, nw_ref, wg_hbm, wu_hbm, wd_hbm, o_ref, h_scr, wg_buf, wu_buf, wd_buf, sem, *, nf):
    i = pl.program_id(0)
    n_tiles = pl.num_programs(0)

    tf = wd_buf.shape[1]

    def chunk_copies(j, slot):
        cols = pl.ds(pl.multiple_of(j * tf, tf), tf)
        return (pltpu.make_async_copy(wg_hbm.at[:, cols], wg_buf.at[slot], sem.at[0, slot]),
                pltpu.make_async_copy(wu_hbm.at[:, cols], wu_buf.at[slot], sem.at[1, slot]),
                pltpu.make_async_copy(wd_hbm.at[cols, :], wd_buf.at[slot], sem.at[2, slot]))

    @pl.when(i == 0)
    def _():
        for cp in chunk_copies(0, 0):
            cp.start()

    h_scr[...] = _rms(x_ref[...], nw_ref[...]).astype(BF16)
    o_ref[...] = x_ref[...]

    def body(j, carry):
        slot = (i * nf + j) % 2

        @pl.when(j + 1 < nf)
        def _():
            for cp in chunk_copies(j + 1, 1 - slot):
                cp.start()

        @pl.when((j + 1 == nf) & (i + 1 < n_tiles))
        def _():
            for cp in chunk_copies(0, 1 - slot):
                cp.start()

        for cp in chunk_copies(j, slot):
            cp.wait()
        o_ref[...] += _swiglu_chunk(h_scr[...], wg_buf[slot], wu_buf[slot], wd_buf[slot])
        return carry

    lax.fori_loop(0, nf, body, 0)


def ffn(x, nw, wg, wu, wd, *, tm, tf):
    n, d = x.shape
    nf = wg.shape[1] // tf
    any_spec = pl.BlockSpec(memory_space=pl.ANY)
    return pl.pallas_call(
        functools.partial(_ffn_kernel, nf=nf),
        grid=(n // tm,),
        in_specs=[pl.BlockSpec((tm, d), lambda i: (i, 0)),
                  pl.BlockSpec((1, d), lambda i: (0, 0)), any_spec, any_spec, any_spec],
        out_specs=pl.BlockSpec((tm, d), lambda i: (i, 0)),
        out_shape=jax.ShapeDtypeStruct((n, d), F32),
        scratch_shapes=[pltpu.VMEM((tm, d), BF16),
                        pltpu.VMEM((2, d, tf), BF16), pltpu.VMEM((2, d, tf), BF16),
                        pltpu.VMEM((2, tf, d), BF16), pltpu.SemaphoreType.DMA((3, 2))],
        compiler_params=_params(("arbitrary",)),
        name="ffn",
    )(x, nw, wg, wu, wd)


def _moe_route_kernel(x_ref, nw_ref, wr_ref, hp_ref, e1_ref, e2_ref, w1_ref, w2_ref, r1_ref, r2_ref,
                      cnt_ref, cnt_scr):
    @pl.when(pl.program_id(0) == 0)
    def _():
        cnt_scr[...] = jnp.zeros_like(cnt_scr)

    tm = x_ref.shape[0]
    half = x_ref.shape[1] // 2
    h = _rms(x_ref[...], nw_ref[...])
    hb = h.astype(BF16)
    h_lo = (h - hb.astype(F32)).astype(BF16)
    wr = wr_ref[...]
    w_hi = wr.astype(BF16)
    w_lo = (wr - w_hi.astype(F32)).astype(BF16)
    logits = jnp.dot(hb, w_hi, preferred_element_type=F32) + (
        jnp.dot(hb, w_lo, preferred_element_type=F32) + jnp.dot(h_lo, w_hi, preferred_element_type=F32))
    ids = lax.broadcasted_iota(jnp.int32, logits.shape, 1).astype(F32)
    m1 = jnp.max(logits, axis=-1, keepdims=True)
    i1 = jnp.min(jnp.where(logits == m1, ids, float(N_EXPERTS)), axis=-1, keepdims=True)
    rest = jnp.where(ids == i1, -jnp.inf, logits)
    m2 = jnp.max(rest, axis=-1, keepdims=True)
    i2 = jnp.min(jnp.where(rest == m2, ids, float(N_EXPERTS)), axis=-1, keepdims=True)
    t = jnp.exp(m2 - m1)
    e1_ref[...] = i1.astype(jnp.int32)
    e2_ref[...] = i2.astype(jnp.int32)
    w1_ref[...] = 1.0 / (1.0 + t)
    w2_ref[...] = t / (1.0 + t)
    oh1 = (ids == i1).astype(F32)
    oh2 = (ids == i2).astype(F32)
    both = oh1 + oh2
    before = (lax.broadcasted_iota(jnp.int32, (tm, tm), 0)
              > lax.broadcasted_iota(jnp.int32, (tm, tm), 1)).astype(BF16)
    prefix = jnp.dot(before, both.astype(BF16), preferred_element_type=F32) + cnt_scr[...]
    r1_ref[...] = jnp.sum(prefix * oh1, axis=-1, keepdims=True).astype(jnp.int32)
    r2_ref[...] = jnp.sum(prefix * oh2, axis=-1, keepdims=True).astype(jnp.int32)
    cnt_scr[...] += jnp.sum(both, axis=0, keepdims=True)
    cnt_ref[...] = cnt_scr[...].astype(jnp.int32)
    hb32 = hb.astype(F32)
    hi = pltpu.bitcast(hb32[:, 0:half], jnp.uint32)
    lo = pltpu.bitcast(hb32[:, half:2 * half], jnp.uint32)
    hp_ref[...] = hi | (lo >> 16)


def moe_route(x, nw, wr, *, tm):
    n, d = x.shape
    col = lambda dt: jax.ShapeDtypeStruct((n, 1), dt)
    col_spec = pl.BlockSpec((tm, 1), lambda i: (i, 0))
    return pl.pallas_call(
        _moe_route_kernel,
        grid=(n // tm,),
        in_specs=[pl.BlockSpec((tm, d), lambda i: (i, 0)),
                  pl.BlockSpec((1, d), lambda i: (0, 0)),
                  pl.BlockSpec(wr.shape, lambda i: (0, 0))],
        out_specs=[pl.BlockSpec((tm, d // 2), lambda i: (i, 0))] + [col_spec] * 6
                  + [pl.BlockSpec((1, N_EXPERTS), lambda i: (0, 0))],
        out_shape=[jax.ShapeDtypeStruct((n, d // 2), jnp.uint32), col(jnp.int32), col(jnp.int32),
                   col(F32), col(F32), col(jnp.int32), col(jnp.int32),
                   jax.ShapeDtypeStruct((1, N_EXPERTS), jnp.int32)],
        scratch_shapes=[pltpu.VMEM((1, N_EXPERTS), F32)],
        compiler_params=_params(("arbitrary",)),
        name="moe_route",
    )(x, nw, wr)


def _row_copy(src, src_row, dst, dst_row, sem):
    return pltpu.make_async_copy(src.at[pl.ds(src_row, 1), :], dst.at[pl.ds(dst_row, 1), :], sem)


def _moe_scatter_kernel(d1_ref, d2_ref, hp_ref, xs_in_hbm, xs_hbm, sem, *, tm):
    del xs_in_hbm

    def issue(r, carry):
        _row_copy(hp_ref, r, xs_hbm, d1_ref[0, 0, r], sem).start()
        _row_copy(hp_ref, r, xs_hbm, d2_ref[0, 0, r], sem).start()
        return carry

    lax.fori_loop(0, tm, issue, 0, unroll=8)

    def drain(r, carry):
        _row_copy(hp_ref, r, xs_hbm, d1_ref[0, 0, r], sem).wait()
        _row_copy(hp_ref, r, xs_hbm, d2_ref[0, 0, r], sem).wait()
        return carry

    lax.fori_loop(0, tm, drain, 0, unroll=8)


def moe_scatter(hp, d1, d2, xs, *, tm):
    n = hp.shape[0]
    idx_spec = pl.BlockSpec((1, 1, tm), lambda i: (i, 0, 0), memory_space=pltpu.SMEM)
    any_spec = pl.BlockSpec(memory_space=pl.ANY)
    return pl.pallas_call(
        functools.partial(_moe_scatter_kernel, tm=tm),
        grid=(n // tm,),
        in_specs=[idx_spec, idx_spec, pl.BlockSpec((tm, hp.shape[1]), lambda i: (i, 0)), any_spec],
        out_specs=any_spec,
        out_shape=jax.ShapeDtypeStruct(xs.shape, xs.dtype),
        scratch_shapes=[pltpu.SemaphoreType.DMA(())],
        input_output_aliases={3: 0},
        compiler_params=_params(("arbitrary",)),
        name="moe_scatter",
    )(d1.reshape(n // tm, 1, tm), d2.reshape(n // tm, 1, tm), hp, xs)


def _moe_expert_kernel(te_ref, na_ref, xs_ref, wg_hbm, wu_hbm, wd_hbm, ys_ref, h_scr, wg_buf, wu_buf, wd_buf, sem,
                       *, nf):
    i = pl.program_id(0)
    n_act = na_ref[0]
    active = i < n_act
    half = xs_ref.shape[1]

    tf = wd_buf.shape[1]

    def chunk_copies(tile, j, slot):
        e = te_ref[tile]
        cols = pl.ds(pl.multiple_of(j * tf, tf), tf)
        return (pltpu.make_async_copy(wg_hbm.at[e, :, cols], wg_buf.at[slot], sem.at[0, slot]),
                pltpu.make_async_copy(wu_hbm.at[e, :, cols], wu_buf.at[slot], sem.at[1, slot]),
                pltpu.make_async_copy(wd_hbm.at[e, cols, :], wd_buf.at[slot], sem.at[2, slot]))

    @pl.when(active)
    def _():
        @pl.when(i == 0)
        def _():
            for cp in chunk_copies(0, 0, 0):
                cp.start()

        packed = xs_ref[...]
        h_scr[:, 0:half] = pltpu.bitcast(packed & jnp.uint32(0xFFFF0000), F32).astype(BF16)
        h_scr[:, half:2 * half] = pltpu.bitcast(packed << 16, F32).astype(BF16)
        ys_ref[...] = jnp.zeros_like(ys_ref)

        def body(j, carry):
            slot = (i * nf + j) % 2

            @pl.when(j + 1 < nf)
            def _():
                for cp in chunk_copies(i, j + 1, 1 - slot):
                    cp.start()

            @pl.when((j + 1 == nf) & (i + 1 < n_act))
            def _():
                for cp in chunk_copies(i + 1, 0, 1 - slot):
                    cp.start()

            for cp in chunk_copies(i, j, slot):
                cp.wait()
            ys_ref[...] += _swiglu_chunk(h_scr[...], wg_buf[slot], wu_buf[slot], wd_buf[slot])
            return carry

        lax.fori_loop(0, nf, body, 0)

    @pl.when(jnp.logical_not(active))
    def _():
        ys_ref[...] = jnp.zeros_like(ys_ref)


def moe_experts(xs, tile_expert, n_active, wg, wu, wd, *, tm, tf):
    rows, half = xs.shape
    d = 2 * half
    nf = wg.shape[2] // tf
    any_spec = pl.BlockSpec(memory_space=pl.ANY)
    return pl.pallas_call(
        functools.partial(_moe_expert_kernel, nf=nf),
        grid_spec=pltpu.PrefetchScalarGridSpec(
            num_scalar_prefetch=2,
            grid=(rows // tm,),
            in_specs=[pl.BlockSpec((tm, half), lambda i, te, na: (i, 0)), any_spec, any_spec, any_spec],
            out_specs=pl.BlockSpec((tm, d), lambda i, te, na: (i, 0)),
            scratch_shapes=[pltpu.VMEM((tm, d), BF16),
                            pltpu.VMEM((2, d, tf), BF16), pltpu.VMEM((2, d, tf), BF16),
                            pltpu.VMEM((2, tf, d), BF16), pltpu.SemaphoreType.DMA((3, 2))]),
        out_shape=jax.ShapeDtypeStruct((rows, d), F32),
        compiler_params=_params(("arbitrary",)),
        name="moe_experts",
    )(tile_expert, n_active, xs, wg, wu, wd)


def _moe_combine_kernel(d1_ref, d2_ref, d1_next_ref, d2_next_ref, x_ref, w1_ref, w2_ref, fw_ref, ys_hbm, o_ref,
                        a_scr, b_scr, sem, *, tm):
    i = pl.program_id(0)
    slot = i % 2

    def gather(r1_ref, r2_ref, s, wait):
        def row(r, carry):
            for src_ref, dst in ((r1_ref, a_scr), (r2_ref, b_scr)):
                cp = _row_copy(ys_hbm, src_ref[0, 0, r], dst.at[s], r, sem.at[s])
                cp.wait() if wait else cp.start()
            return carry

        lax.fori_loop(0, tm, row, 0, unroll=8)

    @pl.when(i == 0)
    def _():
        gather(d1_ref, d2_ref, 0, wait=False)

    @pl.when(i + 1 < pl.num_programs(0))
    def _():
        gather(d1_next_ref, d2_next_ref, 1 - slot, wait=False)

    gather(d1_ref, d2_ref, slot, wait=True)
    y = x_ref[...] + (w1_ref[...] * a_scr[slot] + w2_ref[...] * b_scr[slot])
    o_ref[...] = _rms(y, fw_ref[...])


def moe_combine(x, ys, d1, d2, w1, w2, fw, *, tm):
    n, d = x.shape
    nt = n // tm
    idx_spec = pl.BlockSpec((1, 1, tm), lambda i: (i, 0, 0), memory_space=pltpu.SMEM)
    next_spec = pl.BlockSpec((1, 1, tm), lambda i: (jnp.minimum(i + 1, nt - 1), 0, 0), memory_space=pltpu.SMEM)
    col_spec = pl.BlockSpec((tm, 1), lambda i: (i, 0))
    d1, d2 = d1.reshape(nt, 1, tm), d2.reshape(nt, 1, tm)
    return pl.pallas_call(
        functools.partial(_moe_combine_kernel, tm=tm),
        grid=(nt,),
        in_specs=[idx_spec, idx_spec, next_spec, next_spec, pl.BlockSpec((tm, d), lambda i: (i, 0)),
                  col_spec, col_spec, pl.BlockSpec((1, d), lambda i: (0, 0)), pl.BlockSpec(memory_space=pl.ANY)],
        out_specs=pl.BlockSpec((tm, d), lambda i: (i, 0)),
        out_shape=jax.ShapeDtypeStruct((n, d), F32),
        scratch_shapes=[pltpu.VMEM((2, tm, d), F32), pltpu.VMEM((2, tm, d), F32),
                        pltpu.SemaphoreType.DMA((2,))],
        compiler_params=_params(("arbitrary",)),
        name="moe_combine",
    )(d1, d2, d1, d2, x, w1, w2, fw, ys)


def moe_final(streams, nw, wr, wg, wu, wd, fw):
    n_exp = wg.shape[0]
    d = streams[0].shape[1]
    tms = [min(TOKEN_TILE, x.shape[0]) for x in streams]
    tile = EXPERT_ROW_TILE
    routed = [moe_route(x, nw, wr, tm=tm) for x, tm in zip(streams, tms)]
    counts = [r[7][0] for r in routed]
    total = sum(counts)
    padded = (total + tile - 1) // tile * tile
    ends = jnp.cumsum(padded)
    starts = ends - padded
    n_rows = sum(2 * x.shape[0] for x in streams) + n_exp * tile
    n_tiles = n_rows // tile
    tile_expert = jnp.minimum(
        jnp.sum(jnp.arange(n_tiles, dtype=jnp.int32)[:, None] * tile >= ends[None, :], axis=1), n_exp - 1
    ).astype(jnp.int32)
    n_active = (ends[n_exp - 1:] // tile).astype(jnp.int32)

    def slot_rows(e, rank, offset):
        table = starts + offset
        return (jnp.sum(jnp.where(e == jnp.arange(n_exp)[None, :], table[None, :], 0), axis=1, keepdims=True)
                + rank).astype(jnp.int32)

    xs = jnp.zeros((n_rows, d // 2), jnp.uint32)
    dests = []
    offset = jnp.zeros((n_exp,), jnp.int32)
    for (hp, e1, e2, w1, w2, r1, r2, cnt), tm in zip(routed, tms):
        d1, d2 = slot_rows(e1, r1, offset), slot_rows(e2, r2, offset)
        xs = moe_scatter(hp, d1, d2, xs, tm=tm)
        dests.append((d1, d2))
        offset = offset + cnt[0]
    ys = moe_experts(xs, tile_expert, n_active, wg, wu, wd, tm=tile, tf=EXPERT_CHUNK)
    return [moe_combine(x, ys, d1, d2, r[3], r[4], fw, tm=tm)
            for x, r, (d1, d2), tm in zip(streams, routed, dests, tms)]


def _swap_halves(t):
    half = t.shape[-1] // 2
    return jnp.concatenate([t[..., half:], t[..., :half]], axis=-1)


def _prep_w_in(w):
    d = w.shape[0]
    k_pe = w[:, Q_LORA + KV_LORA:Q_LORA + KV_LORA + QK_ROPE]
    pad = jnp.zeros((d, LANE - QK_ROPE), w.dtype)
    return jnp.concatenate([w[:, :Q_LORA + KV_LORA], w[:, Q_LORA + KV_LORA + QK_ROPE:],
                            k_pe, pad, _swap_halves(k_pe), pad], axis=1).astype(BF16)


def _prep_w_uq(w):
    w = w.reshape(Q_LORA, MLA_HEADS, QK_NOPE + QK_ROPE)
    nope = w[..., :QK_NOPE]
    rope = w[..., QK_NOPE:]
    pad = jnp.zeros((Q_LORA, MLA_HEADS, LANE - QK_ROPE), w.dtype)
    parts = [nope, jnp.concatenate([rope, pad], -1), jnp.concatenate([_swap_halves(rope), pad], -1)]
    return jnp.concatenate([t.reshape(Q_LORA, MLA_HEADS * LANE) for t in parts], axis=1).astype(BF16)


def _rope_tables(pos):
    half = QK_ROPE // 2
    inv_freq = jnp.exp(jnp.arange(half, dtype=F32) * (-math.log(ROPE_THETA) / half))
    ang = pos.astype(F32)[:, None] * inv_freq[None, :]
    cos, sin = jnp.cos(ang), jnp.sin(ang)
    pad = jnp.zeros((pos.shape[0], LANE - QK_ROPE), F32)
    return jnp.concatenate([cos, cos, pad], -1), jnp.concatenate([-sin, sin, pad], -1)


def _row(v):
    return v.reshape(1, -1)


def kernel(x_prompt, x_sample, cache_ckv, cache_kpe, state_hgrn, attn_norm_w, w_in, q_norm_w, kv_norm_w,
           w_uq, w_uk, w_uv, hg_lower_bounds, hg_norm_w, w_out, ffn_norm_w, w_gate, w_up, w_down,
           w_router, we_gate, we_up, we_down, final_norm_w):
    depth = w_in.shape[0]
    b_p, l_p, d = x_prompt.shape
    b_s, l_s, _ = x_sample.shape
    past = cache_ckv.shape[2]
    n_p, n_s = b_p * l_p, b_s * l_s
    assert depth == 2 and d == D_MODEL

    probs = jax.nn.softmax(hg_lower_bounds.astype(F32), axis=0)
    lower_bounds = jnp.cumsum(probs, axis=0) - probs[0:1]

    cos_p, sin_p = _rope_tables(jnp.arange(l_p))
    cos_s, sin_s = _rope_tables(past + jnp.arange(l_s))
    cos_s, sin_s = jnp.tile(cos_s, (b_s, 1)), jnp.tile(sin_s, (b_s, 1))
    zero_state = jnp.zeros((b_p, HG_HEADS, HG_DK, HG_DV), F32)

    xp = x_prompt.reshape(n_p, d)
    xs = x_sample.reshape(n_s, d)
    outs = {k: [] for k in ("ckv_p", "kpe_p", "st_p", "ckv_s", "kpe_s", "st_s")}

    for l in range(depth):
        w_in_l = _prep_w_in(w_in[l])
        w_uq_l = _prep_w_uq(w_uq[l])
        w_uk_l = jnp.transpose(w_uk[l], (1, 2, 0)).astype(BF16)
        w_uv_l = jnp.transpose(w_uv[l], (1, 0, 2)).astype(BF16)
        w_out_l = w_out[l].astype(BF16)
        lb_l = _row(lower_bounds[l])
        mixers = []
        for (x, n, batch, seq, cos, sin) in ((xp, n_p, b_p, l_p, cos_p, sin_p),
                                             (xs, n_s, b_s, l_s, cos_s, sin_s)):
            tm = min(TOKEN_TILE, n)
            p = norm_matmul(x, _row(attn_norm_w[l]), w_in_l, tm=tm, tn=IN_PROJ_COLS)
            q, kb, ckv, kpe = mla_prep(p, _row(q_norm_w[l]), _row(kv_norm_w[l]), w_uq_l, w_uk_l,
                                       cos, sin, tm=tm)
            if x is xp:
                o_mla = attention(q, kb, w_uv_l, batch=batch, lq=seq, lk=seq,
                                  tq=min(ATTN_Q_TILE, seq), kb=ATTN_KEY_BLOCK, causal=True)
                o_hg, st = hgrn(p, lb_l, _row(hg_norm_w[l]), zero_state, batch=batch, seq=seq,
                                chunk=CHUNK, tb=min(HG_BLOCK, seq))
            else:
                past_k = jnp.concatenate(
                    [cache_ckv[l].astype(BF16), cache_kpe[l].astype(BF16),
                     jnp.zeros((batch, past, LANE - QK_ROPE), BF16)], axis=-1)
                keys = jnp.concatenate([past_k, kb.reshape(batch, seq, QK_PAD)], axis=1)
                lk = past + seq
                o_mla = attention(q, keys.reshape(batch * lk, QK_PAD), w_uv_l, batch=batch, lq=seq, lk=lk,
                                  tq=seq, kb=lk, causal=False)
                n_seq = math.gcd(batch, 8)
                o_hg, st = hgrn(p, lb_l, _row(hg_norm_w[l]), state_hgrn[l], batch=batch, seq=seq,
                                chunk=seq, tb=n_seq * seq, n_seq=n_seq)
            x1 = out_proj(x, o_mla, o_hg, w_out_l, tm=tm)
            mixers.append((x1, ckv.reshape(batch, seq, KV_LORA), kpe.reshape(batch, seq, QK_ROPE), st))
        (xp, ckv_p, kpe_p, st_p), (xs, ckv_s, kpe_s, st_s) = mixers
        for name, val in (("ckv_p", ckv_p), ("kpe_p", kpe_p), ("st_p", st_p),
                          ("ckv_s", ckv_s), ("kpe_s", kpe_s), ("st_s", st_s)):
            outs[name].append(val)

        i = l // 2
        if l % 2 == 0:
            wg, wu, wd = w_gate[i].astype(BF16), w_up[i].astype(BF16), w_down[i].astype(BF16)
            xp = ffn(xp, _row(ffn_norm_w[l]), wg, wu, wd, tm=min(FFN_ROW_TILE, n_p), tf=FFN_CHUNK)
            xs = ffn(xs, _row(ffn_norm_w[l]), wg, wu, wd, tm=min(FFN_ROW_TILE, n_s), tf=FFN_CHUNK)
        else:
            wg, wu, wd = we_gate[i].astype(BF16), we_up[i].astype(BF16), we_down[i].astype(BF16)
            xp, xs = moe_final([xp, xs], _row(ffn_norm_w[l]), w_router[i], wg, wu, wd, _row(final_norm_w))

    return (xp.reshape(b_p, l_p, d), xs.reshape(b_s, l_s, d),
            jnp.stack(outs["ckv_p"]), jnp.stack(outs["kpe_p"]), jnp.stack(outs["st_p"]),
            jnp.stack(outs["ckv_s"]), jnp.stack(outs["kpe_s"]), jnp.stack(outs["st_s"]))
```

```python
import functools
import math

import jax
import jax.numpy as jnp
from jax import lax
from jax.experimental import pallas as pl
from jax.experimental.pallas import tpu as pltpu

F32 = jnp.float32
BF16 = jnp.bfloat16

D_MODEL = 2048
CHUNK = 64
RMS_EPS = 1e-6
NEG_INF = -1e30
LB_FLOOR = 1e-30

MLA_HEADS = 8
V_HEAD = 128
QK_NOPE = 128
QK_ROPE = 64
Q_LORA = 512
KV_LORA = 256
ROPE_THETA = 10000.0
ATTN_SCALE = (QK_NOPE + QK_ROPE) ** -0.5
EXP2_SCALE = ATTN_SCALE * math.log2(math.e)
MLA_WIDTH = MLA_HEADS * V_HEAD

HG_HEADS = 8
HG_DK = 128
HG_DV = 128
HG_KEY = HG_HEADS * HG_DK
HG_WIDTH = HG_HEADS * HG_DV
HG_SUB = 16
HG_SAFE_EXPONENT = 80.0
HG_BLOCK = 4096

N_EXPERTS = 8

LANE = 128
QK_PAD = KV_LORA + LANE

COL_CQ = 0
COL_CKV = Q_LORA
COL_HQ = Q_LORA + KV_LORA
COL_HF = COL_HQ + HG_KEY
COL_HI = COL_HF + HG_KEY
COL_HG = COL_HI + HG_WIDTH
COL_KPE = COL_HG + HG_WIDTH
IN_EXT = COL_KPE + 2 * LANE

VMEM_LIMIT = 56 * 1024 * 1024
TOKEN_TILE = 512
EXPERT_ROW_TILE = 1024
EXPERT_CHUNK = 256
FFN_ROW_TILE = 1024
FFN_CHUNK = 256
IN_PROJ_COLS = 1024
ATTN_Q_TILE = 256
ATTN_KEY_BLOCK = 512


def _params(semantics):
    return pltpu.CompilerParams(dimension_semantics=semantics, vmem_limit_bytes=VMEM_LIMIT)


def _rms(x, w):
    return x * lax.rsqrt(jnp.mean(x * x, axis=-1, keepdims=True) + RMS_EPS) * w


def _norm_matmul_kernel(x_ref, nw_ref, w_hbm, o_ref, w_vmem, sem, *, tn):
    @pl.when(pl.program_id(0) == 0)
    def _():
        cp = pltpu.make_async_copy(w_hbm, w_vmem, sem)
        cp.start()
        cp.wait()

    h = _rms(x_ref[...], nw_ref[...]).astype(BF16)
    for c in range(o_ref.shape[1] // tn):
        o_ref[:, c * tn:(c + 1) * tn] = jnp.dot(
            h, w_vmem[:, c * tn:(c + 1) * tn], preferred_element_type=F32).astype(o_ref.dtype)


def norm_matmul(x, nw, w, *, tm, tn):
    n, d = x.shape
    cols = w.shape[1]
    return pl.pallas_call(
        functools.partial(_norm_matmul_kernel, tn=tn),
        grid=(n // tm,),
        in_specs=[pl.BlockSpec((tm, d), lambda i: (i, 0)),
                  pl.BlockSpec((1, d), lambda i: (0, 0)),
                  pl.BlockSpec(memory_space=pl.ANY)],
        out_specs=pl.BlockSpec((tm, cols), lambda i: (i, 0)),
        out_shape=jax.ShapeDtypeStruct((n, cols), BF16),
        scratch_shapes=[pltpu.VMEM((d, cols), BF16), pltpu.SemaphoreType.DMA(())],
        compiler_params=_params(("arbitrary",)),
        name="in_proj",
    )(x, nw, w)


def _mla_prep_kernel(cq_ref, ckv_ref, kp_ref, qnw_ref, kvnw_ref, wuq_ref, wuk_ref, cos_ref, sin_ref,
                     q_ref, kb_ref, ckv_out_ref, kpe_out_ref):
    cos = cos_ref[...]
    sin = sin_ref[...]
    cqn = _rms(cq_ref[...].astype(F32), qnw_ref[...]).astype(BF16)
    q = jnp.dot(cqn, wuq_ref[...], preferred_element_type=F32)
    hw = MLA_HEADS * LANE
    for h in range(MLA_HEADS):
        sl = slice(h * LANE, (h + 1) * LANE)
        q_lat = jnp.dot(q[:, sl].astype(BF16), wuk_ref[h], preferred_element_type=F32)
        rope = q[:, hw + h * LANE:hw + (h + 1) * LANE] * cos + q[:, 2 * hw + h * LANE:2 * hw + (h + 1) * LANE] * sin
        q_ref[h, :, 0:KV_LORA] = q_lat.astype(BF16)
        q_ref[h, :, KV_LORA:QK_PAD] = rope.astype(BF16)
    ckv = _rms(ckv_ref[...].astype(F32), kvnw_ref[...])
    ckv_out_ref[...] = ckv
    kb_ref[:, 0:KV_LORA] = ckv.astype(BF16)
    kp = kp_ref[...].astype(F32)
    kpe = kp[:, 0:LANE] * cos + kp[:, LANE:2 * LANE] * sin
    kpe_out_ref[...] = kpe[:, 0:QK_ROPE]
    kb_ref[:, KV_LORA:QK_PAD] = kpe.astype(BF16)


def mla_prep(p, qnw, kvnw, wuq, wuk, cos, sin, *, tm):
    n = p.shape[0]
    n_pos = cos.shape[0] // tm
    return pl.pallas_call(
        _mla_prep_kernel,
        grid=(n // tm,),
        in_specs=[pl.BlockSpec((tm, Q_LORA), lambda i: (i, COL_CQ // Q_LORA)),
                  pl.BlockSpec((tm, KV_LORA), lambda i: (i, COL_CKV // KV_LORA)),
                  pl.BlockSpec((tm, 2 * LANE), lambda i: (i, COL_KPE // (2 * LANE))),
                  pl.BlockSpec((1, Q_LORA), lambda i: (0, 0)),
                  pl.BlockSpec((1, KV_LORA), lambda i: (0, 0)),
                  pl.BlockSpec(wuq.shape, lambda i: (0, 0)),
                  pl.BlockSpec(wuk.shape, lambda i: (0, 0, 0)),
                  pl.BlockSpec((tm, LANE), lambda i: (i % n_pos, 0)),
                  pl.BlockSpec((tm, LANE), lambda i: (i % n_pos, 0))],
        out_specs=[pl.BlockSpec((MLA_HEADS, tm, QK_PAD), lambda i: (0, i, 0)),
                   pl.BlockSpec((tm, QK_PAD), lambda i: (i, 0)),
                   pl.BlockSpec((tm, KV_LORA), lambda i: (i, 0)),
                   pl.BlockSpec((tm, QK_ROPE), lambda i: (i, 0))],
        out_shape=[jax.ShapeDtypeStruct((MLA_HEADS, n, QK_PAD), BF16),
                   jax.ShapeDtypeStruct((n, QK_PAD), BF16),
                   jax.ShapeDtypeStruct((n, KV_LORA), F32),
                   jax.ShapeDtypeStruct((n, QK_ROPE), F32)],
        compiler_params=_params(("arbitrary",)),
        name="mla_prep",
    )(p, p, p, qnw, kvnw, wuq, wuk, cos, sin)


def _lanes(x, n):
    if n == LANE:
        return x
    if n % LANE == 0:
        return jnp.concatenate([x] * (n // LANE), axis=1)
    return jnp.broadcast_to(x[:, 0:1], (x.shape[0], n))


def _attention_kernel(q_ref, k_ref, wuv_ref, o_ref, m_scr, l_scr, acc_scr, *, tq, kb, causal, lk, n_split):
    rows = MLA_HEADS * tq
    part_rows = rows // n_split
    q = q_ref[...].reshape(rows, QK_PAD)
    m_scr[...] = jnp.full((rows, LANE), NEG_INF, F32)
    l_scr[...] = jnp.zeros((rows, LANE), F32)
    acc_scr[...] = jnp.zeros((rows, KV_LORA), F32)

    def block(start, size, mask):
        k = k_ref[pl.ds(start, size), :]
        v = k[:, 0:KV_LORA]
        scores = [lax.dot_general(q[part * part_rows:(part + 1) * part_rows], k, (((1,), (1,)), ((), ())),
                                  preferred_element_type=F32) for part in range(n_split)]
        for part in range(n_split):
            r = slice(part * part_rows, (part + 1) * part_rows)
            s = scores[part]
            if mask is not None:
                s = jnp.where(mask, s, NEG_INF)
            m_old = m_scr[r, :]
            m_new = jnp.maximum(m_old, jnp.max(s, axis=-1, keepdims=True))
            alpha = jnp.exp2((m_old - m_new) * EXP2_SCALE)
            p = jnp.exp2((s - _lanes(m_new, size)) * EXP2_SCALE)
            l_scr[r, :] = alpha * l_scr[r, :] + jnp.sum(p, axis=-1, keepdims=True)
            acc_scr[r, :] = acc_scr[r, :] * _lanes(alpha, KV_LORA) + jnp.dot(
                p.astype(BF16), v, preferred_element_type=F32)
            m_scr[r, :] = m_new

    if causal:
        q_start = pl.program_id(1) * tq
        n_wide = q_start // (2 * kb)

        def wide_body(j, carry):
            block(pl.multiple_of(j * 2 * kb, 2 * kb), 2 * kb, None)
            return carry

        lax.fori_loop(0, n_wide, wide_body, 0)
        n_main = q_start // kb

        @pl.when(n_main > 2 * n_wide)
        def _():
            block(pl.multiple_of(n_wide * 2 * kb, 2 * kb), kb, None)

        n_before = (q_start - n_main * kb) // tq
        for r in range(kb // tq):
            width = (r + 1) * tq

            @pl.when(n_before == r)
            def _(r=r, width=width):
                tok = jnp.bitwise_and(lax.broadcasted_iota(jnp.int32, (part_rows, width), 0), tq - 1)
                col = lax.broadcasted_iota(jnp.int32, (part_rows, width), 1)
                block(pl.multiple_of(n_main * kb, kb), width, col < r * tq + (tok // CHUNK + 1) * CHUNK)
    else:
        for j in range(lk // kb):
            block(j * kb, kb, None)

    o = (acc_scr[...] / _lanes(l_scr[...], KV_LORA)).astype(BF16)
    for h in range(MLA_HEADS):
        o_ref[:, h * V_HEAD:(h + 1) * V_HEAD] = jnp.dot(
            o[h * tq:(h + 1) * tq], wuv_ref[h], preferred_element_type=F32).astype(o_ref.dtype)


def attention(q, k, wuv, *, batch, lq, lk, tq, kb, causal):
    nq = lq // tq
    rows = MLA_HEADS * tq
    kern = functools.partial(_attention_kernel, tq=tq, kb=kb, causal=causal, lk=lk,
                             n_split=4 if causal else 1)
    return pl.pallas_call(
        kern,
        grid=(batch, nq),
        in_specs=[pl.BlockSpec((MLA_HEADS, tq, QK_PAD), lambda b, i: (0, b * nq + i, 0)),
                  pl.BlockSpec((lk, QK_PAD), lambda b, i: (b, 0)),
                  pl.BlockSpec(wuv.shape, lambda b, i: (0, 0, 0))],
        out_specs=pl.BlockSpec((tq, MLA_WIDTH), lambda b, i: (b * nq + i, 0)),
        out_shape=jax.ShapeDtypeStruct((batch * lq, MLA_WIDTH), BF16),
        scratch_shapes=[pltpu.VMEM((rows, LANE), F32), pltpu.VMEM((rows, LANE), F32),
                        pltpu.VMEM((rows, KV_LORA), F32)],
        compiler_params=_params(("arbitrary", "arbitrary")),
        name="attention_causal" if causal else "attention_full",
    )(q, k, wuv)


def _sigmoid(x):
    return 1.0 / (1.0 + jnp.exp(-x))


def _hgrn_kernel(hq_ref, hf_ref, hi_ref, hg_ref, lb_ref, nw_ref, s0_ref, o_ref, sfin_ref,
                 st_scr, oacc_scr, g_scr, q_scr, k_scr, *, chunk, n_chunks, n_seq):
    t = pl.program_id(2)

    @pl.when(t == 0)
    def _():
        for sq in range(n_seq):
            st_scr[sq] = s0_ref[sq, 0].T

    lb = lb_ref[...]
    lb_floor = jnp.maximum(lb, LB_FLOOR)
    one_m_lb = 1.0 - lb
    tri = (lax.broadcasted_iota(jnp.int32, (chunk, chunk), 0)
           >= lax.broadcasted_iota(jnp.int32, (chunk, chunk), 1))
    tri_bf16 = jnp.where(tri, 1.0, 0.0).astype(BF16)
    ones = jnp.ones((HG_DK, HG_DV), BF16)
    row_id = lax.broadcasted_iota(jnp.int32, (chunk, HG_DK), 0)
    n_sub = chunk // HG_SUB

    hf = hf_ref[...].astype(F32)
    e = jnp.exp(-jnp.abs(hf))
    big = 1.0 / (1.0 + e)
    small = e * big
    pos = hf >= 0.0
    log_f = jnp.log(lb_floor + one_m_lb * jnp.where(pos, big, small))
    k_scr[...] = one_m_lb * jnp.where(pos, small, big)
    hq = hq_ref[...].astype(F32)
    q_scr[...] = hq * _sigmoid(hq)
    lf_a = log_f.astype(BF16)
    rest = log_f - lf_a.astype(F32)
    lf_b = rest.astype(BF16)
    lf_c = (rest - lf_b.astype(F32)).astype(BF16)
    g_min = None
    for c in range(n_seq * n_chunks):
        sl = slice(c * chunk, (c + 1) * chunk)
        parts = jnp.dot(tri_bf16, jnp.concatenate([lf_a[sl], lf_b[sl], lf_c[sl]], axis=1),
                        preferred_element_type=F32)
        g = parts[:, 0:HG_DK] + parts[:, HG_DK:2 * HG_DK] + parts[:, 2 * HG_DK:3 * HG_DK]
        g_scr[sl, :] = g
        g_end = g[chunk - 1:chunk]
        g_min = g_end if g_min is None else jnp.minimum(g_min, g_end)
    factorable = jnp.min(g_min) >= -HG_SAFE_EXPONENT

    def state_step(c, g, k, v16):
        st = st_scr.at[c // n_chunks]
        g_last = g[chunk - 1:chunk]
        kh = k * jnp.exp(g_last - g)
        st[...] = st[...] * jnp.exp(g_last) + lax.dot_general(
            v16, kh.astype(BF16), (((0,), (0,)), ((), ())), preferred_element_type=F32)

    def chunk_factored(c):
        sl = slice(c * chunk, (c + 1) * chunk)
        g, q, k = g_scr[sl, :], q_scr[sl, :], k_scr[sl, :]
        v16 = hi_ref[sl, :]
        qd = (q * jnp.exp(g)).astype(BF16)
        kd = (k * jnp.exp(-g)).astype(BF16)
        a = lax.dot_general(qd, kd, (((1,), (1,)), ((), ())), preferred_element_type=F32)
        a = jnp.where(tri, a, 0.0).astype(BF16)
        oacc_scr[sl, :] = (lax.dot_general(qd, st_scr[c // n_chunks].astype(BF16), (((1,), (1,)), ((), ())),
                                           preferred_element_type=F32)
                           + jnp.dot(a, v16, preferred_element_type=F32))
        state_step(c, g, k, v16)

    def chunk_exact(c):
        sl = slice(c * chunk, (c + 1) * chunk)
        oacc = oacc_scr.at[sl, :]
        g, q, k = g_scr[sl, :], q_scr[sl, :], k_scr[sl, :]
        v16 = hi_ref[sl, :]
        v = v16.astype(F32)

        oacc[...] = lax.dot_general((q * jnp.exp(g)).astype(BF16), st_scr[c // n_chunks].astype(BF16),
                                    (((1,), (1,)), ((), ())), preferred_element_type=F32)
        for i in range(1, n_sub):
            r = i * HG_SUB
            g_edge = g[r - 1:r]
            qt = q[r:r + HG_SUB] * jnp.exp(g[r:r + HG_SUB] - g_edge)
            kt = k[0:r] * jnp.exp(g_edge - g[0:r])
            a = lax.dot_general(qt.astype(BF16), kt.astype(BF16), (((1,), (1,)), ((), ())),
                                preferred_element_type=F32)
            oacc[r:r + HG_SUB, :] += jnp.dot(a.astype(BF16), v16[0:r], preferred_element_type=F32)
        for grp in range(chunk // 8):
            r0 = grp * 8
            r1 = (r0 // HG_SUB + 1) * HG_SUB
            n = r1 - r0
            parts = []
            for s in range(r0, r0 + 8):
                d = jnp.where(row_id[r0:r1] >= s, g[r0:r1] - g[s:s + 1], NEG_INF)
                parts.append(jnp.exp(d) * q[r0:r1] * k[s:s + 1])
            sums = jnp.dot(jnp.concatenate(parts, axis=0).astype(BF16), ones, preferred_element_type=F32)
            upd = sums[0:n] * v[r0:r0 + 1]
            for u in range(1, 8):
                upd = upd + sums[u * n:(u + 1) * n] * v[r0 + u:r0 + u + 1]
            oacc[r0:r1, :] += upd
        state_step(c, g, k, v16)

    @pl.when(factorable)
    def _():
        for c in range(n_seq * n_chunks):
            chunk_factored(c)

    @pl.when(jnp.logical_not(factorable))
    def _():
        for c in range(n_seq * n_chunks):
            chunk_exact(c)

    hg = hg_ref[...].astype(F32)
    o_ref[...] = (_rms(oacc_scr[...], nw_ref[...]) * (hg * _sigmoid(hg))).astype(o_ref.dtype)

    @pl.when(t == pl.num_programs(2) - 1)
    def _():
        for sq in range(n_seq):
            sfin_ref[sq, 0] = st_scr[sq].T


def hgrn(p, lb, nw, s0, *, batch, seq, chunk, tb, n_seq=1):
    nt = max(seq // tb, 1)
    assert tb == n_seq * seq or (n_seq == 1 and seq % tb == 0)
    kern = functools.partial(_hgrn_kernel, chunk=chunk, n_chunks=tb // (n_seq * chunk), n_seq=n_seq)

    def col(base):
        return lambda b, h, t: (b * nt + t, base // LANE + h)

    return pl.pallas_call(
        kern,
        grid=(batch // n_seq, HG_HEADS, nt),
        in_specs=[pl.BlockSpec((tb, LANE), col(COL_HQ)),
                  pl.BlockSpec((tb, LANE), col(COL_HF)),
                  pl.BlockSpec((tb, LANE), col(COL_HI)),
                  pl.BlockSpec((tb, LANE), col(COL_HG)),
                  pl.BlockSpec((1, HG_DK), lambda b, h, t: (0, h)),
                  pl.BlockSpec((1, HG_DV), lambda b, h, t: (0, 0)),
                  pl.BlockSpec((n_seq, 1, HG_DK, HG_DV), lambda b, h, t: (b, h, 0, 0))],
        out_specs=[pl.BlockSpec((tb, HG_DV), lambda b, h, t: (b * nt + t, h)),
                   pl.BlockSpec((n_seq, 1, HG_DK, HG_DV), lambda b, h, t: (b, h, 0, 0))],
        out_shape=[jax.ShapeDtypeStruct((batch * seq, HG_WIDTH), BF16),
                   jax.ShapeDtypeStruct((batch, HG_HEADS, HG_DK, HG_DV), F32)],
        scratch_shapes=[pltpu.VMEM((n_seq, HG_DV, HG_DK), F32), pltpu.VMEM((tb, HG_DV), F32),
                        pltpu.VMEM((tb, HG_DK), F32), pltpu.VMEM((tb, HG_DK), F32),
                        pltpu.VMEM((tb, HG_DK), F32)],
        compiler_params=_params(("arbitrary", "arbitrary", "arbitrary")),
        name="hgrn",
    )(p, p, p, p, lb, nw, s0)


def _out_proj_kernel(x_ref, a_ref, b_ref, w_ref, o_ref):
    o_ref[...] = (x_ref[...]
                  + jnp.dot(a_ref[...], w_ref[0:MLA_WIDTH, :], preferred_element_type=F32)
                  + jnp.dot(b_ref[...], w_ref[MLA_WIDTH:MLA_WIDTH + HG_WIDTH, :], preferred_element_type=F32))


def out_proj(x, a, b, w, *, tm):
    n, d = x.shape
    return pl.pallas_call(
        _out_proj_kernel,
        grid=(n // tm,),
        in_specs=[pl.BlockSpec((tm, d), lambda i: (i, 0)),
                  pl.BlockSpec((tm, MLA_WIDTH), lambda i: (i, 0)),
                  pl.BlockSpec((tm, HG_WIDTH), lambda i: (i, 0)),
                  pl.BlockSpec(w.shape, lambda i: (0, 0))],
        out_specs=pl.BlockSpec((tm, d), lambda i: (i, 0)),
        out_shape=jax.ShapeDtypeStruct((n, d), F32),
        compiler_params=_params(("arbitrary",)),
        name="out_proj",
    )(x, a, b, w)


def _silu(x):
    return x * _sigmoid(x)


def _swiglu_chunk(h, wg, wu, wd):
    a = _silu(jnp.dot(h, wg, preferred_element_type=F32)) * jnp.dot(h, wu, preferred_element_type=F32)
    return jnp.dot(a.astype(BF16), wd, preferred_element_type=F32)


def _ffn_kernel(x_ref, nw_ref, wg_hbm, wu_hbm, wd_hbm, o_ref, h_scr, wg_buf, wu_buf, wd_buf, sem, *, nf):
    i = pl.program_id(0)
    n_tiles = pl.num_programs(0)

    tf = wd_buf.shape[1]

    def chunk_copies(j, slot):
        cols = pl.ds(pl.multiple_of(j * tf, tf), tf)
        return (pltpu.make_async_copy(wg_hbm.at[:, cols], wg_buf.at[slot], sem.at[0, slot]),
                pltpu.make_async_copy(wu_hbm.at[:, cols], wu_buf.at[slot], sem.at[1, slot]),
                pltpu.make_async_copy(wd_hbm.at[cols, :], wd_buf.at[slot], sem.at[2, slot]))

    @pl.when(i == 0)
    def _():
        for cp in chunk_copies(0, 0):
            cp.start()

    h_scr[...] = _rms(x_ref[...], nw_ref[...]).astype(BF16)
    o_ref[...] = x_ref[...]

    def body(j, carry):
        slot = (i * nf + j) % 2

        @pl.when(j + 1 < nf)
        def _():
            for cp in chunk_copies(j + 1, 1 - slot):
                cp.start()

        @pl.when((j + 1 == nf) & (i + 1 < n_tiles))
        def _():
            for cp in chunk_copies(0, 1 - slot):
                cp.start()

        for cp in chunk_copies(j, slot):
            cp.wait()
        o_ref[...] += _swiglu_chunk(h_scr[...], wg_buf[slot], wu_buf[slot], wd_buf[slot])
        return carry

    lax.fori_loop(0, nf, body, 0)


def ffn(x, nw, wg, wu, wd, *, tm, tf):
    n, d = x.shape
    nf = wg.shape[1] // tf
    any_spec = pl.BlockSpec(memory_space=pl.ANY)
    return pl.pallas_call(
        functools.partial(_ffn_kernel, nf=nf),
        grid=(n // tm,),
        in_specs=[pl.BlockSpec((tm, d), lambda i: (i, 0)),
                  pl.BlockSpec((1, d), lambda i: (0, 0)), any_spec, any_spec, any_spec],
        out_specs=pl.BlockSpec((tm, d), lambda i: (i, 0)),
        out_shape=jax.ShapeDtypeStruct((n, d), F32),
        scratch_shapes=[pltpu.VMEM((tm, d), BF16),
                        pltpu.VMEM((2, d, tf), BF16), pltpu.VMEM((2, d, tf), BF16),
                        pltpu.VMEM((2, tf, d), BF16), pltpu.SemaphoreType.DMA((3, 2))],
        compiler_params=_params(("arbitrary",)),
        name="ffn",
    )(x, nw, wg, wu, wd)


def _moe_route_kernel(x_ref, nw_ref, wr_ref, hp_ref, e1_ref, e2_ref, w1_ref, w2_ref, r1_ref, r2_ref,
                      cnt_ref, cnt_scr):
    @pl.when(pl.program_id(0) == 0)
    def _():
        cnt_scr[...] = jnp.zeros_like(cnt_scr)

    tm = x_ref.shape[0]
    half = x_ref.shape[1] // 2
    h = _rms(x_ref[...], nw_ref[...])
    hb = h.astype(BF16)
    h_lo = (h - hb.astype(F32)).astype(BF16)
    wr = wr_ref[...]
    w_hi = wr.astype(BF16)
    w_lo = (wr - w_hi.astype(F32)).astype(BF16)
    logits = jnp.dot(hb, w_hi, preferred_element_type=F32) + (
        jnp.dot(hb, w_lo, preferred_element_type=F32) + jnp.dot(h_lo, w_hi, preferred_element_type=F32))
    ids = lax.broadcasted_iota(jnp.int32, logits.shape, 1).astype(F32)
    m1 = jnp.max(logits, axis=-1, keepdims=True)
    i1 = jnp.min(jnp.where(logits == m1, ids, float(N_EXPERTS)), axis=-1, keepdims=True)
    rest = jnp.where(ids == i1, -jnp.inf, logits)
    m2 = jnp.max(rest, axis=-1, keepdims=True)
    i2 = jnp.min(jnp.where(rest == m2, ids, float(N_EXPERTS)), axis=-1, keepdims=True)
    t = jnp.exp(m2 - m1)
    e1_ref[...] = i1.astype(jnp.int32)
    e2_ref[...] = i2.astype(jnp.int32)
    w1_ref[...] = 1.0 / (1.0 + t)
    w2_ref[...] = t / (1.0 + t)
    oh1 = (ids == i1).astype(F32)
    oh2 = (ids == i2).astype(F32)
    both = oh1 + oh2
    before = (lax.broadcasted_iota(jnp.int32, (tm, tm), 0)
              > lax.broadcasted_iota(jnp.int32, (tm, tm), 1)).astype(BF16)
    prefix = jnp.dot(before, both.astype(BF16), preferred_element_type=F32) + cnt_scr[...]
    r1_ref[...] = jnp.sum(prefix * oh1, axis=-1, keepdims=True).astype(jnp.int32)
    r2_ref[...] = jnp.sum(prefix * oh2, axis=-1, keepdims=True).astype(jnp.int32)
    cnt_scr[...] += jnp.sum(both, axis=0, keepdims=True)
    cnt_ref[...] = cnt_scr[...].astype(jnp.int32)
    hb32 = hb.astype(F32)
    hi = pltpu.bitcast(hb32[:, 0:half], jnp.uint32)
    lo = pltpu.bitcast(hb32[:, half:2 * half], jnp.uint32)
    hp_ref[...] = hi | (lo >> 16)


def moe_route(x, nw, wr, *, tm):
    n, d = x.shape
    col = lambda dt: jax.ShapeDtypeStruct((n, 1), dt)
    col_spec = pl.BlockSpec((tm, 1), lambda i: (i, 0))
    return pl.pallas_call(
        _moe_route_kernel,
        grid=(n // tm,),
        in_specs=[pl.BlockSpec((tm, d), lambda i: (i, 0)),
                  pl.BlockSpec((1, d), lambda i: (0, 0)),
                  pl.BlockSpec(wr.shape, lambda i: (0, 0))],
        out_specs=[pl.BlockSpec((tm, d // 2), lambda i: (i, 0))] + [col_spec] * 6
                  + [pl.BlockSpec((1, N_EXPERTS), lambda i: (0, 0))],
        out_shape=[jax.ShapeDtypeStruct((n, d // 2), jnp.uint32), col(jnp.int32), col(jnp.int32),
                   col(F32), col(F32), col(jnp.int32), col(jnp.int32),
                   jax.ShapeDtypeStruct((1, N_EXPERTS), jnp.int32)],
        scratch_shapes=[pltpu.VMEM((1, N_EXPERTS), F32)],
        compiler_params=_params(("arbitrary",)),
        name="moe_route",
    )(x, nw, wr)


def _row_copy(src, src_row, dst, dst_row, sem):
    return pltpu.make_async_copy(src.at[pl.ds(src_row, 1), :], dst.at[pl.ds(dst_row, 1), :], sem)


def _moe_scatter_kernel(d1_ref, d2_ref, hp_ref, xs_in_hbm, xs_hbm, sem, *, tm):
    del xs_in_hbm

    def issue(r, carry):
        _row_copy(hp_ref, r, xs_hbm, d1_ref[0, 0, r], sem).start(priority=0)
        _row_copy(hp_ref, r, xs_hbm, d2_ref[0, 0, r], sem).start(priority=1)
        return carry

    lax.fori_loop(0, tm, issue, 0, unroll=8)

    def drain(r, carry):
        _row_copy(hp_ref, r, xs_hbm, d1_ref[0, 0, r], sem).wait()
        _row_copy(hp_ref, r, xs_hbm, d2_ref[0, 0, r], sem).wait()
        return carry

    lax.fori_loop(0, tm, drain, 0, unroll=8)


def moe_scatter(hp, d1, d2, xs, *, tm):
    n = hp.shape[0]
    idx_spec = pl.BlockSpec((1, 1, tm), lambda i: (i, 0, 0), memory_space=pltpu.SMEM)
    any_spec = pl.BlockSpec(memory_space=pl.ANY)
    return pl.pallas_call(
        functools.partial(_moe_scatter_kernel, tm=tm),
        grid=(n // tm,),
        in_specs=[idx_spec, idx_spec, pl.BlockSpec((tm, hp.shape[1]), lambda i: (i, 0)), any_spec],
        out_specs=any_spec,
        out_shape=jax.ShapeDtypeStruct(xs.shape, xs.dtype),
        scratch_shapes=[pltpu.SemaphoreType.DMA(())],
        input_output_aliases={3: 0},
        compiler_params=_params(("arbitrary",)),
        name="moe_scatter",
    )(d1.reshape(n // tm, 1, tm), d2.reshape(n // tm, 1, tm), hp, xs)


def _moe_expert_kernel(te_ref, na_ref, xs_ref, wg_hbm, wu_hbm, wd_hbm, ys_ref, h_scr, wg_buf, wu_buf, wd_buf, sem,
                       *, nf):
    i = pl.program_id(0)
    n_act = na_ref[0]
    active = i < n_act
    half = xs_ref.shape[1]

    tf = wd_buf.shape[1]

    def chunk_copies(tile, j, slot):
        e = te_ref[tile]
        cols = pl.ds(pl.multiple_of(j * tf, tf), tf)
        return (pltpu.make_async_copy(wg_hbm.at[e, :, cols], wg_buf.at[slot], sem.at[0, slot]),
                pltpu.make_async_copy(wu_hbm.at[e, :, cols], wu_buf.at[slot], sem.at[1, slot]),
                pltpu.make_async_copy(wd_hbm.at[e, cols, :], wd_buf.at[slot], sem.at[2, slot]))

    @pl.when(active)
    def _():
        @pl.when(i == 0)
        def _():
            for cp in chunk_copies(0, 0, 0):
                cp.start()

        packed = xs_ref[...]
        h_scr[:, 0:half] = pltpu.bitcast(packed & jnp.uint32(0xFFFF0000), F32).astype(BF16)
        h_scr[:, half:2 * half] = pltpu.bitcast(packed << 16, F32).astype(BF16)
        ys_ref[...] = jnp.zeros_like(ys_ref)

        def body(j, carry):
            slot = (i * nf + j) % 2

            @pl.when(j + 1 < nf)
            def _():
                for cp in chunk_copies(i, j + 1, 1 - slot):
                    cp.start()

            @pl.when((j + 1 == nf) & (i + 1 < n_act))
            def _():
                for cp in chunk_copies(i + 1, 0, 1 - slot):
                    cp.start()

            for cp in chunk_copies(i, j, slot):
                cp.wait()
            ys_ref[...] += _swiglu_chunk(h_scr[...], wg_buf[slot], wu_buf[slot], wd_buf[slot])
            return carry

        lax.fori_loop(0, nf, body, 0)

    @pl.when(jnp.logical_not(active))
    def _():
        ys_ref[...] = jnp.zeros_like(ys_ref)


def moe_experts(xs, tile_expert, n_active, wg, wu, wd, *, tm, tf):
    rows, half = xs.shape
    d = 2 * half
    nf = wg.shape[2] // tf
    any_spec = pl.BlockSpec(memory_space=pl.ANY)
    return pl.pallas_call(
        functools.partial(_moe_expert_kernel, nf=nf),
        grid_spec=pltpu.PrefetchScalarGridSpec(
            num_scalar_prefetch=2,
            grid=(rows // tm,),
            in_specs=[pl.BlockSpec((tm, half), lambda i, te, na: (i, 0)), any_spec, any_spec, any_spec],
            out_specs=pl.BlockSpec((tm, d), lambda i, te, na: (i, 0)),
            scratch_shapes=[pltpu.VMEM((tm, d), BF16),
                            pltpu.VMEM((2, d, tf), BF16), pltpu.VMEM((2, d, tf), BF16),
                            pltpu.VMEM((2, tf, d), BF16), pltpu.SemaphoreType.DMA((3, 2))]),
        out_shape=jax.ShapeDtypeStruct((rows, d), F32),
        compiler_params=_params(("arbitrary",)),
        name="moe_experts",
    )(tile_expert, n_active, xs, wg, wu, wd)


def _moe_combine_kernel(d1_ref, d2_ref, d1_next_ref, d2_next_ref, x_ref, w1_ref, w2_ref, fw_ref, ys_hbm, o_ref,
                        a_scr, b_scr, sem, *, tm):
    i = pl.program_id(0)
    slot = i % 2

    def gather(r1_ref, r2_ref, s, wait):
        def row(r, carry):
            for prio, (src_ref, dst) in enumerate(((r1_ref, a_scr), (r2_ref, b_scr))):
                cp = _row_copy(ys_hbm, src_ref[0, 0, r], dst.at[s], r, sem.at[s])
                cp.wait() if wait else cp.start(priority=prio)
            return carry

        lax.fori_loop(0, tm, row, 0, unroll=8)

    @pl.when(i == 0)
    def _():
        gather(d1_ref, d2_ref, 0, wait=False)

    @pl.when(i + 1 < pl.num_programs(0))
    def _():
        gather(d1_next_ref, d2_next_ref, 1 - slot, wait=False)

    gather(d1_ref, d2_ref, slot, wait=True)
    y = x_ref[...] + (w1_ref[...] * a_scr[slot] + w2_ref[...] * b_scr[slot])
    o_ref[...] = _rms(y, fw_ref[...])


def moe_combine(x, ys, d1, d2, w1, w2, fw, *, tm):
    n, d = x.shape
    nt = n // tm
    idx_spec = pl.BlockSpec((1, 1, tm), lambda i: (i, 0, 0), memory_space=pltpu.SMEM)
    next_spec = pl.BlockSpec((1, 1, tm), lambda i: (jnp.minimum(i + 1, nt - 1), 0, 0), memory_space=pltpu.SMEM)
    col_spec = pl.BlockSpec((tm, 1), lambda i: (i, 0))
    d1, d2 = d1.reshape(nt, 1, tm), d2.reshape(nt, 1, tm)
    return pl.pallas_call(
        functools.partial(_moe_combine_kernel, tm=tm),
        grid=(nt,),
        in_specs=[idx_spec, idx_spec, next_spec, next_spec, pl.BlockSpec((tm, d), lambda i: (i, 0)),
                  col_spec, col_spec, pl.BlockSpec((1, d), lambda i: (0, 0)), pl.BlockSpec(memory_space=pl.ANY)],
        out_specs=pl.BlockSpec((tm, d), lambda i: (i, 0)),
        out_shape=jax.ShapeDtypeStruct((n, d), F32),
        scratch_shapes=[pltpu.VMEM((2, tm, d), F32), pltpu.VMEM((2, tm, d), F32),
                        pltpu.SemaphoreType.DMA((2,))],
        compiler_params=_params(("arbitrary",)),
        name="moe_combine",
    )(d1, d2, d1, d2, x, w1, w2, fw, ys)


def moe_final(streams, nw, wr, wg, wu, wd, fw):
    n_exp = wg.shape[0]
    d = streams[0].shape[1]
    tms = [min(TOKEN_TILE, x.shape[0]) for x in streams]
    tile = EXPERT_ROW_TILE
    routed = [moe_route(x, nw, wr, tm=tm) for x, tm in zip(streams, tms)]
    counts = [r[7][0] for r in routed]
    total = sum(counts)
    padded = (total + tile - 1) // tile * tile
    ends = jnp.cumsum(padded)
    starts = ends - padded
    n_rows = sum(2 * x.shape[0] for x in streams) + n_exp * tile
    n_tiles = n_rows // tile
    tile_expert = jnp.minimum(
        jnp.sum(jnp.arange(n_tiles, dtype=jnp.int32)[:, None] * tile >= ends[None, :], axis=1), n_exp - 1
    ).astype(jnp.int32)
    n_active = (ends[n_exp - 1:] // tile).astype(jnp.int32)

    def slot_rows(e, rank, offset):
        table = starts + offset
        return (jnp.sum(jnp.where(e == jnp.arange(n_exp)[None, :], table[None, :], 0), axis=1, keepdims=True)
                + rank).astype(jnp.int32)

    xs = jnp.zeros((n_rows, d // 2), jnp.uint32)
    dests = []
    offset = jnp.zeros((n_exp,), jnp.int32)
    for (hp, e1, e2, w1, w2, r1, r2, cnt), tm in zip(routed, tms):
        d1, d2 = slot_rows(e1, r1, offset), slot_rows(e2, r2, offset)
        xs = moe_scatter(hp, d1, d2, xs, tm=tm)
        dests.append((d1, d2))
        offset = offset + cnt[0]
    ys = moe_experts(xs, tile_expert, n_active, wg, wu, wd, tm=tile, tf=EXPERT_CHUNK)
    return [moe_combine(x, ys, d1, d2, r[3], r[4], fw, tm=tm)
            for x, r, (d1, d2), tm in zip(streams, routed, dests, tms)]


def _swap_halves(t):
    half = t.shape[-1] // 2
    return jnp.concatenate([t[..., half:], t[..., :half]], axis=-1)


def _prep_w_in(w):
    d = w.shape[0]
    k_pe = w[:, Q_LORA + KV_LORA:Q_LORA + KV_LORA + QK_ROPE]
    pad = jnp.zeros((d, LANE - QK_ROPE), w.dtype)
    return jnp.concatenate([w[:, :Q_LORA + KV_LORA], w[:, Q_LORA + KV_LORA + QK_ROPE:],
                            k_pe, pad, _swap_halves(k_pe), pad], axis=1).astype(BF16)


def _prep_w_uq(w):
    w = w.reshape(Q_LORA, MLA_HEADS, QK_NOPE + QK_ROPE)
    nope = w[..., :QK_NOPE]
    rope = w[..., QK_NOPE:]
    pad = jnp.zeros((Q_LORA, MLA_HEADS, LANE - QK_ROPE), w.dtype)
    parts = [nope, jnp.concatenate([rope, pad], -1), jnp.concatenate([_swap_halves(rope), pad], -1)]
    return jnp.concatenate([t.reshape(Q_LORA, MLA_HEADS * LANE) for t in parts], axis=1).astype(BF16)


def _rope_tables(pos):
    half = QK_ROPE // 2
    inv_freq = jnp.exp(jnp.arange(half, dtype=F32) * (-math.log(ROPE_THETA) / half))
    ang = pos.astype(F32)[:, None] * inv_freq[None, :]
    cos, sin = jnp.cos(ang), jnp.sin(ang)
    pad = jnp.zeros((pos.shape[0], LANE - QK_ROPE), F32)
    return jnp.concatenate([cos, cos, pad], -1), jnp.concatenate([-sin, sin, pad], -1)


def _row(v):
    return v.reshape(1, -1)


def kernel(x_prompt, x_sample, cache_ckv, cache_kpe, state_hgrn, attn_norm_w, w_in, q_norm_w, kv_norm_w,
           w_uq, w_uk, w_uv, hg_lower_bounds, hg_norm_w, w_out, ffn_norm_w, w_gate, w_up, w_down,
           w_router, we_gate, we_up, we_down, final_norm_w):
    depth = w_in.shape[0]
    b_p, l_p, d = x_prompt.shape
    b_s, l_s, _ = x_sample.shape
    past = cache_ckv.shape[2]
    n_p, n_s = b_p * l_p, b_s * l_s
    assert depth == 2 and d == D_MODEL

    probs = jax.nn.softmax(hg_lower_bounds.astype(F32), axis=0)
    lower_bounds = jnp.cumsum(probs, axis=0) - probs[0:1]

    cos_p, sin_p = _rope_tables(jnp.arange(l_p))
    cos_s, sin_s = _rope_tables(past + jnp.arange(l_s))
    cos_s, sin_s = jnp.tile(cos_s, (b_s, 1)), jnp.tile(sin_s, (b_s, 1))
    zero_state = jnp.zeros((b_p, HG_HEADS, HG_DK, HG_DV), F32)

    xp = x_prompt.reshape(n_p, d)
    xs = x_sample.reshape(n_s, d)
    outs = {k: [] for k in ("ckv_p", "kpe_p", "st_p", "ckv_s", "kpe_s", "st_s")}

    for l in range(depth):
        w_in_l = _prep_w_in(w_in[l])
        w_uq_l = _prep_w_uq(w_uq[l])
        w_uk_l = jnp.transpose(w_uk[l], (1, 2, 0)).astype(BF16)
        w_uv_l = jnp.transpose(w_uv[l], (1, 0, 2)).astype(BF16)
        w_out_l = w_out[l].astype(BF16)
        lb_l = _row(lower_bounds[l])
        mixers = []
        for (x, n, batch, seq, cos, sin) in ((xp, n_p, b_p, l_p, cos_p, sin_p),
                                             (xs, n_s, b_s, l_s, cos_s, sin_s)):
            tm = min(TOKEN_TILE, n)
            p = norm_matmul(x, _row(attn_norm_w[l]), w_in_l, tm=tm, tn=IN_PROJ_COLS)
            q, kb, ckv, kpe = mla_prep(p, _row(q_norm_w[l]), _row(kv_norm_w[l]), w_uq_l, w_uk_l,
                                       cos, sin, tm=tm)
            if x is xp:
                o_mla = attention(q, kb, w_uv_l, batch=batch, lq=seq, lk=seq,
                                  tq=min(ATTN_Q_TILE, seq), kb=ATTN_KEY_BLOCK, causal=True)
                o_hg, st = hgrn(p, lb_l, _row(hg_norm_w[l]), zero_state, batch=batch, seq=seq,
                                chunk=CHUNK, tb=min(HG_BLOCK, seq))
            else:
                past_k = jnp.concatenate(
                    [cache_ckv[l].astype(BF16), cache_kpe[l].astype(BF16),
                     jnp.zeros((batch, past, LANE - QK_ROPE), BF16)], axis=-1)
                keys = jnp.concatenate([past_k, kb.reshape(batch, seq, QK_PAD)], axis=1)
                lk = past + seq
                o_mla = attention(q, keys.reshape(batch * lk, QK_PAD), w_uv_l, batch=batch, lq=seq, lk=lk,
                                  tq=seq, kb=lk, causal=False)
                n_seq = math.gcd(batch, 8)
                o_hg, st = hgrn(p, lb_l, _row(hg_norm_w[l]), state_hgrn[l], batch=batch, seq=seq,
                                chunk=seq, tb=n_seq * seq, n_seq=n_seq)
            x1 = out_proj(x, o_mla, o_hg, w_out_l, tm=tm)
            mixers.append((x1, ckv.reshape(batch, seq, KV_LORA), kpe.reshape(batch, seq, QK_ROPE), st))
        (xp, ckv_p, kpe_p, st_p), (xs, ckv_s, kpe_s, st_s) = mixers
        for name, val in (("ckv_p", ckv_p), ("kpe_p", kpe_p), ("st_p", st_p),
                          ("ckv_s", ckv_s), ("kpe_s", kpe_s), ("st_s", st_s)):
            outs[name].append(val)

        i = l // 2
        if l % 2 == 0:
            wg, wu, wd = w_gate[i].astype(BF16), w_up[i].astype(BF16), w_down[i].astype(BF16)
            xp = ffn(xp, _row(ffn_norm_w[l]), wg, wu, wd, tm=min(FFN_ROW_TILE, n_p), tf=FFN_CHUNK)
            xs = ffn(xs, _row(ffn_norm_w[l]), wg, wu, wd, tm=min(FFN_ROW_TILE, n_s), tf=FFN_CHUNK)
        else:
            wg, wu, wd = we_gate[i].astype(BF16), we_up[i].astype(BF16), we_down[i].astype(BF16)
            xp, xs = moe_final([xp, xs], _row(ffn_norm_w[l]), w_router[i], wg, wu, wd, _row(final_norm_w))

    return (xp.reshape(b_p, l_p, d), xs.reshape(b_s, l_s, d),
            jnp.stack(outs["ckv_p"]), jnp.stack(outs["kpe_p"]), jnp.stack(outs["st_p"]),
            jnp.stack(outs["ckv_s"]), jnp.stack(outs["kpe_s"]), jnp.stack(outs["st_s"]))
```
